```python
import math
import jax
import jax.numpy as jnp
from jax import lax
import numpy as np

D_MODEL = 1024
BATCH = 16
SEQ = 4096
DEPTH = 2
DEC_BATCH = 8
DEC_SEQ = 2048
PAST_LEN = 128

EPS = 1e-6
NEG_INF = -1e30
HEADS_A = 8
HEAD_DIM_A = 64
DILATED_PATTERNS = ((128, 1), (512, 4), (2048, 16))
N_GROUPS_A = 3
WIDTH_A = HEADS_A * HEAD_DIM_A
ALIBI_MAX = 8.0
HEADS_B = 8
NOPE_B = 64
ROPE_B = 32
QK_B = NOPE_B + ROPE_B
V_B = 64
Q_RANK = 256
KV_RANK = 128
ROPE_THETA = 10000.0
Q_BLOCK = 128
WIDTH_B = HEADS_B * V_B
WIDTH_C = 512
HYENA_EMB = 33
HYENA_HID = 64
HYENA_FAST_DECAY = 0.3
HYENA_SLOW_DECAY = 1.5
HYENA_TARGET = 1e-2
D_FF = 2816
N_BRANCHES = 3
COLS_A = 3 * N_GROUPS_A * WIDTH_A
COLS_B = Q_RANK + KV_RANK + ROPE_B
COLS_C = 3 * WIDTH_C
COLS_G = N_BRANCHES * D_MODEL
IN_COLS = COLS_A + COLS_B + COLS_C + COLS_G

kernel_name = 'hybrid_bidir_encoder_gated_merge'

F32 = jnp.float32


def rmsnorm(x, g):
    x32 = x.astype(F32)
    y = x32 * lax.rsqrt(jnp.mean(x32 * x32, axis=-1, keepdims=True) + EPS)
    return (y * g.astype(F32)).astype(x.dtype)


def dwconv3(x, w, b):
    xp = jnp.pad(x, ((0, 0), (1, 1), (0, 0)))
    return xp[:, :-2] * w[0] + xp[:, 1:-1] * w[1] + xp[:, 2:] * w[2] + b


def rope(x, pos):
    half = x.shape[-1] // 2
    inv = ROPE_THETA ** (-jnp.arange(half, dtype=F32) / half)
    ang = pos[:, None] * inv[None, :]
    cos = jnp.cos(ang)[None, :, None, :]
    sin = jnp.sin(ang)[None, :, None, :]
    x32 = x.astype(F32)
    x1, x2 = x32[..., :half], x32[..., half:]
    return jnp.concatenate([x1 * cos - x2 * sin, x1 * sin + x2 * cos], axis=-1).astype(x.dtype)


def dilated_group(q, k, v, dil, half, slopes):
    b, L, h, e = q.shape
    lu = L // dil
    nb = -(-lu // half)
    lp = nb * half

    def by_residue(a, front, back):
        a = a.reshape(b, lu, dil, h, a.shape[-1])
        return jnp.pad(a, ((0, 0), (front, back), (0, 0), (0, 0), (0, 0)))

    qb = by_residue(q, 0, lp - lu).reshape(b, nb, half, dil, h, e)

    def windows(a):
        ap = by_residue(a, half, lp - lu + half).reshape(b, nb + 2, half, dil, h, a.shape[-1])
        return jnp.concatenate([ap[:, :-2], ap[:, 1:-1], ap[:, 2:]], axis=2)

    kw = windows(k)
    vw = windows(v)
    s = jnp.einsum('bnidhe,bnjdhe->bndhij', qb, kw).astype(F32) * (e ** -0.5)
    i = jnp.arange(half)
    j = jnp.arange(3 * half)
    rel = j[None, :] - half - i[:, None]
    key_u = jnp.arange(nb)[:, None, None] * half + j[None, None, :] - half
    valid = (jnp.abs(rel) <= half)[None] & (key_u >= 0) & (key_u < lu)
    alibi = -slopes[:, None, None] * (dil * jnp.abs(rel)).astype(F32)[None]
    s = jnp.where(valid[None, :, None, None], s + alibi, NEG_INF)
    m = jnp.max(s, axis=-1, keepdims=True)
    p = jnp.exp(s - m)
    den = jnp.sum(p, axis=-1)
    o = jnp.einsum('bndhij,bnjdhe->bnidhe', p, vw.astype(F32))
    o = o / den.transpose(0, 1, 4, 2, 3)[..., None]
    lse = (m[..., 0] + jnp.log(den)).transpose(0, 1, 4, 2, 3)
    o = o.reshape(b, lp, dil, h, e)[:, :lu].reshape(b, L, h, e)
    lse = lse.reshape(b, lp, dil, h)[:, :lu].reshape(b, L, h)
    return o, lse


def dilated_mixer(q, k, v, q_g, k_g):
    b, L = q.shape[:2]
    slopes = jnp.exp2(-ALIBI_MAX * (jnp.arange(HEADS_A, dtype=F32) + 1.0) / HEADS_A)
    outs = []
    lses = []
    for g, (win, dil) in enumerate(DILATED_PATTERNS):
        o, lse = dilated_group(rmsnorm(q[:, :, g], q_g[g]), rmsnorm(k[:, :, g], k_g[g]), v[:, :, g],
                               dil, win // (2 * dil), slopes)
        outs.append(o)
        lses.append(lse)
    alpha = jax.nn.softmax(jnp.stack(lses, axis=2), axis=2)
    o = jnp.sum(jnp.stack(outs, axis=2) * alpha[..., None], axis=2)
    return o.reshape(b, L, WIDTH_A).astype(q.dtype)


def dense_attention(q, k, v):
    b, L, h, e = q.shape
    nq = L // Q_BLOCK
    qb = q.reshape(b, nq, Q_BLOCK, h, e).transpose(1, 0, 2, 3, 4)

    def block(qi):
        s = jnp.einsum('bqhe,bkhe->bhqk', qi, k).astype(F32) * (e ** -0.5)
        p = jax.nn.softmax(s, axis=-1)
        return jnp.einsum('bhqk,bkhd->bqhd', p.astype(v.dtype), v)

    o = lax.map(block, qb)
    return o.transpose(1, 0, 2, 3, 4).reshape(b, L, h, v.shape[-1])


def latent_attention(c_q, c_kv, k_pe, q_lat_g, kv_lat_g, w_uq, w_ukv, q_g, k_g):
    b, L, _ = c_q.shape
    pos = jnp.arange(L, dtype=F32)
    q = (rmsnorm(c_q, q_lat_g) @ w_uq).reshape(b, L, HEADS_B, QK_B)
    kv = (rmsnorm(c_kv, kv_lat_g) @ w_ukv).reshape(b, L, HEADS_B, NOPE_B + V_B)
    k = jnp.concatenate([kv[..., :NOPE_B], jnp.broadcast_to(k_pe[:, :, None, :], (b, L, HEADS_B, ROPE_B))], axis=-1)
    v = kv[..., NOPE_B:]
    q = rmsnorm(q, q_g)
    k = rmsnorm(k, k_g)
    q = jnp.concatenate([q[..., :NOPE_B], rope(q[..., NOPE_B:], pos)], axis=-1)
    k = jnp.concatenate([k[..., :NOPE_B], rope(k[..., NOPE_B:], pos)], axis=-1)
    return dense_attention(q, k, v).reshape(b, L, WIDTH_B)


def hyena_filters(L, w1, b1, w2, b2, w3, freq):
    t = jnp.linspace(0.0, 1.0, L, dtype=F32)[:, None]
    bands = (HYENA_EMB - 1) // 2
    ang = (2.0 * math.pi / L) * jnp.arange(L, dtype=F32)[:, None]
    f = jnp.linspace(1e-4, bands - 1, bands, dtype=F32)[None, :]
    z = jnp.concatenate([t, jnp.cos(f * ang), -jnp.sin(f * ang)], axis=-1)
    fr = freq.astype(F32)
    hid = jnp.sin(fr * (z @ w1.astype(F32) + b1.astype(F32)))
    hid = jnp.sin(fr * (hid @ w2.astype(F32) + b2.astype(F32)))
    filt = hid @ w3.astype(F32)
    deltas = jnp.abs(jnp.linspace(math.log(HYENA_TARGET) / HYENA_FAST_DECAY,
                                  math.log(HYENA_TARGET) / HYENA_SLOW_DECAY, WIDTH_C, dtype=F32))
    filt = filt * jnp.exp(-t * jnp.tile(deltas, 2)[None, :])
    return filt[:, :WIDTH_C], filt[:, WIDTH_C:]


def bidir_long_conv(u, h_fwd, h_bwd):
    L, c = h_fwd.shape
    kfull = jnp.concatenate([h_fwd, jnp.zeros((1, c), F32), h_bwd[1:][::-1]], axis=0)
    kf = jnp.fft.rfft(kfull, n=2 * L, axis=0)
    uf = jnp.fft.rfft(u.astype(F32), n=2 * L, axis=1)
    return jnp.fft.irfft(uf * kf[None], n=2 * L, axis=1)[:, :L]


def hyena_mixer(pc, conv_w, conv_b, w1, b1, w2, b2, w3, freq, skip):
    L = pc.shape[1]
    x0, x1, v = jnp.split(dwconv3(pc, conv_w, conv_b), 3, axis=-1)
    u = v * x1
    hf, hb = hyena_filters(L, w1, b1, w2, b2, w3, freq)
    y = bidir_long_conv(u, hf, hb).astype(u.dtype) + skip * u
    return x0 * y


def encoder_layer(x, P, l):
    b, L, _ = x.shape
    h = rmsnorm(x, P['norm_attn_g'][l])
    proj = h @ P['w_in'][l]
    pa, pb, pc, pg = jnp.split(proj, [COLS_A, COLS_A + COLS_B, COLS_A + COLS_B + COLS_C], axis=-1)
    pa = pa.reshape(b, L, 3, N_GROUPS_A, HEADS_A, HEAD_DIM_A)
    y_a = dilated_mixer(pa[:, :, 0], pa[:, :, 1], pa[:, :, 2], P['a_q_g'][l], P['a_k_g'][l])
    c_q, c_kv, k_pe = jnp.split(pb, [Q_RANK, Q_RANK + KV_RANK], axis=-1)
    y_b = latent_attention(c_q, c_kv, k_pe, P['b_q_lat_g'][l], P['b_kv_lat_g'][l], P['b_w_uq'][l],
                           P['b_w_ukv'][l], P['b_q_g'][l], P['b_k_g'][l])
    y_c = hyena_mixer(pc, P['c_conv_w'][l], P['c_conv_b'][l], P['c_w1'][l], P['c_b1'][l], P['c_w2'][l],
                      P['c_b2'][l], P['c_w3'][l], P['c_freq'][l], P['c_skip'][l])
    gates = jax.nn.sigmoid(pg + P['b_gate'][l]).reshape(b, L, N_BRANCHES, D_MODEL)
    mixed = (gates[:, :, 0] * (y_a @ P['w_br_a'][l])
             + gates[:, :, 1] * (y_b @ P['w_br_b'][l])
             + gates[:, :, 2] * (y_c @ P['w_br_c'][l]))
    x = x + mixed @ P['w_out'][l]
    h2 = rmsnorm(x, P['norm_ffn_g'][l])
    u = dwconv3(h2 @ P['w_up'][l], P['ffn_conv_w'][l], P['ffn_conv_b'][l])
    a, g = jnp.split(u, 2, axis=-1)
    return x + (jax.nn.gelu(a) * g) @ P['w_down'][l]


def setup_inputs(seed: int = 0) -> dict:
    key = jax.random.key(seed)
    ks = jax.random.split(key, 32)
    counter = [0]

    def nxt():
        counter[0] += 1
        return ks[counter[0] - 1]

    def nrm(shape, scale):
        return scale * jax.random.normal(nxt(), shape, F32)

    def gain(shape):
        return 1.0 + 0.1 * jax.random.normal(nxt(), shape, F32)

    return {
        'x_prompt': nrm((BATCH, SEQ, D_MODEL), 1.0),
        'x_sample': nrm((DEC_BATCH, DEC_SEQ, D_MODEL), 1.0),
        'norm_attn_g': gain((DEPTH, D_MODEL)),
        'w_in': nrm((DEPTH, D_MODEL, IN_COLS), D_MODEL ** -0.5),
        'b_gate': nrm((DEPTH, COLS_G), 0.02),
        'a_q_g': gain((DEPTH, N_GROUPS_A, HEAD_DIM_A)),
        'a_k_g': gain((DEPTH, N_GROUPS_A, HEAD_DIM_A)),
        'b_q_lat_g': gain((DEPTH, Q_RANK)),
        'b_kv_lat_g': gain((DEPTH, KV_RANK)),
        'b_w_uq': nrm((DEPTH, Q_RANK, HEADS_B * QK_B), Q_RANK ** -0.5),
        'b_w_ukv': nrm((DEPTH, KV_RANK, HEADS_B * (NOPE_B + V_B)), KV_RANK ** -0.5),
        'b_q_g': gain((DEPTH, QK_B)),
        'b_k_g': gain((DEPTH, QK_B)),
        'c_conv_w': nrm((DEPTH, 3, COLS_C), 3 ** -0.5),
        'c_conv_b': nrm((DEPTH, COLS_C), 0.02),
        'c_w1': nrm((DEPTH, HYENA_EMB, HYENA_HID), HYENA_EMB ** -0.5),
        'c_b1': nrm((DEPTH, HYENA_HID), 0.1),
        'c_w2': nrm((DEPTH, HYENA_HID, HYENA_HID), HYENA_HID ** -0.5),
        'c_b2': nrm((DEPTH, HYENA_HID), 0.1),
        'c_w3': nrm((DEPTH, HYENA_HID, 2 * WIDTH_C), 0.1 * HYENA_HID ** -0.5),
        'c_freq': gain((DEPTH, HYENA_HID)),
        'c_skip': nrm((DEPTH, WIDTH_C), 0.5),
        'w_br_a': nrm((DEPTH, WIDTH_A, D_MODEL), WIDTH_A ** -0.5),
        'w_br_b': nrm((DEPTH, WIDTH_B, D_MODEL), WIDTH_B ** -0.5),
        'w_br_c': nrm((DEPTH, WIDTH_C, D_MODEL), WIDTH_C ** -0.5),
        'w_out': nrm((DEPTH, D_MODEL, D_MODEL), D_MODEL ** -0.5),
        'norm_ffn_g': gain((DEPTH, D_MODEL)),
        'w_up': nrm((DEPTH, D_MODEL, 2 * D_FF), D_MODEL ** -0.5),
        'ffn_conv_w': nrm((DEPTH, 3, 2 * D_FF), 3 ** -0.5),
        'ffn_conv_b': nrm((DEPTH, 2 * D_FF), 0.02),
        'w_down': nrm((DEPTH, D_FF, D_MODEL), D_FF ** -0.5),
    }


def reference(x_prompt, x_sample, norm_attn_g, w_in, b_gate, a_q_g, a_k_g, b_q_lat_g, b_kv_lat_g, b_w_uq,
              b_w_ukv, b_q_g, b_k_g, c_conv_w, c_conv_b, c_w1, c_b1, c_w2, c_b2, c_w3, c_freq, c_skip,
              w_br_a, w_br_b, w_br_c, w_out, norm_ffn_g, w_up, ffn_conv_w, ffn_conv_b, w_down):
    P = {
        'norm_attn_g': norm_attn_g, 'w_in': w_in, 'b_gate': b_gate, 'a_q_g': a_q_g, 'a_k_g': a_k_g,
        'b_q_lat_g': b_q_lat_g, 'b_kv_lat_g': b_kv_lat_g, 'b_w_uq': b_w_uq, 'b_w_ukv': b_w_ukv,
        'b_q_g': b_q_g, 'b_k_g': b_k_g, 'c_conv_w': c_conv_w, 'c_conv_b': c_conv_b, 'c_w1': c_w1,
        'c_b1': c_b1, 'c_w2': c_w2, 'c_b2': c_b2, 'c_w3': c_w3, 'c_freq': c_freq, 'c_skip': c_skip,
        'w_br_a': w_br_a, 'w_br_b': w_br_b, 'w_br_c': w_br_c, 'w_out': w_out, 'norm_ffn_g': norm_ffn_g,
        'w_up': w_up, 'ffn_conv_w': ffn_conv_w, 'ffn_conv_b': ffn_conv_b, 'w_down': w_down,
    }
    y_prompt = x_prompt
    for l in range(DEPTH):
        y_prompt = encoder_layer(y_prompt, P, l)
    y_sample = x_sample
    for l in range(DEPTH):
        y_sample = encoder_layer(y_sample, P, l)
    return (y_prompt, y_sample)
```

```python
import functools
import math

import jax
import jax.numpy as jnp
from jax import lax
from jax.experimental import pallas as pl
from jax.experimental.pallas import tpu as pltpu

F32 = jnp.float32
BF16 = jnp.bfloat16

D_MODEL = 1024
DEPTH = 2
EPS = 1e-6
HEADS_A = 8
HEAD_DIM_A = 64
DILATIONS = (1, 4, 16)
HALF_A = 64
WIDTH_A = HEADS_A * HEAD_DIM_A
ALIBI_MAX = 8.0
HEADS_B = 8
NOPE_B = 64
ROPE_B = 32
QK_B = NOPE_B + ROPE_B
V_B = 64
Q_RANK = 256
KV_RANK = 128
ROPE_THETA = 10000.0
WIDTH_B = HEADS_B * V_B
WIDTH_C = 512
HYENA_EMB = 33
HYENA_HID = 64
HYENA_FAST_DECAY = 0.3
HYENA_SLOW_DECAY = 1.5
HYENA_TARGET = 1e-2
D_FF = 2816
COLS_A = 3 * 3 * WIDTH_A
COLS_B = Q_RANK + KV_RANK + ROPE_B
COLS_C = 3 * WIDTH_C
COLS_G = 3 * D_MODEL
OFF_A = 0
OFF_C = COLS_A
OFF_G = OFF_C + COLS_C
OFF_B = OFF_G + COLS_G
PROJ_W = OFF_B + 512
LANE = 128
SLOT_B = 128
VROWS = 80
TOEP = 256
MASKED = 1e32
VMEM_LIMIT = 48 * 1024 * 1024


def _cparams(sem):
    return pltpu.CompilerParams(dimension_semantics=sem, vmem_limit_bytes=VMEM_LIMIT)


def _nt_dot(a, b):
    return lax.dot_general(a, b, (((1,), (1,)), ((), ())), preferred_element_type=F32)


def _dot(a, b):
    return jnp.dot(a, b, preferred_element_type=F32)


def _dot_exact(a, b):
    return jnp.dot(a, b, preferred_element_type=F32, precision=lax.Precision.HIGHEST)


def _inproj_kernel(x_ref, g_ref, w_ref, o_ref, h_ref):
    @pl.when(pl.program_id(1) == 0)
    def _():
        x = x_ref[...]
        ms = jnp.mean(x * x, axis=-1, keepdims=True)
        h_ref[...] = (x * lax.rsqrt(ms + EPS) * g_ref[...]).astype(BF16)

    o_ref[...] = _dot(h_ref[...], w_ref[...]).astype(o_ref.dtype)


def _inproj(x2d, gain, w_r):
    t = x2d.shape[0]
    tm, tn = 1024, PROJ_W // 4
    return pl.pallas_call(
        _inproj_kernel,
        grid=(t // tm, PROJ_W // tn),
        in_specs=[
            pl.BlockSpec((tm, D_MODEL), lambda i, j: (i, 0)),
            pl.BlockSpec((1, D_MODEL), lambda i, j: (0, 0)),
            pl.BlockSpec((D_MODEL, tn), lambda i, j: (0, j)),
        ],
        out_specs=pl.BlockSpec((tm, tn), lambda i, j: (i, j)),
        out_shape=jax.ShapeDtypeStruct((t, PROJ_W), BF16),
        scratch_shapes=[pltpu.VMEM((tm, D_MODEL), BF16)],
        compiler_params=_cparams(("parallel", "arbitrary")),
        name="inproj",
    )(x2d, gain, w_r)


def _dil_attn_kernel(negc_ref, q_ref, k_ref, v_ref, qg_ref, kg_ref, o_ref, lse_ref, qn_ref, kn_ref, *, lu):
    hb = pl.program_id(2)
    cw = q_ref.shape[-1]
    win = min(2 * LANE, lu)
    row = lax.broadcasted_iota(jnp.int32, (cw, cw), 0) // HEAD_DIM_A
    col = lax.broadcasted_iota(jnp.int32, (cw, cw), 1) // HEAD_DIM_A
    seg_ones = jnp.where(row == col, 1.0, 0.0).astype(BF16)

    chunk = min(512, lu)

    def norm_body(c, carry):
        r0 = pl.multiple_of(c * chunk, chunk)
        for src, gain, dst in ((q_ref, qg_ref, qn_ref), (k_ref, kg_ref, kn_ref)):
            x = src[0, pl.ds(r0, chunk), :].astype(F32)
            ssq = _dot((x * x).astype(BF16), seg_ones)
            dst[pl.ds(r0, chunk), :] = (x * lax.rsqrt(ssq * (1.0 / HEAD_DIM_A) + EPS) * gain[...]).astype(BF16)
        return carry

    lax.fori_loop(0, lu // chunk, norm_body, 0)

    lane_lo = lax.broadcasted_iota(jnp.int32, (1, LANE), 1) < HEAD_DIM_A
    i_q = lax.broadcasted_iota(jnp.int32, (LANE, win), 0)
    j_k = lax.broadcasted_iota(jnp.int32, (LANE, win), 1)
    lane_id = lax.broadcasted_iota(jnp.int32, (LANE, LANE), 1)

    def tile_body(t, carry):
        q0 = pl.multiple_of(t * LANE, LANE)
        w0 = pl.multiple_of(jnp.clip(q0 - HALF_A, 0, lu - win), HALF_A)
        q = qn_ref[pl.ds(q0, LANE), :]
        kw = kn_ref[pl.ds(w0, win), :]
        vw = v_ref[0, pl.ds(w0, win), :]
        absrel = jnp.abs(j_k - i_q + (w0 - q0)).astype(F32)
        absrel = jnp.where(absrel <= float(HALF_A), absrel, MASKED)
        lse_tile = jnp.zeros((LANE, LANE), F32)
        outs = []
        for pair in range(cw // LANE):
            sl = slice(pair * LANE, (pair + 1) * LANE)
            qp, kp, vp = q[:, sl], kw[:, sl], vw[:, sl]
            zero = jnp.zeros_like(qp)
            qs = jnp.concatenate([jnp.where(lane_lo, qp, zero), jnp.where(lane_lo, zero, qp)], axis=0)
            s = _nt_dot(qs, kp)
            vzero = jnp.zeros_like(vp)
            o_pair = jnp.zeros((LANE, LANE), F32)
            for a in range(2):
                head = 2 * pair + a
                sa = s[a * LANE:(a + 1) * LANE] + absrel * negc_ref[4 * hb + head]
                m = jnp.max(sa, axis=1, keepdims=True)
                p = jnp.exp(sa - m)
                den = jnp.sum(p, axis=1, keepdims=True)
                va = jnp.where(lane_lo, vp, vzero) if a == 0 else jnp.where(lane_lo, vzero, vp)
                o_pair = o_pair + _dot(p.astype(BF16), va) * (1.0 / den)
                lse_tile = jnp.where(lane_id == head, m + jnp.log(den), lse_tile)
            outs.append(o_pair)
        o_ref[0, pl.ds(q0, LANE), :] = jnp.concatenate(outs, axis=1).astype(o_ref.dtype)
        lse_ref[0, pl.ds(q0, LANE), :] = lse_tile
        return carry

    lax.fori_loop(0, lu // LANE, tile_body, 0)


def _mixer_a_group(proj, qg, kg, negc, b, l, g, dil):
    lu = l // dil
    pv = proj.reshape(b, lu, dil * PROJ_W)
    nblk = PROJ_W // 256
    cq, ck, cv = (OFF_A + (0 + g) * 512) // 256, (OFF_A + (3 + g) * 512) // 256, (OFF_A + (6 + g) * 512) // 256

    def in_spec(c0):
        return pl.BlockSpec((1, lu, 256), lambda bi, r, hb: (bi, 0, r * nblk + c0 + hb))

    o, lse = pl.pallas_call(
        functools.partial(_dil_attn_kernel, lu=lu),
        grid=(b, dil, 2),
        in_specs=[
            pl.BlockSpec(memory_space=pltpu.SMEM),
            in_spec(cq), in_spec(ck), in_spec(cv),
            pl.BlockSpec((1, 256), lambda bi, r, hb: (0, 0)),
            pl.BlockSpec((1, 256), lambda bi, r, hb: (0, 0)),
        ],
        out_specs=[
            pl.BlockSpec((1, lu, 256), lambda bi, r, hb: (bi, 0, r * 2 + hb)),
            pl.BlockSpec((1, lu, LANE), lambda bi, r, hb: (bi, 0, r * 2 + hb)),
        ],
        out_shape=[
            jax.ShapeDtypeStruct((b, lu, dil * WIDTH_A), BF16),
            jax.ShapeDtypeStruct((b, lu, dil * 2 * LANE), F32),
        ],
        scratch_shapes=[pltpu.VMEM((lu, 256), BF16), pltpu.VMEM((lu, 256), BF16)],
        compiler_params=_cparams(("parallel", "parallel", "arbitrary")),
        name=f"dilated_attention_g{g}",
    )(negc, pv, pv, pv, qg, kg)
    return o.reshape(b * l, WIDTH_A), lse.reshape(b * l, 2 * LANE)


def _mla_prep_kernel(pb_ref, gql_ref, gkl_ref, wq_ref, wk_ref, wv_ref, gq_ref, gk_ref, rc_ref, ra_ref, rb_ref,
                     q_ref, k_ref, vt_ref):
    c = pb_ref[...].astype(F32)
    cq = c[:, :Q_RANK]
    cqn = (cq * lax.rsqrt(jnp.mean(cq * cq, axis=-1, keepdims=True) + EPS) * gql_ref[...]).astype(BF16)
    ckv = c[:, Q_RANK:Q_RANK + KV_RANK]
    ckvn = (ckv * lax.rsqrt(jnp.mean(ckv * ckv, axis=-1, keepdims=True) + EPS) * gkl_ref[...]).astype(BF16)
    q = _dot(cqn, wq_ref[...])
    k = _dot(jnp.concatenate([ckvn, pb_ref[:, Q_RANK + KV_RANK:]], axis=1), wk_ref[...])
    v = _dot(ckvn, wv_ref[...])
    rc, ra, rb = rc_ref[...], ra_ref[...], rb_ref[...]

    def finish(x, gain_ref, dst):
        for h in range(HEADS_B):
            sl = slice(h * SLOT_B, (h + 1) * SLOT_B)
            xh = x[:, sl]
            ssq = jnp.sum(xh * xh, axis=-1, keepdims=True)
            xn = xh * lax.rsqrt(ssq * (1.0 / QK_B) + EPS) * gain_ref[:, sl]
            half = ROPE_B // 2
            xr = xn * rc + pltpu.roll(xn, SLOT_B - half, 1) * ra + pltpu.roll(xn, half, 1) * rb
            dst[:, sl] = xr.astype(dst.dtype)

    finish(q, gq_ref, q_ref)
    finish(k, gk_ref, k_ref)
    vt_ref[0] = v.T.astype(vt_ref.dtype)


def _mla_prep(proj, gql, gkl, wq, wk, wv, gq, gk, rc, ra, rb, b, l):
    t = b * l
    tm = 512
    per_seq = l // tm
    const = lambda i: (0, 0)
    return pl.pallas_call(
        _mla_prep_kernel,
        grid=(t // tm,),
        in_specs=[
            pl.BlockSpec((tm, 512), lambda i: (i, OFF_B // 512)),
            pl.BlockSpec((1, Q_RANK), const), pl.BlockSpec((1, KV_RANK), const),
            pl.BlockSpec((Q_RANK, HEADS_B * SLOT_B), const),
            pl.BlockSpec((256, HEADS_B * SLOT_B), const),
            pl.BlockSpec((KV_RANK, WIDTH_B), const),
            pl.BlockSpec((1, HEADS_B * SLOT_B), const), pl.BlockSpec((1, HEADS_B * SLOT_B), const),
            pl.BlockSpec((tm, SLOT_B), lambda i: (i % per_seq, 0)),
            pl.BlockSpec((tm, SLOT_B), lambda i: (i % per_seq, 0)),
            pl.BlockSpec((tm, SLOT_B), lambda i: (i % per_seq, 0)),
        ],
        out_specs=[
            pl.BlockSpec((tm, HEADS_B * SLOT_B), lambda i: (i, 0)),
            pl.BlockSpec((tm, HEADS_B * SLOT_B), lambda i: (i, 0)),
            pl.BlockSpec((1, WIDTH_B, tm), lambda i: (i // per_seq, 0, i % per_seq)),
        ],
        out_shape=[
            jax.ShapeDtypeStruct((t, HEADS_B * SLOT_B), BF16),
            jax.ShapeDtypeStruct((t, HEADS_B * SLOT_B), BF16),
            jax.ShapeDtypeStruct((b, WIDTH_B, l), BF16),
        ],
        compiler_params=_cparams(("parallel",)),
        name="latent_prep",
    )(proj, gql, gkl, wq, wk, wv, gq, gk, rc, ra, rb)


def _mla_attn_kernel(q_ref, k_ref, vt_ref, o_ref, vaug_ref, *, tk):
    nkv = vaug_ref.shape[1]
    tq = q_ref.shape[1]

    @pl.when(pl.program_id(2) == 0)
    def _():
        ones_rows = jnp.where(lax.broadcasted_iota(jnp.int32, (VROWS - V_B, tk), 0) == 0, 1.0, 0.0).astype(BF16)
        for a in range(2):
            for j in range(nkv):
                vaug_ref[a, j, 0:V_B, :] = vt_ref[0, a * V_B:(a + 1) * V_B, j * tk:(j + 1) * tk]
                vaug_ref[a, j, V_B:VROWS, :] = ones_rows

    outs = []
    for a in range(2):
        qh = q_ref[0, :, a * SLOT_B:(a + 1) * SLOT_B]

        def body(j, carry, a=a, qh=qh):
            m, acc = carry
            k0 = pl.multiple_of(j * tk, tk)
            kb = k_ref[0, pl.ds(k0, tk), a * SLOT_B:(a + 1) * SLOT_B]
            st = _nt_dot(kb, qh)
            mn = jnp.maximum(m, jnp.max(st, axis=0, keepdims=True))
            p = jnp.exp(st - mn).astype(BF16)
            acc = acc * jnp.exp(m - mn) + _dot(vaug_ref[a, j], p)
            return mn, acc

        m0 = jnp.full((1, tq), -jnp.inf, F32)
        _, acc = lax.fori_loop(0, nkv, body, (m0, jnp.zeros((VROWS, tq), F32)))
        outs.append(acc[0:V_B] * (1.0 / acc[V_B:V_B + 1]))
    o_ref[0] = jnp.concatenate(outs, axis=0).T.astype(o_ref.dtype)


def _mla_attn(qp, kp, vt, b, l):
    tq, tk = 256, 512
    q3 = qp.reshape(b, l, HEADS_B * SLOT_B)
    k3 = kp.reshape(b, l, HEADS_B * SLOT_B)
    out = pl.pallas_call(
        functools.partial(_mla_attn_kernel, tk=tk),
        grid=(b, HEADS_B // 2, l // tq),
        in_specs=[
            pl.BlockSpec((1, tq, 2 * SLOT_B), lambda bi, p, qi: (bi, qi, p)),
            pl.BlockSpec((1, l, 2 * SLOT_B), lambda bi, p, qi: (bi, 0, p)),
            pl.BlockSpec((1, 2 * V_B, l), lambda bi, p, qi: (bi, p, 0)),
        ],
        out_specs=pl.BlockSpec((1, tq, 2 * V_B), lambda bi, p, qi: (bi, qi, p)),
        out_shape=jax.ShapeDtypeStruct((b, l, WIDTH_B), BF16),
        scratch_shapes=[pltpu.VMEM((2, l // tk, VROWS, tk), BF16)],
        compiler_params=_cparams(("parallel", "parallel", "arbitrary")),
        name="latent_attention",
    )(q3, k3, vt)
    return out.reshape(b * l, WIDTH_B)


def _shift_rows(x, prev_row, next_row):
    n = x.shape[0]
    rid = lax.broadcasted_iota(jnp.int32, x.shape, 0)
    xm = jnp.where(rid == 0, prev_row, pltpu.roll(x, 1, 0))
    xp = jnp.where(rid == n - 1, next_row, pltpu.roll(x, n - 1, 0))
    return xm, xp


def _hyena_prep_kernel(pc_ref, prev_ref, next_ref, w_ref, b_ref, u_ref, x0_ref, *, per_seq):
    i = pl.program_id(0) % per_seq
    x = pc_ref[...].astype(F32)
    hr = prev_ref.shape[0]
    prev_row = jnp.where(i == 0, 0.0, prev_ref[...].astype(F32)[hr - 1:hr, :])
    next_row = jnp.where(i == per_seq - 1, 0.0, next_ref[...].astype(F32)[0:1, :])
    xm, xp = _shift_rows(x, prev_row, next_row)
    y = xm * w_ref[0:1, :] + x * w_ref[1:2, :] + xp * w_ref[2:3, :] + b_ref[...]
    x0_ref[...] = y[:, :WIDTH_C].astype(x0_ref.dtype)
    u_ref[...] = (y[:, 2 * WIDTH_C:] * y[:, WIDTH_C:2 * WIDTH_C]).astype(u_ref.dtype)


def _hyena_prep(proj, conv_w, conv_b, b, l):
    t = b * l
    tm, hr = 512, 16
    per_seq = l // tm
    nh = tm // hr
    cb = OFF_C // COLS_C
    return pl.pallas_call(
        functools.partial(_hyena_prep_kernel, per_seq=per_seq),
        grid=(t // tm,),
        in_specs=[
            pl.BlockSpec((tm, COLS_C), lambda i: (i, cb)),
            pl.BlockSpec((hr, COLS_C), lambda i: (jnp.maximum(i * nh - 1, 0), cb)),
            pl.BlockSpec((hr, COLS_C), lambda i: (jnp.minimum((i + 1) * nh, t // hr - 1), cb)),
            pl.BlockSpec((3, COLS_C), lambda i: (0, 0)),
            pl.BlockSpec((1, COLS_C), lambda i: (0, 0)),
        ],
        out_specs=[pl.BlockSpec((tm, WIDTH_C), lambda i: (i, 0)), pl.BlockSpec((tm, WIDTH_C), lambda i: (i, 0))],
        out_shape=[jax.ShapeDtypeStruct((t, WIDTH_C), BF16), jax.ShapeDtypeStruct((t, WIDTH_C), BF16)],
        compiler_params=_cparams(("parallel",)),
        name="hyena_prep",
    )(proj, proj, proj, conv_w, conv_b)


def _hyena_filter_kernel(pos_ref, fcol_ref, w1t_ref, w1c_ref, w1s_ref, b1_ref, w2_ref, b2_ref, w3_ref, fr_ref,
                         dl_ref, o_ref):
    tt = pos_ref[0:1, :]
    ang = pos_ref[1:2, :]
    valid = pos_ref[2:3, :]
    arg = fcol_ref[...] * ang
    pre1 = w1t_ref[...] * tt + _dot_exact(w1c_ref[...], jnp.cos(arg)) + _dot_exact(w1s_ref[...], jnp.sin(arg))
    fr = fr_ref[...]
    hid = jnp.sin(fr * (pre1 + b1_ref[...]))
    hid = jnp.sin(fr * (_dot_exact(w2_ref[...], hid) + b2_ref[...]))
    filt = _dot_exact(w3_ref[0], hid)
    o_ref[...] = filt * jnp.exp(-dl_ref[...] * tt) * valid


def _hyena_filter(pos, fcol, w1t, w1c, w1s, b1, w2t, b2, w3t, fr, dl, l):
    n = 2048
    const = lambda j: (0, 0)
    return pl.pallas_call(
        _hyena_filter_kernel,
        grid=(2 * l // n,),
        in_specs=[
            pl.BlockSpec((3, n), lambda j: (0, j)),
            pl.BlockSpec((16, 1), const),
            pl.BlockSpec((HYENA_HID, 1), const), pl.BlockSpec((HYENA_HID, 16), const),
            pl.BlockSpec((HYENA_HID, 16), const), pl.BlockSpec((HYENA_HID, 1), const),
            pl.BlockSpec((HYENA_HID, HYENA_HID), const), pl.BlockSpec((HYENA_HID, 1), const),
            pl.BlockSpec((1, WIDTH_C, HYENA_HID), lambda j: (j // (l // n), 0, 0)),
            pl.BlockSpec((HYENA_HID, 1), const), pl.BlockSpec((WIDTH_C, 1), const),
        ],
        out_specs=pl.BlockSpec((WIDTH_C, n), lambda j: (0, j)),
        out_shape=jax.ShapeDtypeStruct((WIDTH_C, 2 * l), F32),
        compiler_params=_cparams(("parallel",)),
        name="hyena_filter",
    )(pos, fcol, w1t, w1c, w1s, b1, w2t, b2, w3t, fr, dl)


def _hyena_conv_kernel(kf_ref, u_ref, o_ref, uf_ref, acc_ref, *, nb, bsz):
    p = TOEP
    n2 = 2 * nb * p

    def channel(c, carry):
        krow = kf_ref[pl.ds(c, 1), :]
        uf_ref[...] = u_ref[c].astype(F32)
        acc_ref[...] = jnp.zeros_like(acc_ref)
        for d in range(-(nb - 1), nb):
            a0 = (d * p) % n2
            b0 = ((d - 1) * p) % n2
            seg = jnp.concatenate([krow[:, a0:a0 + p], krow[:, b0:b0 + p]], axis=1)
            rolled = pltpu.roll(jnp.broadcast_to(seg, (p, 2 * p)), 0, 1, stride=1, stride_axis=0)
            toep = rolled[:, :p].astype(BF16)
            rows = (nb - abs(d)) * bsz
            src = 0 if d >= 0 else -d * bsz
            dst = d * bsz if d >= 0 else 0
            acc_ref[dst:dst + rows, :] += _dot(uf_ref[src:src + rows, :].astype(BF16), toep)
        o_ref[c] = acc_ref[...].astype(o_ref.dtype)
        return carry

    lax.fori_loop(0, kf_ref.shape[0], channel, 0)


def _hyena_conv(kf, u_t, nb, bsz):
    cblk = 8
    rows = nb * bsz
    return pl.pallas_call(
        functools.partial(_hyena_conv_kernel, nb=nb, bsz=bsz),
        grid=(WIDTH_C // cblk,),
        in_specs=[
            pl.BlockSpec((cblk, 2 * nb * TOEP), lambda i: (i, 0)),
            pl.BlockSpec((cblk, rows, TOEP), lambda i: (i, 0, 0)),
        ],
        out_specs=pl.BlockSpec((cblk, rows, TOEP), lambda i: (i, 0, 0)),
        out_shape=jax.ShapeDtypeStruct((WIDTH_C, rows, TOEP), BF16),
        scratch_shapes=[pltpu.VMEM((rows, TOEP), F32), pltpu.VMEM((rows, TOEP), F32)],
        compiler_params=_cparams(("parallel",)),
        name="hyena_conv",
    )(kf, u_t)


def _merge_kernel(oa0_ref, oa1_ref, oa2_ref, l0_ref, l1_ref, l2_ref, yb_ref, cv_ref, u_ref, x0_ref, pg_ref, x_ref,
                  ex_ref, skip_ref, bg_ref, wa_ref, wb_ref, wc_ref, wo_ref, o_ref):
    lses = [l0_ref[...], l1_ref[...], l2_ref[...]]
    mx = jnp.maximum(jnp.maximum(lses[0], lses[1]), lses[2])
    num = None
    den = None
    for lse, oa in zip(lses, (oa0_ref, oa1_ref, oa2_ref)):
        w = _dot(jnp.exp(lse - mx).astype(BF16), ex_ref[...])
        num = w * oa[...].astype(F32) if num is None else num + w * oa[...].astype(F32)
        den = w if den is None else den + w
    ya = (num / den).astype(BF16)
    u = u_ref[...].astype(F32)
    yc = (x0_ref[...].astype(F32) * (cv_ref[...].astype(F32) + skip_ref[...] * u)).astype(BF16)
    mixed = None
    for i, (y, w_ref) in enumerate(((ya, wa_ref), (yb_ref[...], wb_ref), (yc, wc_ref))):
        sl = slice(i * D_MODEL, (i + 1) * D_MODEL)
        gate = jax.nn.sigmoid(pg_ref[:, sl].astype(F32) + bg_ref[:, sl])
        term = gate * _dot(y, w_ref[...])
        mixed = term if mixed is None else mixed + term
    o_ref[...] = x_ref[...] + _dot(mixed.astype(BF16), wo_ref[...])


def _merge(oas, lses, yb, cv, u, x0, proj, x2d, expand, skip, bg, wa, wb, wc, wo):
    t = x2d.shape[0]
    tm = 512
    row = lambda i: (i, 0)
    const = lambda i: (0, 0)
    half = pl.BlockSpec((tm, 512), row)
    return pl.pallas_call(
        _merge_kernel,
        grid=(t // tm,),
        in_specs=[
            half, half, half,
            pl.BlockSpec((tm, 2 * LANE), row), pl.BlockSpec((tm, 2 * LANE), row), pl.BlockSpec((tm, 2 * LANE), row),
            half, half, half, half,
            pl.BlockSpec((tm, COLS_G), lambda i: (i, OFF_G // COLS_G)),
            pl.BlockSpec((tm, D_MODEL), row),
            pl.BlockSpec((2 * LANE, WIDTH_A), const),
            pl.BlockSpec((1, WIDTH_C), const), pl.BlockSpec((1, COLS_G), const),
            pl.BlockSpec((WIDTH_A, D_MODEL), const), pl.BlockSpec((WIDTH_B, D_MODEL), const),
            pl.BlockSpec((WIDTH_C, D_MODEL), const), pl.BlockSpec((D_MODEL, D_MODEL), const),
        ],
        out_specs=pl.BlockSpec((tm, D_MODEL), row),
        out_shape=jax.ShapeDtypeStruct((t, D_MODEL), F32),
        compiler_params=_cparams(("parallel",)),
        name="branch_merge",
    )(*oas, *lses, yb, cv, u, x0, proj, x2d, expand, skip, bg, wa, wb, wc, wo)


def _ffn_up_kernel(x_ref, prev_ref, next_ref, g_ref, wa_ref, wg_ref, cwa_ref, cwg_ref, cba_ref, cbg_ref, o_ref,
                   h_ref, *, per_seq, halo):
    tm = x_ref.shape[0]

    @pl.when(pl.program_id(1) == 0)
    def _():
        i = pl.program_id(0) % per_seq

        def norm(x):
            return x * lax.rsqrt(jnp.mean(x * x, axis=-1, keepdims=True) + EPS) * g_ref[...]

        h_ref[0:halo, :] = jnp.where(i == 0, 0.0, norm(prev_ref[...])).astype(BF16)
        h_ref[halo:halo + tm, :] = norm(x_ref[...]).astype(BF16)
        h_ref[halo + tm:, :] = jnp.where(i == per_seq - 1, 0.0, norm(next_ref[...])).astype(BF16)

    h = h_ref[...]
    n = h.shape[0]

    def conv(w_ref, cw_ref, cb_ref):
        y = _dot(h, w_ref[...])
        ym = pltpu.roll(y, 1, 0)[halo:halo + tm]
        yp = pltpu.roll(y, n - 1, 0)[halo:halo + tm]
        return ym * cw_ref[0:1, :] + y[halo:halo + tm] * cw_ref[1:2, :] + yp * cw_ref[2:3, :] + cb_ref[...]

    a = conv(wa_ref, cwa_ref, cba_ref)
    g = conv(wg_ref, cwg_ref, cbg_ref)
    o_ref[...] = (jax.nn.gelu(a) * g).astype(o_ref.dtype)


def _ffn_up(x2d, gain, wa, wg, cwa, cwg, cba, cbg, l):
    t = x2d.shape[0]
    tm, tn, halo = 512, D_FF // 2, 16
    per_seq = l // tm
    nh = tm // halo
    const = lambda i, j: (0, 0)
    col = lambda i, j: (0, j)
    return pl.pallas_call(
        functools.partial(_ffn_up_kernel, per_seq=per_seq, halo=halo),
        grid=(t // tm, D_FF // tn),
        in_specs=[
            pl.BlockSpec((tm, D_MODEL), lambda i, j: (i, 0)),
            pl.BlockSpec((halo, D_MODEL), lambda i, j: (jnp.maximum(i * nh - 1, 0), 0)),
            pl.BlockSpec((halo, D_MODEL), lambda i, j: (jnp.minimum((i + 1) * nh, t // halo - 1), 0)),
            pl.BlockSpec((1, D_MODEL), const),
            pl.BlockSpec((D_MODEL, tn), col), pl.BlockSpec((D_MODEL, tn), col),
            pl.BlockSpec((3, tn), col), pl.BlockSpec((3, tn), col),
            pl.BlockSpec((1, tn), col), pl.BlockSpec((1, tn), col),
        ],
        out_specs=pl.BlockSpec((tm, tn), lambda i, j: (i, j)),
        out_shape=jax.ShapeDtypeStruct((t, D_FF), BF16),
        scratch_shapes=[pltpu.VMEM((tm + 2 * halo, D_MODEL), BF16)],
        compiler_params=_cparams(("parallel", "arbitrary")),
        name="ffn_up",
    )(x2d, x2d, x2d, gain, wa, wg, cwa, cwg, cba, cbg)


def _ffn_down_kernel(x_ref, a_ref, w_ref, o_ref):
    o_ref[...] = x_ref[...] + _dot(a_ref[...], w_ref[...])


def _ffn_down(x2d, act, w):
    t = x2d.shape[0]
    tm = 512
    return pl.pallas_call(
        _ffn_down_kernel,
        grid=(t // tm,),
        in_specs=[
            pl.BlockSpec((tm, D_MODEL), lambda i: (i, 0)),
            pl.BlockSpec((tm, D_FF), lambda i: (i, 0)),
            pl.BlockSpec((D_FF, D_MODEL), lambda i: (0, 0)),
        ],
        out_specs=pl.BlockSpec((tm, D_MODEL), lambda i: (i, 0)),
        out_shape=jax.ShapeDtypeStruct((t, D_MODEL), F32),
        compiler_params=_cparams(("parallel",)),
        name="ffn_down",
    )(x2d, act, w)


def _layer_params(p, l):
    w_in = p['w_in'][l]
    a_end, b_end, c_end = COLS_A, COLS_A + COLS_B, COLS_A + COLS_B + COLS_C
    w_r = jnp.concatenate([w_in[:, :a_end], w_in[:, b_end:c_end], w_in[:, c_end:], w_in[:, a_end:b_end],
                           jnp.zeros((D_MODEL, PROJ_W - OFF_B - COLS_B), F32)], axis=1).astype(BF16)
    slot_pad = SLOT_B - QK_B

    def slots(w):
        return jnp.pad(w, [(0, 0)] * (w.ndim - 1) + [(0, slot_pad)]).reshape(*w.shape[:-2], HEADS_B * SLOT_B)

    wq = slots(p['b_w_uq'][l].reshape(Q_RANK, HEADS_B, QK_B)).astype(BF16)
    wkv = p['b_w_ukv'][l].reshape(KV_RANK, HEADS_B, NOPE_B + V_B)
    wk_nope = jnp.pad(wkv[:, :, :NOPE_B], ((0, 0), (0, 0), (0, SLOT_B - NOPE_B))).reshape(KV_RANK, -1)
    place = jnp.pad(jnp.eye(ROPE_B, dtype=F32), ((0, 0), (NOPE_B, SLOT_B - QK_B)))
    wk_pe = jnp.tile(place, (1, HEADS_B))
    wk = jnp.concatenate([wk_nope, wk_pe, jnp.zeros((256 - KV_RANK - ROPE_B, HEADS_B * SLOT_B), F32)], axis=0)
    wv = wkv[:, :, NOPE_B:].reshape(KV_RANK, WIDTH_B)
    gq = jnp.tile(jnp.pad(p['b_q_g'][l], (0, slot_pad)), HEADS_B)[None] * (QK_B ** -0.5)
    gk = jnp.tile(jnp.pad(p['b_k_g'][l], (0, slot_pad)), HEADS_B)[None]
    w_up = p['w_up'][l]
    cw = p['ffn_conv_w'][l]
    cb = p['ffn_conv_b'][l]
    return dict(
        norm_attn_g=p['norm_attn_g'][l][None], w_r=w_r,
        a_qg=[jnp.tile(p['a_q_g'][l, g], 4)[None] * (HEAD_DIM_A ** -0.5) for g in range(3)],
        a_kg=[jnp.tile(p['a_k_g'][l, g], 4)[None] for g in range(3)],
        gql=p['b_q_lat_g'][l][None], gkl=p['b_kv_lat_g'][l][None],
        wq=wq, wk=wk.astype(BF16), wv=wv.astype(BF16), gq=gq, gk=gk,
        c_conv_w=p['c_conv_w'][l], c_conv_b=p['c_conv_b'][l][None],
        w1t=p['c_w1'][l][0:1].T, w1c=p['c_w1'][l][1:17].T, w1s=-p['c_w1'][l][17:33].T,
        b1=p['c_b1'][l][:, None], w2t=p['c_w2'][l].T, b2=p['c_b2'][l][:, None],
        w3t=p['c_w3'][l].T.reshape(2, WIDTH_C, HYENA_HID), fr=p['c_freq'][l][:, None],
        skip=p['c_skip'][l][None], bg=p['b_gate'][l][None],
        wa=p['w_br_a'][l].astype(BF16), wb=p['w_br_b'][l].astype(BF16), wc=p['w_br_c'][l].astype(BF16),
        wo=p['w_out'][l].astype(BF16), norm_ffn_g=p['norm_ffn_g'][l][None],
        w_up_a=w_up[:, :D_FF].astype(BF16), w_up_g=w_up[:, D_FF:].astype(BF16),
        cwa=cw[:, :D_FF], cwg=cw[:, D_FF:], cba=cb[None, :D_FF], cbg=cb[None, D_FF:],
        w_down=p['w_down'][l].astype(BF16),
    )


def _seq_constants(l):
    half = ROPE_B // 2
    pos = jnp.arange(l, dtype=F32)
    inv = ROPE_THETA ** (-jnp.arange(half, dtype=F32) / half)
    ang = pos[:, None] * inv[None, :]
    cos, sin = jnp.cos(ang), jnp.sin(ang)
    one = jnp.ones((l, NOPE_B), F32)
    zn = jnp.zeros((l, NOPE_B), F32)
    zh = jnp.zeros((l, half), F32)
    zp = jnp.zeros((l, SLOT_B - QK_B), F32)
    rc = jnp.concatenate([one, cos, cos, zp], axis=1)
    ra = jnp.concatenate([zn, -sin, zh, zp], axis=1)
    rb = jnp.concatenate([zn, zh, sin, zp], axis=1)
    m = jnp.arange(2 * l)
    p_idx = jnp.where(m < l, m, 2 * l - m)
    tlin = jnp.linspace(0.0, 1.0, l, dtype=F32)
    tt = tlin[jnp.minimum(p_idx, l - 1)]
    angf = (2.0 * math.pi / l) * p_idx.astype(F32)
    valid = (m != l).astype(F32)
    bands = (HYENA_EMB - 1) // 2
    fcol = jnp.linspace(1e-4, bands - 1, bands, dtype=F32)[:, None]
    deltas = jnp.abs(jnp.linspace(math.log(HYENA_TARGET) / HYENA_FAST_DECAY,
                                  math.log(HYENA_TARGET) / HYENA_SLOW_DECAY, WIDTH_C, dtype=F32))[:, None]
    return dict(rc=rc, ra=ra, rb=rb, pos=jnp.stack([tt, angf, valid]), fcol=fcol, deltas=deltas)


def _static_tables():
    hidx = jnp.arange(2 * LANE)
    head_of_row = jnp.where(hidx % LANE < 4, (hidx // LANE) * 4 + hidx % LANE, -1)
    expand = (head_of_row[:, None] == (jnp.arange(WIDTH_A) // HEAD_DIM_A)[None, :]).astype(BF16)
    slopes = jnp.exp2(-ALIBI_MAX * (jnp.arange(HEADS_A, dtype=F32) + 1.0) / HEADS_A)
    return expand, slopes


def _layer(x2d, lp, sc, expand, slopes, b, l):
    proj = _inproj(x2d, lp['norm_attn_g'], lp['w_r'])
    oas, lses = [], []
    for g, dil in enumerate(DILATIONS):
        o, lse = _mixer_a_group(proj, lp['a_qg'][g], lp['a_kg'][g], -slopes * dil, b, l, g, dil)
        oas.append(o)
        lses.append(lse)
    qp, kp, vt = _mla_prep(proj, lp['gql'], lp['gkl'], lp['wq'], lp['wk'], lp['wv'], lp['gq'], lp['gk'],
                           sc['rc'], sc['ra'], sc['rb'], b, l)
    yb = _mla_attn(qp, kp, vt, b, l)
    u, x0 = _hyena_prep(proj, lp['c_conv_w'], lp['c_conv_b'], b, l)
    kf = _hyena_filter(sc['pos'], sc['fcol'], lp['w1t'], lp['w1c'], lp['w1s'], lp['b1'], lp['w2t'], lp['b2'],
                       lp['w3t'], lp['fr'], sc['deltas'], l)
    nb = l // TOEP
    u_t = jnp.transpose(u.reshape(b, nb, TOEP, WIDTH_C), (3, 1, 0, 2)).reshape(WIDTH_C, nb * b, TOEP)
    cv_t = _hyena_conv(kf, u_t, nb, b)
    cv = jnp.transpose(cv_t.reshape(WIDTH_C, nb, b, TOEP), (2, 1, 3, 0)).reshape(b * l, WIDTH_C)
    xm = _merge(oas, lses, yb, cv, u, x0, proj, x2d, expand, lp['skip'], lp['bg'],
                lp['wa'], lp['wb'], lp['wc'], lp['wo'])
    act = _ffn_up(xm, lp['norm_ffn_g'], lp['w_up_a'], lp['w_up_g'], lp['cwa'], lp['cwg'], lp['cba'], lp['cbg'], l)
    return _ffn_down(xm, act, lp['w_down'])


def _trunk(x, layer_params, expand, slopes):
    b, l, _ = x.shape
    sc = _seq_constants(l)
    y = x.reshape(b * l, D_MODEL)
    for lp in layer_params:
        y = _layer(y, lp, sc, expand, slopes, b, l)
    return y.reshape(b, l, D_MODEL)


def kernel(x_prompt, x_sample, norm_attn_g, w_in, b_gate, a_q_g, a_k_g, b_q_lat_g, b_kv_lat_g, b_w_uq, b_w_ukv, b_q_g, b_k_g, c_conv_w, c_conv_b, c_w1, c_b1, c_w2, c_b2, c_w3, c_freq, c_skip, w_br_a, w_br_b, w_br_c, w_out, norm_ffn_g, w_up, ffn_conv_w, ffn_conv_b, w_down):
    p = dict(norm_attn_g=norm_attn_g, w_in=w_in, b_gate=b_gate, a_q_g=a_q_g, a_k_g=a_k_g, b_q_lat_g=b_q_lat_g,
             b_kv_lat_g=b_kv_lat_g, b_w_uq=b_w_uq, b_w_ukv=b_w_ukv, b_q_g=b_q_g, b_k_g=b_k_g, c_conv_w=c_conv_w,
             c_conv_b=c_conv_b, c_w1=c_w1, c_b1=c_b1, c_w2=c_w2, c_b2=c_b2, c_w3=c_w3, c_freq=c_freq,
             c_skip=c_skip, w_br_a=w_br_a, w_br_b=w_br_b, w_br_c=w_br_c, w_out=w_out, norm_ffn_g=norm_ffn_g,
             w_up=w_up, ffn_conv_w=ffn_conv_w, ffn_conv_b=ffn_conv_b, w_down=w_down)
    layer_params = [_layer_params(p, l) for l in range(DEPTH)]
    expand, slopes = _static_tables()
    return _trunk(x_prompt, layer_params, expand, slopes), _trunk(x_sample, layer_params, expand, slopes)
```

```python
import functools
import math

import jax
import jax.numpy as jnp
from jax import lax
from jax.experimental import pallas as pl
from jax.experimental.pallas import tpu as pltpu

F32 = jnp.float32
BF16 = jnp.bfloat16

D_MODEL = 1024
DEPTH = 2
EPS = 1e-6
HEADS_A = 8
HEAD_DIM_A = 64
DILATIONS = (1, 4, 16)
HALF_A = 64
WIDTH_A = HEADS_A * HEAD_DIM_A
ALIBI_MAX = 8.0
HEADS_B = 8
NOPE_B = 64
ROPE_B = 32
QK_B = NOPE_B + ROPE_B
V_B = 64
Q_RANK = 256
KV_RANK = 128
ROPE_THETA = 10000.0
WIDTH_B = HEADS_B * V_B
WIDTH_C = 512
HYENA_EMB = 33
HYENA_HID = 64
HYENA_FAST_DECAY = 0.3
HYENA_SLOW_DECAY = 1.5
HYENA_TARGET = 1e-2
D_FF = 2816
COLS_A = 3 * 3 * WIDTH_A
COLS_B = Q_RANK + KV_RANK + ROPE_B
COLS_C = 3 * WIDTH_C
COLS_G = 3 * D_MODEL
OFF_A = 0
OFF_C = COLS_A
OFF_G = OFF_C + COLS_C
OFF_B = OFF_G + COLS_G
PROJ_W = OFF_B + 512
LANE = 128
SLOT_B = 128
VROWS = 80
TOEP = 256
LOG2E = math.log2(math.e)
LN2 = math.log(2.0)
MASKED = 1e32
VMEM_LIMIT = 48 * 1024 * 1024


def _cparams(sem):
    return pltpu.CompilerParams(dimension_semantics=sem, vmem_limit_bytes=VMEM_LIMIT)


def _nt_dot(a, b):
    return lax.dot_general(a, b, (((1,), (1,)), ((), ())), preferred_element_type=F32)


def _dot(a, b):
    return jnp.dot(a, b, preferred_element_type=F32)


def _dot_exact(a, b):
    return jnp.dot(a, b, preferred_element_type=F32, precision=lax.Precision.HIGHEST)


def _inproj_kernel(x_ref, g_ref, w_ref, o_ref, h_ref):
    @pl.when(pl.program_id(1) == 0)
    def _():
        x = x_ref[...]
        ms = jnp.mean(x * x, axis=-1, keepdims=True)
        h_ref[...] = (x * lax.rsqrt(ms + EPS) * g_ref[...]).astype(BF16)

    o_ref[...] = _dot(h_ref[...], w_ref[...]).astype(o_ref.dtype)


def _inproj(x2d, gain, w_r):
    t = x2d.shape[0]
    tm, tn = 1024, PROJ_W // 4
    return pl.pallas_call(
        _inproj_kernel,
        grid=(t // tm, PROJ_W // tn),
        in_specs=[
            pl.BlockSpec((tm, D_MODEL), lambda i, j: (i, 0)),
            pl.BlockSpec((1, D_MODEL), lambda i, j: (0, 0)),
            pl.BlockSpec((D_MODEL, tn), lambda i, j: (0, j)),
        ],
        out_specs=pl.BlockSpec((tm, tn), lambda i, j: (i, j)),
        out_shape=jax.ShapeDtypeStruct((t, PROJ_W), BF16),
        scratch_shapes=[pltpu.VMEM((tm, D_MODEL), BF16)],
        compiler_params=_cparams(("parallel", "arbitrary")),
        name="inproj",
    )(x2d, gain, w_r)


def _dil_attn_kernel(negc_ref, q_ref, k_ref, v_ref, qg_ref, kg_ref, o_ref, lse_ref,
                     qn_ref, kn_ref, vn_ref, stg_ref, ost_ref, bias_ref, *, l, dil):
    hb = pl.program_id(1)
    lu = l // dil
    cw = q_ref.shape[-1]
    win = min(2 * LANE, lu)
    nq = lu // LANE
    row = lax.broadcasted_iota(jnp.int32, (cw, cw), 0) // HEAD_DIM_A
    col = lax.broadcasted_iota(jnp.int32, (cw, cw), 1) // HEAD_DIM_A
    seg_ones = jnp.where(row == col, 1.0, 0.0).astype(BF16)

    chunk = 512
    per = chunk // dil

    def norm_body(c, carry):
        r0 = pl.multiple_of(c * chunk, chunk)
        for src, gain, dst in ((q_ref, qg_ref, qn_ref), (k_ref, kg_ref, kn_ref), (v_ref, None, vn_ref)):
            if gain is None and dil == 1:
                continue
            x = src[0, pl.ds(r0, chunk), :].astype(F32)
            if gain is not None:
                ssq = _dot((x * x).astype(BF16), seg_ones)
                x = x * lax.rsqrt(ssq * (1.0 / HEAD_DIM_A) + EPS) * gain[...]
            if dil == 1:
                dst[pl.ds(r0, chunk), :] = x.astype(BF16)
            else:
                for h in range(cw // LANE):
                    stg_ref[h] = x[:, h * LANE:(h + 1) * LANE]
                for r in range(dil):
                    d0 = pl.multiple_of(r * lu + c * per, per)
                    for h in range(cw // LANE):
                        dst[pl.ds(d0, per), h * LANE:(h + 1) * LANE] = (
                            stg_ref[h, pl.ds(r, per, stride=dil), :].astype(BF16))
        return carry

    lax.fori_loop(0, l // chunk, norm_body, 0)

    lane_lo = lax.broadcasted_iota(jnp.int32, (1, LANE), 1) < HEAD_DIM_A
    i_q = lax.broadcasted_iota(jnp.int32, (LANE, win), 0)
    j_k = lax.broadcasted_iota(jnp.int32, (LANE, win), 1)
    lane_id = lax.broadcasted_iota(jnp.int32, (LANE, LANE), 1)

    npair = cw // LANE
    tiles_per_step = 2
    for variant in range(3):
        absrel = jnp.abs(j_k - i_q - variant * HALF_A).astype(F32)
        absrel = jnp.where(absrel <= float(HALF_A), absrel, MASKED)
        for head in range(2 * npair):
            bias_ref[variant * (2 * npair) + head] = absrel * negc_ref[4 * hb + head]

    def step_body(step, carry):
        tiles = []
        for u in range(tiles_per_step):
            idx = step * tiles_per_step + u
            r = idx // nq
            q0 = pl.multiple_of((idx % nq) * LANE, LANE)
            w0 = pl.multiple_of(jnp.clip(q0 - HALF_A, 0, lu - win), HALF_A)
            base = pl.multiple_of(r * lu, LANE)
            q = qn_ref[pl.ds(base + q0, LANE), :]
            kw = kn_ref[pl.ds(base + w0, win), :]
            vw = v_ref[0, pl.ds(w0, win), :] if dil == 1 else vn_ref[pl.ds(base + w0, win), :]
            scores = []
            for pair in range(npair):
                qp = q[:, pair * LANE:(pair + 1) * LANE]
                zero = jnp.zeros_like(qp)
                qs = jnp.concatenate([jnp.where(lane_lo, qp, zero), jnp.where(lane_lo, zero, qp)], axis=0)
                scores.append(_nt_dot(qs, kw[:, pair * LANE:(pair + 1) * LANE]))
            tiles.append((r, q0, w0, vw, scores))

        probs = []
        for r, q0, w0, vw, scores in tiles:
            variant = (q0 - w0) // HALF_A
            m_tile = jnp.zeros((LANE, LANE), F32)
            den_tile = jnp.ones((LANE, LANE), F32)
            tile_p = []
            for pair in range(npair):
                for a in range(2):
                    head = 2 * pair + a
                    sa = scores[pair][a * LANE:(a + 1) * LANE] + bias_ref[variant * (2 * npair) + head]
                    m = jnp.max(sa, axis=1, keepdims=True)
                    p = jnp.exp2(sa - m)
                    den = jnp.sum(p, axis=1, keepdims=True)
                    tile_p.append((p.astype(BF16), 1.0 / den))
                    m_tile = jnp.where(lane_id == head, m, m_tile)
                    den_tile = jnp.where(lane_id == head, den, den_tile)
            lse_tile = jnp.where(lane_id < 2 * npair, (m_tile + jnp.log2(den_tile)) * LN2, 0.0)
            probs.append((tile_p, lse_tile))

        for (r, q0, w0, vw, scores), (tile_p, lse_tile) in zip(tiles, probs):
            outs = []
            for pair in range(npair):
                vp = vw[:, pair * LANE:(pair + 1) * LANE]
                vzero = jnp.zeros_like(vp)
                (pa, ra), (pb, rb) = tile_p[2 * pair], tile_p[2 * pair + 1]
                outs.append(_dot(pa, jnp.where(lane_lo, vp, vzero)) * ra + _dot(pb, jnp.where(lane_lo, vzero, vp)) * rb)
            if dil == 1:
                o_ref[0, pl.ds(q0, LANE), :] = jnp.concatenate(outs, axis=1).astype(o_ref.dtype)
                lse_ref[0, pl.ds(q0, LANE), :] = lse_tile
            else:
                for h, o_pair in enumerate(outs):
                    ost_ref[h, pl.ds(r + q0 * dil, LANE, stride=dil), :] = o_pair
                lse_ref[0, pl.ds(r + q0 * dil, LANE, stride=dil), :] = lse_tile
        return carry

    lax.fori_loop(0, l // (LANE * tiles_per_step), step_body, 0)

    if dil != 1:
        def out_body(c, carry):
            r0 = pl.multiple_of(c * chunk, chunk)
            for h in range(cw // LANE):
                o_ref[0, pl.ds(r0, chunk), h * LANE:(h + 1) * LANE] = (
                    ost_ref[h, pl.ds(r0, chunk), :].astype(o_ref.dtype))
            return carry

        lax.fori_loop(0, l // chunk, out_body, 0)


def _mixer_a_group(proj, qg, kg, negc, b, l, g, dil):
    pv = proj.reshape(b, l, PROJ_W)
    cq, ck, cv = (OFF_A + (0 + g) * 512) // 256, (OFF_A + (3 + g) * 512) // 256, (OFF_A + (6 + g) * 512) // 256

    def in_spec(c0):
        return pl.BlockSpec((1, l, 256), lambda bi, hb: (bi, 0, c0 + hb))

    stage_rows = (8, 8) if dil == 1 else (512, l)
    o, lse = pl.pallas_call(
        functools.partial(_dil_attn_kernel, l=l, dil=dil),
        grid=(b, 2),
        in_specs=[
            pl.BlockSpec(memory_space=pltpu.SMEM),
            in_spec(cq), in_spec(ck), in_spec(cv),
            pl.BlockSpec((1, 256), lambda bi, hb: (0, 0)),
            pl.BlockSpec((1, 256), lambda bi, hb: (0, 0)),
        ],
        out_specs=[
            pl.BlockSpec((1, l, 256), lambda bi, hb: (bi, 0, hb)),
            pl.BlockSpec((1, l, LANE), lambda bi, hb: (bi, 0, hb)),
        ],
        out_shape=[
            jax.ShapeDtypeStruct((b, l, WIDTH_A), BF16),
            jax.ShapeDtypeStruct((b, l, 2 * LANE), F32),
        ],
        scratch_shapes=[pltpu.VMEM((l, 256), BF16), pltpu.VMEM((l, 256), BF16),
                        pltpu.VMEM((l if dil != 1 else 16, 256), BF16),
                        pltpu.VMEM((2, stage_rows[0], LANE), F32), pltpu.VMEM((2, stage_rows[1], LANE), F32),
                        pltpu.VMEM((12, LANE, min(2 * LANE, l // dil)), F32)],
        compiler_params=_cparams(("parallel", "arbitrary")),
        name=f"dilated_attention_g{g}",
    )(negc, pv, pv, pv, qg, kg)
    return o.reshape(b * l, WIDTH_A), lse.reshape(b * l, 2 * LANE)


def _mla_prep_kernel(pb_ref, gql_ref, gkl_ref, wq_ref, wk_ref, wv_ref, gq_ref, gk_ref, rc_ref, ra_ref, rb_ref,
                     q_ref, k_ref, vt_ref):
    c = pb_ref[...].astype(F32)
    cq = c[:, :Q_RANK]
    cqn = (cq * lax.rsqrt(jnp.mean(cq * cq, axis=-1, keepdims=True) + EPS) * gql_ref[...]).astype(BF16)
    ckv = c[:, Q_RANK:Q_RANK + KV_RANK]
    ckvn = (ckv * lax.rsqrt(jnp.mean(ckv * ckv, axis=-1, keepdims=True) + EPS) * gkl_ref[...]).astype(BF16)
    q = _dot(cqn, wq_ref[...])
    k = _dot(jnp.concatenate([ckvn, pb_ref[:, Q_RANK + KV_RANK:]], axis=1), wk_ref[...])
    v = _dot(ckvn, wv_ref[...])
    rc, ra, rb = rc_ref[...], ra_ref[...], rb_ref[...]

    def finish(x, gain_ref, dst):
        for h in range(HEADS_B):
            sl = slice(h * SLOT_B, (h + 1) * SLOT_B)
            xh = x[:, sl]
            ssq = jnp.sum(xh * xh, axis=-1, keepdims=True)
            xn = xh * lax.rsqrt(ssq * (1.0 / QK_B) + EPS) * gain_ref[:, sl]
            half = ROPE_B // 2
            xr = xn * rc + pltpu.roll(xn, SLOT_B - half, 1) * ra + pltpu.roll(xn, half, 1) * rb
            dst[:, sl] = xr.astype(dst.dtype)

    finish(q, gq_ref, q_ref)
    finish(k, gk_ref, k_ref)
    vt_ref[0] = v.T.astype(vt_ref.dtype)


def _mla_prep(proj, gql, gkl, wq, wk, wv, gq, gk, rc, ra, rb, b, l):
    t = b * l
    tm = 512
    per_seq = l // tm
    const = lambda i: (0, 0)
    return pl.pallas_call(
        _mla_prep_kernel,
        grid=(t // tm,),
        in_specs=[
            pl.BlockSpec((tm, 512), lambda i: (i, OFF_B // 512)),
            pl.BlockSpec((1, Q_RANK), const), pl.BlockSpec((1, KV_RANK), const),
            pl.BlockSpec((Q_RANK, HEADS_B * SLOT_B), const),
            pl.BlockSpec((256, HEADS_B * SLOT_B), const),
            pl.BlockSpec((KV_RANK, WIDTH_B), const),
            pl.BlockSpec((1, HEADS_B * SLOT_B), const), pl.BlockSpec((1, HEADS_B * SLOT_B), const),
            pl.BlockSpec((tm, SLOT_B), lambda i: (i % per_seq, 0)),
            pl.BlockSpec((tm, SLOT_B), lambda i: (i % per_seq, 0)),
            pl.BlockSpec((tm, SLOT_B), lambda i: (i % per_seq, 0)),
        ],
        out_specs=[
            pl.BlockSpec((tm, HEADS_B * SLOT_B), lambda i: (i, 0)),
            pl.BlockSpec((tm, HEADS_B * SLOT_B), lambda i: (i, 0)),
            pl.BlockSpec((1, WIDTH_B, tm), lambda i: (i // per_seq, 0, i % per_seq)),
        ],
        out_shape=[
            jax.ShapeDtypeStruct((t, HEADS_B * SLOT_B), BF16),
            jax.ShapeDtypeStruct((t, HEADS_B * SLOT_B), BF16),
            jax.ShapeDtypeStruct((b, WIDTH_B, l), BF16),
        ],
        compiler_params=_cparams(("parallel",)),
        name="latent_prep",
    )(proj, gql, gkl, wq, wk, wv, gq, gk, rc, ra, rb)


def _mla_attn_kernel(q_ref, k_ref, vt_ref, o_ref, vaug_ref, *, tk):
    nkv = vaug_ref.shape[1]
    tq = q_ref.shape[1]

    @pl.when(pl.program_id(2) == 0)
    def _():
        ones_rows = jnp.where(lax.broadcasted_iota(jnp.int32, (VROWS - V_B, tk), 0) == 0, 1.0, 0.0).astype(BF16)
        for a in range(2):
            for j in range(nkv):
                vaug_ref[a, j, 0:V_B, :] = vt_ref[0, a * V_B:(a + 1) * V_B, j * tk:(j + 1) * tk]
                vaug_ref[a, j, V_B:VROWS, :] = ones_rows

    qh = [q_ref[0, :, a * SLOT_B:(a + 1) * SLOT_B] for a in range(2)]
    m = [jnp.full((1, tq), -jnp.inf, F32) for _ in range(2)]
    acc = [jnp.zeros((VROWS, tq), F32) for _ in range(2)]

    def scores(j):
        return [_nt_dot(k_ref[0, j * tk:(j + 1) * tk, a * SLOT_B:(a + 1) * SLOT_B], qh[a]) for a in range(2)]

    st = scores(0)
    for j in range(nkv):
        st_next = scores(j + 1) if j + 1 < nkv else None
        ps = []
        for a in range(2):
            mn = jnp.maximum(m[a], jnp.max(st[a], axis=0, keepdims=True))
            ps.append(jnp.exp2(st[a] - mn).astype(BF16))
            acc[a] = acc[a] * jnp.exp2(m[a] - mn)
            m[a] = mn
        for a in range(2):
            acc[a] = acc[a] + _dot(vaug_ref[a, j], ps[a])
        st = st_next
    outs = [acc[a][0:V_B] * (1.0 / acc[a][V_B:V_B + 1]) for a in range(2)]
    o_ref[0] = jnp.concatenate(outs, axis=0).T.astype(o_ref.dtype)


def _mla_attn(qp, kp, vt, b, l):
    tq, tk = 256, 512
    q3 = qp.reshape(b, l, HEADS_B * SLOT_B)
    k3 = kp.reshape(b, l, HEADS_B * SLOT_B)
    out = pl.pallas_call(
        functools.partial(_mla_attn_kernel, tk=tk),
        grid=(b, HEADS_B // 2, l // tq),
        in_specs=[
            pl.BlockSpec((1, tq, 2 * SLOT_B), lambda bi, p, qi: (bi, qi, p)),
            pl.BlockSpec((1, l, 2 * SLOT_B), lambda bi, p, qi: (bi, 0, p)),
            pl.BlockSpec((1, 2 * V_B, l), lambda bi, p, qi: (bi, p, 0)),
        ],
        out_specs=pl.BlockSpec((1, tq, 2 * V_B), lambda bi, p, qi: (bi, qi, p)),
        out_shape=jax.ShapeDtypeStruct((b, l, WIDTH_B), BF16),
        scratch_shapes=[pltpu.VMEM((2, l // tk, VROWS, tk), BF16)],
        compiler_params=_cparams(("parallel", "parallel", "arbitrary")),
        name="latent_attention",
    )(q3, k3, vt)
    return out.reshape(b * l, WIDTH_B)


def _shift_rows(x, prev_row, next_row):
    n = x.shape[0]
    rid = lax.broadcasted_iota(jnp.int32, x.shape, 0)
    xm = jnp.where(rid == 0, prev_row, pltpu.roll(x, 1, 0))
    xp = jnp.where(rid == n - 1, next_row, pltpu.roll(x, n - 1, 0))
    return xm, xp


def _hyena_prep_kernel(pc_ref, prev_ref, next_ref, w_ref, b_ref, u_ref, x0_ref, *, per_seq):
    i = pl.program_id(0) % per_seq
    x = pc_ref[...].astype(F32)
    hr = prev_ref.shape[0]
    prev_row = jnp.where(i == 0, 0.0, prev_ref[...].astype(F32)[hr - 1:hr, :])
    next_row = jnp.where(i == per_seq - 1, 0.0, next_ref[...].astype(F32)[0:1, :])
    xm, xp = _shift_rows(x, prev_row, next_row)
    y = xm * w_ref[0:1, :] + x * w_ref[1:2, :] + xp * w_ref[2:3, :] + b_ref[...]
    x0_ref[...] = y[:, :WIDTH_C].astype(x0_ref.dtype)
    u_ref[...] = (y[:, 2 * WIDTH_C:] * y[:, WIDTH_C:2 * WIDTH_C]).astype(u_ref.dtype)


def _hyena_prep(proj, conv_w, conv_b, b, l):
    t = b * l
    tm, hr = 512, 16
    per_seq = l // tm
    nh = tm // hr
    cb = OFF_C // COLS_C
    return pl.pallas_call(
        functools.partial(_hyena_prep_kernel, per_seq=per_seq),
        grid=(t // tm,),
        in_specs=[
            pl.BlockSpec((tm, COLS_C), lambda i: (i, cb)),
            pl.BlockSpec((hr, COLS_C), lambda i: (jnp.maximum(i * nh - 1, 0), cb)),
            pl.BlockSpec((hr, COLS_C), lambda i: (jnp.minimum((i + 1) * nh, t // hr - 1), cb)),
            pl.BlockSpec((3, COLS_C), lambda i: (0, 0)),
            pl.BlockSpec((1, COLS_C), lambda i: (0, 0)),
        ],
        out_specs=[pl.BlockSpec((tm, WIDTH_C), lambda i: (i, 0)), pl.BlockSpec((tm, WIDTH_C), lambda i: (i, 0))],
        out_shape=[jax.ShapeDtypeStruct((t, WIDTH_C), BF16), jax.ShapeDtypeStruct((t, WIDTH_C), BF16)],
        compiler_params=_cparams(("parallel",)),
        name="hyena_prep",
    )(proj, proj, proj, conv_w, conv_b)


def _hyena_filter_kernel(pos_ref, fcol_ref, w1t_ref, w1c_ref, w1s_ref, b1_ref, w2_ref, b2_ref, w3_ref, fr_ref,
                         dl_ref, o_ref):
    tt = pos_ref[0:1, :]
    ang = pos_ref[1:2, :]
    valid = pos_ref[2:3, :]
    arg = fcol_ref[...] * ang
    pre1 = w1t_ref[...] * tt + _dot_exact(w1c_ref[...], jnp.cos(arg)) + _dot_exact(w1s_ref[...], jnp.sin(arg))
    fr = fr_ref[...]
    hid = jnp.sin(fr * (pre1 + b1_ref[...]))
    hid = jnp.sin(fr * (_dot_exact(w2_ref[...], hid) + b2_ref[...]))
    filt = _dot_exact(w3_ref[0], hid)
    o_ref[...] = filt * jnp.exp(-dl_ref[...] * tt) * valid


def _hyena_filter(pos, fcol, w1t, w1c, w1s, b1, w2t, b2, w3t, fr, dl, l):
    n = 2048
    const = lambda j: (0, 0)
    return pl.pallas_call(
        _hyena_filter_kernel,
        grid=(2 * l // n,),
        in_specs=[
            pl.BlockSpec((3, n), lambda j: (0, j)),
            pl.BlockSpec((16, 1), const),
            pl.BlockSpec((HYENA_HID, 1), const), pl.BlockSpec((HYENA_HID, 16), const),
            pl.BlockSpec((HYENA_HID, 16), const), pl.BlockSpec((HYENA_HID, 1), const),
            pl.BlockSpec((HYENA_HID, HYENA_HID), const), pl.BlockSpec((HYENA_HID, 1), const),
            pl.BlockSpec((1, WIDTH_C, HYENA_HID), lambda j: (j // (l // n), 0, 0)),
            pl.BlockSpec((HYENA_HID, 1), const), pl.BlockSpec((WIDTH_C, 1), const),
        ],
        out_specs=pl.BlockSpec((WIDTH_C, n), lambda j: (0, j)),
        out_shape=jax.ShapeDtypeStruct((WIDTH_C, 2 * l), F32),
        compiler_params=_cparams(("parallel",)),
        name="hyena_filter",
    )(pos, fcol, w1t, w1c, w1s, b1, w2t, b2, w3t, fr, dl)


def _hyena_conv_kernel(kf_ref, u_ref, o_ref, uf_ref, acc_ref, *, nb, bsz):
    p = TOEP
    n2 = 2 * nb * p

    def channel(c, carry):
        krow = kf_ref[pl.ds(c, 1), :]
        uf_ref[...] = u_ref[c].astype(F32)
        acc_ref[...] = jnp.zeros_like(acc_ref)
        for d in range(-(nb - 1), nb):
            a0 = (d * p) % n2
            b0 = ((d - 1) * p) % n2
            seg = jnp.concatenate([krow[:, a0:a0 + p], krow[:, b0:b0 + p]], axis=1)
            rolled = pltpu.roll(jnp.broadcast_to(seg, (p, 2 * p)), 0, 1, stride=1, stride_axis=0)
            toep = rolled[:, :p].astype(BF16)
            rows = (nb - abs(d)) * bsz
            src = 0 if d >= 0 else -d * bsz
            dst = d * bsz if d >= 0 else 0
            acc_ref[dst:dst + rows, :] += _dot(uf_ref[src:src + rows, :].astype(BF16), toep)
        o_ref[c] = acc_ref[...].astype(o_ref.dtype)
        return carry

    lax.fori_loop(0, kf_ref.shape[0], channel, 0)


def _hyena_conv(kf, u_t, nb, bsz):
    cblk = 8
    rows = nb * bsz
    return pl.pallas_call(
        functools.partial(_hyena_conv_kernel, nb=nb, bsz=bsz),
        grid=(WIDTH_C // cblk,),
        in_specs=[
            pl.BlockSpec((cblk, 2 * nb * TOEP), lambda i: (i, 0)),
            pl.BlockSpec((cblk, rows, TOEP), lambda i: (i, 0, 0)),
        ],
        out_specs=pl.BlockSpec((cblk, rows, TOEP), lambda i: (i, 0, 0)),
        out_shape=jax.ShapeDtypeStruct((WIDTH_C, rows, TOEP), BF16),
        scratch_shapes=[pltpu.VMEM((rows, TOEP), F32), pltpu.VMEM((rows, TOEP), F32)],
        compiler_params=_cparams(("parallel",)),
        name="hyena_conv",
    )(kf, u_t)


def _merge_kernel(oa0_ref, oa1_ref, oa2_ref, l0_ref, l1_ref, l2_ref, yb_ref, cv_ref, u_ref, x0_ref, pg_ref, x_ref,
                  ex_ref, skip_ref, bg_ref, wa_ref, wb_ref, wc_ref, wo_ref, o_ref):
    lses = [l0_ref[...], l1_ref[...], l2_ref[...]]
    mx = jnp.maximum(jnp.maximum(lses[0], lses[1]), lses[2])
    num = None
    den = None
    for lse, oa in zip(lses, (oa0_ref, oa1_ref, oa2_ref)):
        w = _dot(jnp.exp(lse - mx).astype(BF16), ex_ref[...])
        num = w * oa[...].astype(F32) if num is None else num + w * oa[...].astype(F32)
        den = w if den is None else den + w
    ya = (num / den).astype(BF16)
    u = u_ref[...].astype(F32)
    yc = (x0_ref[...].astype(F32) * (cv_ref[...].astype(F32) + skip_ref[...] * u)).astype(BF16)
    mixed = None
    for i, (y, w_ref) in enumerate(((ya, wa_ref), (yb_ref[...], wb_ref), (yc, wc_ref))):
        sl = slice(i * D_MODEL, (i + 1) * D_MODEL)
        gate = jax.nn.sigmoid(pg_ref[:, sl].astype(F32) + bg_ref[:, sl])
        term = gate * _dot(y, w_ref[...])
        mixed = term if mixed is None else mixed + term
    o_ref[...] = x_ref[...] + _dot(mixed.astype(BF16), wo_ref[...])


def _merge(oas, lses, yb, cv, u, x0, proj, x2d, expand, skip, bg, wa, wb, wc, wo):
    t = x2d.shape[0]
    tm = 512
    row = lambda i: (i, 0)
    const = lambda i: (0, 0)
    half = pl.BlockSpec((tm, 512), row)
    return pl.pallas_call(
        _merge_kernel,
        grid=(t // tm,),
        in_specs=[
            half, half, half,
            pl.BlockSpec((tm, 2 * LANE), row), pl.BlockSpec((tm, 2 * LANE), row), pl.BlockSpec((tm, 2 * LANE), row),
            half, half, half, half,
            pl.BlockSpec((tm, COLS_G), lambda i: (i, OFF_G // COLS_G)),
            pl.BlockSpec((tm, D_MODEL), row),
            pl.BlockSpec((2 * LANE, WIDTH_A), const),
            pl.BlockSpec((1, WIDTH_C), const), pl.BlockSpec((1, COLS_G), const),
            pl.BlockSpec((WIDTH_A, D_MODEL), const), pl.BlockSpec((WIDTH_B, D_MODEL), const),
            pl.BlockSpec((WIDTH_C, D_MODEL), const), pl.BlockSpec((D_MODEL, D_MODEL), const),
        ],
        out_specs=pl.BlockSpec((tm, D_MODEL), row),
        out_shape=jax.ShapeDtypeStruct((t, D_MODEL), F32),
        compiler_params=_cparams(("parallel",)),
        name="branch_merge",
    )(*oas, *lses, yb, cv, u, x0, proj, x2d, expand, skip, bg, wa, wb, wc, wo)


def _ffn_up_kernel(x_ref, prev_ref, next_ref, g_ref, wa_ref, wg_ref, cwa_ref, cwg_ref, cba_ref, cbg_ref, o_ref,
                   h_ref, *, per_seq, halo):
    tm = x_ref.shape[0]

    @pl.when(pl.program_id(1) == 0)
    def _():
        i = pl.program_id(0) % per_seq

        def norm(x):
            return x * lax.rsqrt(jnp.mean(x * x, axis=-1, keepdims=True) + EPS) * g_ref[...]

        h_ref[0:halo, :] = jnp.where(i == 0, 0.0, norm(prev_ref[...])).astype(BF16)
        h_ref[halo:halo + tm, :] = norm(x_ref[...]).astype(BF16)
        h_ref[halo + tm:, :] = jnp.where(i == per_seq - 1, 0.0, norm(next_ref[...])).astype(BF16)

    h = h_ref[...]
    n = h.shape[0]

    def conv(w_ref, cw_ref, cb_ref):
        y = _dot(h, w_ref[...])
        ym = pltpu.roll(y, 1, 0)[halo:halo + tm]
        yp = pltpu.roll(y, n - 1, 0)[halo:halo + tm]
        return ym * cw_ref[0:1, :] + y[halo:halo + tm] * cw_ref[1:2, :] + yp * cw_ref[2:3, :] + cb_ref[...]

    a = conv(wa_ref, cwa_ref, cba_ref)
    g = conv(wg_ref, cwg_ref, cbg_ref)
    o_ref[...] = (jax.nn.gelu(a) * g).astype(o_ref.dtype)


def _ffn_up(x2d, gain, wa, wg, cwa, cwg, cba, cbg, l):
    t = x2d.shape[0]
    tm, tn, halo = 512, D_FF // 2, 16
    per_seq = l // tm
    nh = tm // halo
    const = lambda i, j: (0, 0)
    col = lambda i, j: (0, j)
    return pl.pallas_call(
        functools.partial(_ffn_up_kernel, per_seq=per_seq, halo=halo),
        grid=(t // tm, D_FF // tn),
        in_specs=[
            pl.BlockSpec((tm, D_MODEL), lambda i, j: (i, 0)),
            pl.BlockSpec((halo, D_MODEL), lambda i, j: (jnp.maximum(i * nh - 1, 0), 0)),
            pl.BlockSpec((halo, D_MODEL), lambda i, j: (jnp.minimum((i + 1) * nh, t // halo - 1), 0)),
            pl.BlockSpec((1, D_MODEL), const),
            pl.BlockSpec((D_MODEL, tn), col), pl.BlockSpec((D_MODEL, tn), col),
            pl.BlockSpec((3, tn), col), pl.BlockSpec((3, tn), col),
            pl.BlockSpec((1, tn), col), pl.BlockSpec((1, tn), col),
        ],
        out_specs=pl.BlockSpec((tm, tn), lambda i, j: (i, j)),
        out_shape=jax.ShapeDtypeStruct((t, D_FF), BF16),
        scratch_shapes=[pltpu.VMEM((tm + 2 * halo, D_MODEL), BF16)],
        compiler_params=_cparams(("parallel", "arbitrary")),
        name="ffn_up",
    )(x2d, x2d, x2d, gain, wa, wg, cwa, cwg, cba, cbg)


def _ffn_down_kernel(x_ref, a_ref, w_ref, o_ref):
    o_ref[...] = x_ref[...] + _dot(a_ref[...], w_ref[...])


def _ffn_down(x2d, act, w):
    t = x2d.shape[0]
    tm = 512
    return pl.pallas_call(
        _ffn_down_kernel,
        grid=(t // tm,),
        in_specs=[
            pl.BlockSpec((tm, D_MODEL), lambda i: (i, 0)),
            pl.BlockSpec((tm, D_FF), lambda i: (i, 0)),
            pl.BlockSpec((D_FF, D_MODEL), lambda i: (0, 0)),
        ],
        out_specs=pl.BlockSpec((tm, D_MODEL), lambda i: (i, 0)),
        out_shape=jax.ShapeDtypeStruct((t, D_MODEL), F32),
        compiler_params=_cparams(("parallel",)),
        name="ffn_down",
    )(x2d, act, w)


def _layer_params(p, l):
    w_in = p['w_in'][l]
    a_end, b_end, c_end = COLS_A, COLS_A + COLS_B, COLS_A + COLS_B + COLS_C
    w_r = jnp.concatenate([w_in[:, :a_end], w_in[:, b_end:c_end], w_in[:, c_end:], w_in[:, a_end:b_end],
                           jnp.zeros((D_MODEL, PROJ_W - OFF_B - COLS_B), F32)], axis=1).astype(BF16)
    slot_pad = SLOT_B - QK_B

    def slots(w):
        return jnp.pad(w, [(0, 0)] * (w.ndim - 1) + [(0, slot_pad)]).reshape(*w.shape[:-2], HEADS_B * SLOT_B)

    wq = slots(p['b_w_uq'][l].reshape(Q_RANK, HEADS_B, QK_B)).astype(BF16)
    wkv = p['b_w_ukv'][l].reshape(KV_RANK, HEADS_B, NOPE_B + V_B)
    wk_nope = jnp.pad(wkv[:, :, :NOPE_B], ((0, 0), (0, 0), (0, SLOT_B - NOPE_B))).reshape(KV_RANK, -1)
    place = jnp.pad(jnp.eye(ROPE_B, dtype=F32), ((0, 0), (NOPE_B, SLOT_B - QK_B)))
    wk_pe = jnp.tile(place, (1, HEADS_B))
    wk = jnp.concatenate([wk_nope, wk_pe, jnp.zeros((256 - KV_RANK - ROPE_B, HEADS_B * SLOT_B), F32)], axis=0)
    wv = wkv[:, :, NOPE_B:].reshape(KV_RANK, WIDTH_B)
    gq = jnp.tile(jnp.pad(p['b_q_g'][l], (0, slot_pad)), HEADS_B)[None] * (QK_B ** -0.5 * LOG2E)
    gk = jnp.tile(jnp.pad(p['b_k_g'][l], (0, slot_pad)), HEADS_B)[None]
    w_up = p['w_up'][l]
    cw = p['ffn_conv_w'][l]
    cb = p['ffn_conv_b'][l]
    return dict(
        norm_attn_g=p['norm_attn_g'][l][None], w_r=w_r,
        a_qg=[jnp.tile(p['a_q_g'][l, g], 4)[None] * (HEAD_DIM_A ** -0.5 * LOG2E) for g in range(3)],
        a_kg=[jnp.tile(p['a_k_g'][l, g], 4)[None] for g in range(3)],
        gql=p['b_q_lat_g'][l][None], gkl=p['b_kv_lat_g'][l][None],
        wq=wq, wk=wk.astype(BF16), wv=wv.astype(BF16), gq=gq, gk=gk,
        c_conv_w=p['c_conv_w'][l], c_conv_b=p['c_conv_b'][l][None],
        w1t=p['c_w1'][l][0:1].T, w1c=p['c_w1'][l][1:17].T, w1s=-p['c_w1'][l][17:33].T,
        b1=p['c_b1'][l][:, None], w2t=p['c_w2'][l].T, b2=p['c_b2'][l][:, None],
        w3t=p['c_w3'][l].T.reshape(2, WIDTH_C, HYENA_HID), fr=p['c_freq'][l][:, None],
        skip=p['c_skip'][l][None], bg=p['b_gate'][l][None],
        wa=p['w_br_a'][l].astype(BF16), wb=p['w_br_b'][l].astype(BF16), wc=p['w_br_c'][l].astype(BF16),
        wo=p['w_out'][l].astype(BF16), norm_ffn_g=p['norm_ffn_g'][l][None],
        w_up_a=w_up[:, :D_FF].astype(BF16), w_up_g=w_up[:, D_FF:].astype(BF16),
        cwa=cw[:, :D_FF], cwg=cw[:, D_FF:], cba=cb[None, :D_FF], cbg=cb[None, D_FF:],
        w_down=p['w_down'][l].astype(BF16),
    )


def _seq_constants(l):
    half = ROPE_B // 2
    pos = jnp.arange(l, dtype=F32)
    inv = ROPE_THETA ** (-jnp.arange(half, dtype=F32) / half)
    ang = pos[:, None] * inv[None, :]
    cos, sin = jnp.cos(ang), jnp.sin(ang)
    one = jnp.ones((l, NOPE_B), F32)
    zn = jnp.zeros((l, NOPE_B), F32)
    zh = jnp.zeros((l, half), F32)
    zp = jnp.zeros((l, SLOT_B - QK_B), F32)
    rc = jnp.concatenate([one, cos, cos, zp], axis=1)
    ra = jnp.concatenate([zn, -sin, zh, zp], axis=1)
    rb = jnp.concatenate([zn, zh, sin, zp], axis=1)
    m = jnp.arange(2 * l)
    p_idx = jnp.where(m < l, m, 2 * l - m)
    tlin = jnp.linspace(0.0, 1.0, l, dtype=F32)
    tt = tlin[jnp.minimum(p_idx, l - 1)]
    angf = (2.0 * math.pi / l) * p_idx.astype(F32)
    valid = (m != l).astype(F32)
    bands = (HYENA_EMB - 1) // 2
    fcol = jnp.linspace(1e-4, bands - 1, bands, dtype=F32)[:, None]
    deltas = jnp.abs(jnp.linspace(math.log(HYENA_TARGET) / HYENA_FAST_DECAY,
                                  math.log(HYENA_TARGET) / HYENA_SLOW_DECAY, WIDTH_C, dtype=F32))[:, None]
    return dict(rc=rc, ra=ra, rb=rb, pos=jnp.stack([tt, angf, valid]), fcol=fcol, deltas=deltas)


def _static_tables():
    hidx = jnp.arange(2 * LANE)
    head_of_row = jnp.where(hidx % LANE < 4, (hidx // LANE) * 4 + hidx % LANE, -1)
    expand = (head_of_row[:, None] == (jnp.arange(WIDTH_A) // HEAD_DIM_A)[None, :]).astype(BF16)
    slopes = jnp.exp2(-ALIBI_MAX * (jnp.arange(HEADS_A, dtype=F32) + 1.0) / HEADS_A)
    return expand, slopes


def _layer(x2d, lp, sc, expand, slopes, b, l):
    proj = _inproj(x2d, lp['norm_attn_g'], lp['w_r'])
    oas, lses = [], []
    for g, dil in enumerate(DILATIONS):
        o, lse = _mixer_a_group(proj, lp['a_qg'][g], lp['a_kg'][g], -slopes * (dil * LOG2E), b, l, g, dil)
        oas.append(o)
        lses.append(lse)
    qp, kp, vt = _mla_prep(proj, lp['gql'], lp['gkl'], lp['wq'], lp['wk'], lp['wv'], lp['gq'], lp['gk'],
                           sc['rc'], sc['ra'], sc['rb'], b, l)
    yb = _mla_attn(qp, kp, vt, b, l)
    u, x0 = _hyena_prep(proj, lp['c_conv_w'], lp['c_conv_b'], b, l)
    kf = _hyena_filter(sc['pos'], sc['fcol'], lp['w1t'], lp['w1c'], lp['w1s'], lp['b1'], lp['w2t'], lp['b2'],
                       lp['w3t'], lp['fr'], sc['deltas'], l)
    nb = l // TOEP
    u_t = jnp.transpose(u.reshape(b, nb, TOEP, WIDTH_C), (3, 1, 0, 2)).reshape(WIDTH_C, nb * b, TOEP)
    cv_t = _hyena_conv(kf, u_t, nb, b)
    cv = jnp.transpose(cv_t.reshape(WIDTH_C, nb, b, TOEP), (2, 1, 3, 0)).reshape(b * l, WIDTH_C)
    xm = _merge(oas, lses, yb, cv, u, x0, proj, x2d, expand, lp['skip'], lp['bg'],
                lp['wa'], lp['wb'], lp['wc'], lp['wo'])
    act = _ffn_up(xm, lp['norm_ffn_g'], lp['w_up_a'], lp['w_up_g'], lp['cwa'], lp['cwg'], lp['cba'], lp['cbg'], l)
    return _ffn_down(xm, act, lp['w_down'])


def _trunk(x, layer_params, expand, slopes):
    b, l, _ = x.shape
    sc = _seq_constants(l)
    y = x.reshape(b * l, D_MODEL)
    for lp in layer_params:
        y = _layer(y, lp, sc, expand, slopes, b, l)
    return y.reshape(b, l, D_MODEL)


def kernel(x_prompt, x_sample, norm_attn_g, w_in, b_gate, a_q_g, a_k_g, b_q_lat_g, b_kv_lat_g, b_w_uq, b_w_ukv, b_q_g, b_k_g, c_conv_w, c_conv_b, c_w1, c_b1, c_w2, c_b2, c_w3, c_freq, c_skip, w_br_a, w_br_b, w_br_c, w_out, norm_ffn_g, w_up, ffn_conv_w, ffn_conv_b, w_down):
    p = dict(norm_attn_g=norm_attn_g, w_in=w_in, b_gate=b_gate, a_q_g=a_q_g, a_k_g=a_k_g, b_q_lat_g=b_q_lat_g,
             b_kv_lat_g=b_kv_lat_g, b_w_uq=b_w_uq, b_w_ukv=b_w_ukv, b_q_g=b_q_g, b_k_g=b_k_g, c_conv_w=c_conv_w,
             c_conv_b=c_conv_b, c_w1=c_w1, c_b1=c_b1, c_w2=c_w2, c_b2=c_b2, c_w3=c_w3, c_freq=c_freq,
             c_skip=c_skip, w_br_a=w_br_a, w_br_b=w_br_b, w_br_c=w_br_c, w_out=w_out, norm_ffn_g=norm_ffn_g,
             w_up=w_up, ffn_conv_w=ffn_conv_w, ffn_conv_b=ffn_conv_b, w_down=w_down)
    layer_params = [_layer_params(p, l) for l in range(DEPTH)]
    expand, slopes = _static_tables()
    return _trunk(x_prompt, layer_params, expand, slopes), _trunk(x_sample, layer_params, expand, slopes)
```

```python
import functools
import math

import jax
import jax.numpy as jnp
from jax import lax
from jax.experimental import pallas as pl
from jax.experimental.pallas import tpu as pltpu

F32 = jnp.float32
BF16 = jnp.bfloat16

D_MODEL = 1024
DEPTH = 2
EPS = 1e-6
HEADS_A = 8
HEAD_DIM_A = 64
DILATIONS = (1, 4, 16)
HALF_A = 64
WIDTH_A = HEADS_A * HEAD_DIM_A
ALIBI_MAX = 8.0
HEADS_B = 8
NOPE_B = 64
ROPE_B = 32
QK_B = NOPE_B + ROPE_B
V_B = 64
Q_RANK = 256
KV_RANK = 128
ROPE_THETA = 10000.0
WIDTH_B = HEADS_B * V_B
WIDTH_C = 512
HYENA_EMB = 33
HYENA_HID = 64
HYENA_FAST_DECAY = 0.3
HYENA_SLOW_DECAY = 1.5
HYENA_TARGET = 1e-2
D_FF = 2816
COLS_A = 3 * 3 * WIDTH_A
COLS_B = Q_RANK + KV_RANK + ROPE_B
COLS_C = 3 * WIDTH_C
COLS_G = 3 * D_MODEL
OFF_A = 0
OFF_C = COLS_A
OFF_G = OFF_C + COLS_C
OFF_B = OFF_G + COLS_G
PROJ_W = OFF_B + 512
LANE = 128
SLOT_B = 128
VROWS = 80
TOEP = 256
LOG2E = math.log2(math.e)
LN2 = math.log(2.0)
MASKED = 1e32
VMEM_LIMIT = 48 * 1024 * 1024


def _cparams(sem):
    return pltpu.CompilerParams(dimension_semantics=sem, vmem_limit_bytes=VMEM_LIMIT)


def _nt_dot(a, b):
    return lax.dot_general(a, b, (((1,), (1,)), ((), ())), preferred_element_type=F32)


def _dot(a, b):
    return jnp.dot(a, b, preferred_element_type=F32)


def _dot_exact(a, b):
    return jnp.dot(a, b, preferred_element_type=F32, precision=lax.Precision.HIGHEST)


def _inproj_kernel(x_ref, g_ref, w_ref, o_ref, h_ref):
    @pl.when(pl.program_id(1) == 0)
    def _():
        x = x_ref[...]
        ms = jnp.mean(x * x, axis=-1, keepdims=True)
        h_ref[...] = (x * lax.rsqrt(ms + EPS) * g_ref[...]).astype(BF16)

    o_ref[...] = _dot(h_ref[...], w_ref[...]).astype(o_ref.dtype)


def _inproj(x2d, gain, w_r):
    t = x2d.shape[0]
    tm, tn = 1024, PROJ_W // 4
    return pl.pallas_call(
        _inproj_kernel,
        grid=(t // tm, PROJ_W // tn),
        in_specs=[
            pl.BlockSpec((tm, D_MODEL), lambda i, j: (i, 0)),
            pl.BlockSpec((1, D_MODEL), lambda i, j: (0, 0)),
            pl.BlockSpec((D_MODEL, tn), lambda i, j: (0, j)),
        ],
        out_specs=pl.BlockSpec((tm, tn), lambda i, j: (i, j)),
        out_shape=jax.ShapeDtypeStruct((t, PROJ_W), BF16),
        scratch_shapes=[pltpu.VMEM((tm, D_MODEL), BF16)],
        compiler_params=_cparams(("parallel", "arbitrary")),
        name="inproj",
    )(x2d, gain, w_r)


def _dil_attn_kernel(negc_ref, q_ref, k_ref, v_ref, qg_ref, kg_ref, o_ref, lse_ref,
                     qn_ref, kn_ref, vn_ref, stg_ref, ost_ref, bias_ref, *, l, dil):
    hb = pl.program_id(1)
    lu = l // dil
    cw = q_ref.shape[-1]
    win = min(2 * LANE, lu)
    nq = lu // LANE
    row = lax.broadcasted_iota(jnp.int32, (cw, cw), 0) // HEAD_DIM_A
    col = lax.broadcasted_iota(jnp.int32, (cw, cw), 1) // HEAD_DIM_A
    seg_ones = jnp.where(row == col, 1.0, 0.0).astype(BF16)

    chunk = 512
    per = chunk // dil

    def norm_body(c, carry):
        r0 = pl.multiple_of(c * chunk, chunk)
        for src, gain, dst in ((q_ref, qg_ref, qn_ref), (k_ref, kg_ref, kn_ref), (v_ref, None, vn_ref)):
            if gain is None and dil == 1:
                continue
            x = src[0, pl.ds(r0, chunk), :].astype(F32)
            if gain is not None:
                ssq = _dot((x * x).astype(BF16), seg_ones)
                x = x * lax.rsqrt(ssq * (1.0 / HEAD_DIM_A) + EPS) * gain[...]
            if dil == 1:
                dst[pl.ds(r0, chunk), :] = x.astype(BF16)
            else:
                for h in range(cw // LANE):
                    stg_ref[h] = x[:, h * LANE:(h + 1) * LANE]
                for r in range(dil):
                    d0 = pl.multiple_of(r * lu + c * per, per)
                    for h in range(cw // LANE):
                        dst[pl.ds(d0, per), h * LANE:(h + 1) * LANE] = (
                            stg_ref[h, pl.ds(r, per, stride=dil), :].astype(BF16))
        return carry

    lax.fori_loop(0, l // chunk, norm_body, 0)

    lane_lo = lax.broadcasted_iota(jnp.int32, (1, LANE), 1) < HEAD_DIM_A
    i_q = lax.broadcasted_iota(jnp.int32, (LANE, win), 0)
    j_k = lax.broadcasted_iota(jnp.int32, (LANE, win), 1)
    lane_id = lax.broadcasted_iota(jnp.int32, (LANE, LANE), 1)

    npair = cw // LANE
    tiles_per_step = 2
    for variant in range(3):
        absrel = jnp.abs(j_k - i_q - variant * HALF_A).astype(F32)
        absrel = jnp.where(absrel <= float(HALF_A), absrel, MASKED)
        for head in range(2 * npair):
            bias_ref[variant * (2 * npair) + head] = absrel * negc_ref[4 * hb + head]

    def step_body(step, carry):
        tiles = []
        for u in range(tiles_per_step):
            idx = step * tiles_per_step + u
            r = idx // nq
            q0 = pl.multiple_of((idx % nq) * LANE, LANE)
            w0 = pl.multiple_of(jnp.clip(q0 - HALF_A, 0, lu - win), HALF_A)
            base = pl.multiple_of(r * lu, LANE)
            q = qn_ref[pl.ds(base + q0, LANE), :]
            kw = kn_ref[pl.ds(base + w0, win), :]
            vw = v_ref[0, pl.ds(w0, win), :] if dil == 1 else vn_ref[pl.ds(base + w0, win), :]
            scores = []
            for pair in range(npair):
                qp = q[:, pair * LANE:(pair + 1) * LANE]
                zero = jnp.zeros_like(qp)
                qs = jnp.concatenate([jnp.where(lane_lo, qp, zero), jnp.where(lane_lo, zero, qp)], axis=0)
                scores.append(_nt_dot(qs, kw[:, pair * LANE:(pair + 1) * LANE]))
            tiles.append((r, q0, w0, vw, scores))

        probs = []
        for r, q0, w0, vw, scores in tiles:
            variant = (q0 - w0) // HALF_A
            m_tile = jnp.zeros((LANE, LANE), F32)
            den_tile = jnp.ones((LANE, LANE), F32)
            tile_p = []
            for pair in range(npair):
                for a in range(2):
                    head = 2 * pair + a
                    sa = scores[pair][a * LANE:(a + 1) * LANE] + bias_ref[variant * (2 * npair) + head]
                    m = jnp.max(sa, axis=1, keepdims=True)
                    p = jnp.exp2(sa - m)
                    den = jnp.sum(p, axis=1, keepdims=True)
                    tile_p.append((p.astype(BF16), 1.0 / den))
                    m_tile = jnp.where(lane_id == head, m, m_tile)
                    den_tile = jnp.where(lane_id == head, den, den_tile)
            lse_tile = jnp.where(lane_id < 2 * npair, (m_tile + jnp.log2(den_tile)) * LN2, 0.0)
            probs.append((tile_p, lse_tile))

        for (r, q0, w0, vw, scores), (tile_p, lse_tile) in zip(tiles, probs):
            outs = []
            for pair in range(npair):
                vp = vw[:, pair * LANE:(pair + 1) * LANE]
                vzero = jnp.zeros_like(vp)
                (pa, ra), (pb, rb) = tile_p[2 * pair], tile_p[2 * pair + 1]
                outs.append(_dot(pa, jnp.where(lane_lo, vp, vzero)) * ra + _dot(pb, jnp.where(lane_lo, vzero, vp)) * rb)
            if dil == 1:
                o_ref[0, pl.ds(q0, LANE), :] = jnp.concatenate(outs, axis=1).astype(o_ref.dtype)
                lse_ref[0, pl.ds(q0, LANE), :] = lse_tile
            else:
                for h, o_pair in enumerate(outs):
                    ost_ref[h, pl.ds(r + q0 * dil, LANE, stride=dil), :] = o_pair
                lse_ref[0, pl.ds(r + q0 * dil, LANE, stride=dil), :] = lse_tile
        return carry

    lax.fori_loop(0, l // (LANE * tiles_per_step), step_body, 0)

    if dil != 1:
        def out_body(c, carry):
            r0 = pl.multiple_of(c * chunk, chunk)
            for h in range(cw // LANE):
                o_ref[0, pl.ds(r0, chunk), h * LANE:(h + 1) * LANE] = (
                    ost_ref[h, pl.ds(r0, chunk), :].astype(o_ref.dtype))
            return carry

        lax.fori_loop(0, l // chunk, out_body, 0)


def _mixer_a_group(proj, qg, kg, negc, b, l, g, dil):
    pv = proj.reshape(b, l, PROJ_W)
    cq, ck, cv = (OFF_A + (0 + g) * 512) // 256, (OFF_A + (3 + g) * 512) // 256, (OFF_A + (6 + g) * 512) // 256

    def in_spec(c0):
        return pl.BlockSpec((1, l, 256), lambda bi, hb: (bi, 0, c0 + hb))

    stage_rows = (8, 8) if dil == 1 else (512, l)
    o, lse = pl.pallas_call(
        functools.partial(_dil_attn_kernel, l=l, dil=dil),
        grid=(b, 2),
        in_specs=[
            pl.BlockSpec(memory_space=pltpu.SMEM),
            in_spec(cq), in_spec(ck), in_spec(cv),
            pl.BlockSpec((1, 256), lambda bi, hb: (0, 0)),
            pl.BlockSpec((1, 256), lambda bi, hb: (0, 0)),
        ],
        out_specs=[
            pl.BlockSpec((1, l, 256), lambda bi, hb: (bi, 0, hb)),
            pl.BlockSpec((1, l, LANE), lambda bi, hb: (bi, 0, hb)),
        ],
        out_shape=[
            jax.ShapeDtypeStruct((b, l, WIDTH_A), BF16),
            jax.ShapeDtypeStruct((b, l, 2 * LANE), F32),
        ],
        scratch_shapes=[pltpu.VMEM((l, 256), BF16), pltpu.VMEM((l, 256), BF16),
                        pltpu.VMEM((l if dil != 1 else 16, 256), BF16),
                        pltpu.VMEM((2, stage_rows[0], LANE), F32), pltpu.VMEM((2, stage_rows[1], LANE), F32),
                        pltpu.VMEM((12, LANE, min(2 * LANE, l // dil)), F32)],
        compiler_params=_cparams(("parallel", "arbitrary")),
        name=f"dilated_attention_g{g}",
    )(negc, pv, pv, pv, qg, kg)
    return o.reshape(b * l, WIDTH_A), lse.reshape(b * l, 2 * LANE)


def _mla_prep_kernel(pb_ref, gql_ref, gkl_ref, wq_ref, wk_ref, wv_ref, gq_ref, gk_ref, rc_ref, ra_ref, rb_ref,
                     sel_ref, exp_ref, qt_ref, k_ref, vt_ref):
    c = pb_ref[...].astype(F32)
    cq = c[:, :Q_RANK]
    cqn = (cq * lax.rsqrt(jnp.mean(cq * cq, axis=-1, keepdims=True) + EPS) * gql_ref[...]).astype(BF16)
    ckv = c[:, Q_RANK:Q_RANK + KV_RANK]
    ckvn = (ckv * lax.rsqrt(jnp.mean(ckv * ckv, axis=-1, keepdims=True) + EPS) * gkl_ref[...]).astype(BF16)
    q = _dot(cqn, wq_ref[...])
    k = _dot(jnp.concatenate([ckvn, pb_ref[:, Q_RANK + KV_RANK:]], axis=1), wk_ref[...])
    v = _dot(ckvn, wv_ref[...])
    rc, ra, rb = rc_ref[...], ra_ref[...], rb_ref[...]

    lane = lax.broadcasted_iota(jnp.int32, (1, LANE), 1)
    half = ROPE_B // 2

    def finish(x):
        ssq = _dot((x * x).astype(BF16), sel_ref[...])
        r = lax.rsqrt(ssq * (1.0 / QK_B) + EPS)
        r_hi = r.astype(BF16)
        r_lo = (r - r_hi.astype(F32)).astype(BF16)
        scale = _dot(jnp.where(lane < HEADS_B, r_hi, r_lo), exp_ref[...])
        return x * scale

    def rope(xn):
        return xn * rc + pltpu.roll(xn, SLOT_B - half, 1) * ra + pltpu.roll(xn, half, 1) * rb

    qn = finish(q) * gq_ref[...]
    kn = finish(k) * gk_ref[...]
    for h in range(HEADS_B):
        sl = slice(h * SLOT_B, (h + 1) * SLOT_B)
        qt_ref[0, sl, :] = rope(qn[:, sl]).T.astype(qt_ref.dtype)
        k_ref[:, sl] = rope(kn[:, sl]).astype(k_ref.dtype)
    vt_ref[0] = v.T.astype(vt_ref.dtype)


def _mla_prep(proj, gql, gkl, wq, wk, wv, gq, gk, rc, ra, rb, b, l):
    t = b * l
    tm = 512
    per_seq = l // tm
    const = lambda i: (0, 0)
    slot_of_lane = jnp.arange(HEADS_B * SLOT_B) // SLOT_B
    col = jnp.arange(LANE)
    sel = ((col[None, :] % HEADS_B == slot_of_lane[:, None]) & (col[None, :] < 2 * HEADS_B)).astype(BF16)
    spread = sel.T
    return pl.pallas_call(
        _mla_prep_kernel,
        grid=(t // tm,),
        in_specs=[
            pl.BlockSpec((tm, 512), lambda i: (i, OFF_B // 512)),
            pl.BlockSpec((1, Q_RANK), const), pl.BlockSpec((1, KV_RANK), const),
            pl.BlockSpec((Q_RANK, HEADS_B * SLOT_B), const),
            pl.BlockSpec((256, HEADS_B * SLOT_B), const),
            pl.BlockSpec((KV_RANK, WIDTH_B), const),
            pl.BlockSpec((1, HEADS_B * SLOT_B), const), pl.BlockSpec((1, HEADS_B * SLOT_B), const),
            pl.BlockSpec((tm, SLOT_B), lambda i: (i % per_seq, 0)),
            pl.BlockSpec((tm, SLOT_B), lambda i: (i % per_seq, 0)),
            pl.BlockSpec((tm, SLOT_B), lambda i: (i % per_seq, 0)),
            pl.BlockSpec((HEADS_B * SLOT_B, LANE), const), pl.BlockSpec((LANE, HEADS_B * SLOT_B), const),
        ],
        out_specs=[
            pl.BlockSpec((1, HEADS_B * SLOT_B, tm), lambda i: (i // per_seq, 0, i % per_seq)),
            pl.BlockSpec((tm, HEADS_B * SLOT_B), lambda i: (i, 0)),
            pl.BlockSpec((1, WIDTH_B, tm), lambda i: (i // per_seq, 0, i % per_seq)),
        ],
        out_shape=[
            jax.ShapeDtypeStruct((b, HEADS_B * SLOT_B, l), BF16),
            jax.ShapeDtypeStruct((t, HEADS_B * SLOT_B), BF16),
            jax.ShapeDtypeStruct((b, WIDTH_B, l), BF16),
        ],
        compiler_params=_cparams(("parallel",)),
        name="latent_prep",
    )(proj, gql, gkl, wq, wk, wv, gq, gk, rc, ra, rb, sel, spread)


def _mla_attn_kernel(qt_ref, k_ref, vt_ref, o_ref, vaug_ref, *, tk):
    nkv = vaug_ref.shape[1]
    tq = qt_ref.shape[2]

    @pl.when(pl.program_id(2) == 0)
    def _():
        ones_rows = jnp.where(lax.broadcasted_iota(jnp.int32, (VROWS - V_B, tk), 0) == 0, 1.0, 0.0).astype(BF16)
        for a in range(2):
            for j in range(nkv):
                vaug_ref[a, j, 0:V_B, :] = vt_ref[0, a * V_B:(a + 1) * V_B, j * tk:(j + 1) * tk]
                vaug_ref[a, j, V_B:VROWS, :] = ones_rows

    qh = [qt_ref[0, a * SLOT_B:(a + 1) * SLOT_B, :] for a in range(2)]
    m = [jnp.full((1, tq), -jnp.inf, F32) for _ in range(2)]
    acc = [jnp.zeros((VROWS, tq), F32) for _ in range(2)]

    def scores(j):
        return [_dot(k_ref[0, j * tk:(j + 1) * tk, a * SLOT_B:(a + 1) * SLOT_B], qh[a]) for a in range(2)]

    def values(j, ps):
        return [_dot(vaug_ref[a, j], ps[a]) for a in range(2)]

    st = scores(0)
    ps = None
    for j in range(nkv):
        st_next = scores(j + 1) if j + 1 < nkv else None
        pv = values(j - 1, ps) if j > 0 else None
        ps = []
        for a in range(2):
            mn = jnp.maximum(m[a], jnp.max(st[a], axis=0, keepdims=True))
            ps.append(jnp.exp2(st[a] - mn).astype(BF16))
            if pv is not None:
                acc[a] = (acc[a] + pv[a]) * jnp.exp2(m[a] - mn)
            m[a] = mn
        st = st_next
    pv = values(nkv - 1, ps)
    acc = [acc[a] + pv[a] for a in range(2)]
    outs = [acc[a][0:V_B] * (1.0 / acc[a][V_B:V_B + 1]) for a in range(2)]
    o_ref[0] = jnp.concatenate(outs, axis=0).T.astype(o_ref.dtype)


def _mla_attn(qt, kp, vt, b, l):
    tq, tk = 256, 256
    k3 = kp.reshape(b, l, HEADS_B * SLOT_B)
    out = pl.pallas_call(
        functools.partial(_mla_attn_kernel, tk=tk),
        grid=(b, HEADS_B // 2, l // tq),
        in_specs=[
            pl.BlockSpec((1, 2 * SLOT_B, tq), lambda bi, p, qi: (bi, p, qi)),
            pl.BlockSpec((1, l, 2 * SLOT_B), lambda bi, p, qi: (bi, 0, p)),
            pl.BlockSpec((1, 2 * V_B, l), lambda bi, p, qi: (bi, p, 0)),
        ],
        out_specs=pl.BlockSpec((1, tq, 2 * V_B), lambda bi, p, qi: (bi, qi, p)),
        out_shape=jax.ShapeDtypeStruct((b, l, WIDTH_B), BF16),
        scratch_shapes=[pltpu.VMEM((2, l // tk, VROWS, tk), BF16)],
        compiler_params=_cparams(("parallel", "parallel", "arbitrary")),
        name="latent_attention",
    )(qt, k3, vt)
    return out.reshape(b * l, WIDTH_B)


def _shift_rows(x, prev_row, next_row):
    n = x.shape[0]
    rid = lax.broadcasted_iota(jnp.int32, x.shape, 0)
    xm = jnp.where(rid == 0, prev_row, pltpu.roll(x, 1, 0))
    xp = jnp.where(rid == n - 1, next_row, pltpu.roll(x, n - 1, 0))
    return xm, xp


def _hyena_prep_kernel(pc_ref, prev_ref, next_ref, w_ref, b_ref, u_ref, x0_ref, *, per_seq):
    i = pl.program_id(0) % per_seq
    x = pc_ref[...].astype(F32)
    hr = prev_ref.shape[0]
    prev_row = jnp.where(i == 0, 0.0, prev_ref[...].astype(F32)[hr - 1:hr, :])
    next_row = jnp.where(i == per_seq - 1, 0.0, next_ref[...].astype(F32)[0:1, :])
    xm, xp = _shift_rows(x, prev_row, next_row)
    y = xm * w_ref[0:1, :] + x * w_ref[1:2, :] + xp * w_ref[2:3, :] + b_ref[...]
    x0_ref[...] = y[:, :WIDTH_C].astype(x0_ref.dtype)
    u_ref[...] = (y[:, 2 * WIDTH_C:] * y[:, WIDTH_C:2 * WIDTH_C]).astype(u_ref.dtype)


def _hyena_prep(proj, conv_w, conv_b, b, l):
    t = b * l
    tm, hr = 512, 16
    per_seq = l // tm
    nh = tm // hr
    cb = OFF_C // COLS_C
    return pl.pallas_call(
        functools.partial(_hyena_prep_kernel, per_seq=per_seq),
        grid=(t // tm,),
        in_specs=[
            pl.BlockSpec((tm, COLS_C), lambda i: (i, cb)),
            pl.BlockSpec((hr, COLS_C), lambda i: (jnp.maximum(i * nh - 1, 0), cb)),
            pl.BlockSpec((hr, COLS_C), lambda i: (jnp.minimum((i + 1) * nh, t // hr - 1), cb)),
            pl.BlockSpec((3, COLS_C), lambda i: (0, 0)),
            pl.BlockSpec((1, COLS_C), lambda i: (0, 0)),
        ],
        out_specs=[pl.BlockSpec((tm, WIDTH_C), lambda i: (i, 0)), pl.BlockSpec((tm, WIDTH_C), lambda i: (i, 0))],
        out_shape=[jax.ShapeDtypeStruct((t, WIDTH_C), BF16), jax.ShapeDtypeStruct((t, WIDTH_C), BF16)],
        compiler_params=_cparams(("parallel",)),
        name="hyena_prep",
    )(proj, proj, proj, conv_w, conv_b)


def _hyena_filter_kernel(pos_ref, fcol_ref, w1t_ref, w1c_ref, w1s_ref, b1_ref, w2_ref, b2_ref, w3_ref, fr_ref,
                         dl_ref, o_ref):
    tt = pos_ref[0:1, :]
    ang = pos_ref[1:2, :]
    valid = pos_ref[2:3, :]
    arg = fcol_ref[...] * ang
    pre1 = w1t_ref[...] * tt + _dot_exact(w1c_ref[...], jnp.cos(arg)) + _dot_exact(w1s_ref[...], jnp.sin(arg))
    fr = fr_ref[...]
    hid = jnp.sin(fr * (pre1 + b1_ref[...]))
    hid = jnp.sin(fr * (_dot_exact(w2_ref[...], hid) + b2_ref[...]))
    filt = _dot_exact(w3_ref[0], hid)
    o_ref[...] = filt * jnp.exp(-dl_ref[...] * tt) * valid


def _hyena_filter(pos, fcol, w1t, w1c, w1s, b1, w2t, b2, w3t, fr, dl, l):
    n = 2048
    const = lambda j: (0, 0)
    return pl.pallas_call(
        _hyena_filter_kernel,
        grid=(2 * l // n,),
        in_specs=[
            pl.BlockSpec((3, n), lambda j: (0, j)),
            pl.BlockSpec((16, 1), const),
            pl.BlockSpec((HYENA_HID, 1), const), pl.BlockSpec((HYENA_HID, 16), const),
            pl.BlockSpec((HYENA_HID, 16), const), pl.BlockSpec((HYENA_HID, 1), const),
            pl.BlockSpec((HYENA_HID, HYENA_HID), const), pl.BlockSpec((HYENA_HID, 1), const),
            pl.BlockSpec((1, WIDTH_C, HYENA_HID), lambda j: (j // (l // n), 0, 0)),
            pl.BlockSpec((HYENA_HID, 1), const), pl.BlockSpec((WIDTH_C, 1), const),
        ],
        out_specs=pl.BlockSpec((WIDTH_C, n), lambda j: (0, j)),
        out_shape=jax.ShapeDtypeStruct((WIDTH_C, 2 * l), F32),
        compiler_params=_cparams(("parallel",)),
        name="hyena_filter",
    )(pos, fcol, w1t, w1c, w1s, b1, w2t, b2, w3t, fr, dl)


def _hyena_conv_kernel(kf_ref, u_ref, o_ref, uf_ref, acc_ref, *, nb, bsz):
    p = TOEP
    n2 = 2 * nb * p

    def channel(c, carry):
        krow = kf_ref[pl.ds(c, 1), :]
        uf_ref[...] = u_ref[c].astype(F32)
        acc_ref[...] = jnp.zeros_like(acc_ref)
        for d in range(-(nb - 1), nb):
            a0 = (d * p) % n2
            b0 = ((d - 1) * p) % n2
            seg = jnp.concatenate([krow[:, a0:a0 + p], krow[:, b0:b0 + p]], axis=1)
            rolled = pltpu.roll(jnp.broadcast_to(seg, (p, 2 * p)), 0, 1, stride=1, stride_axis=0)
            toep = rolled[:, :p].astype(BF16)
            rows = (nb - abs(d)) * bsz
            src = 0 if d >= 0 else -d * bsz
            dst = d * bsz if d >= 0 else 0
            acc_ref[dst:dst + rows, :] += _dot(uf_ref[src:src + rows, :].astype(BF16), toep)
        o_ref[c] = acc_ref[...].astype(o_ref.dtype)
        return carry

    lax.fori_loop(0, kf_ref.shape[0], channel, 0)


def _hyena_conv(kf, u_t, nb, bsz):
    cblk = 8
    rows = nb * bsz
    return pl.pallas_call(
        functools.partial(_hyena_conv_kernel, nb=nb, bsz=bsz),
        grid=(WIDTH_C // cblk,),
        in_specs=[
            pl.BlockSpec((cblk, 2 * nb * TOEP), lambda i: (i, 0)),
            pl.BlockSpec((cblk, rows, TOEP), lambda i: (i, 0, 0)),
        ],
        out_specs=pl.BlockSpec((cblk, rows, TOEP), lambda i: (i, 0, 0)),
        out_shape=jax.ShapeDtypeStruct((WIDTH_C, rows, TOEP), BF16),
        scratch_shapes=[pltpu.VMEM((rows, TOEP), F32), pltpu.VMEM((rows, TOEP), F32)],
        compiler_params=_cparams(("parallel",)),
        name="hyena_conv",
    )(kf, u_t)


def _merge_kernel(oa0_ref, oa1_ref, oa2_ref, l0_ref, l1_ref, l2_ref, yb_ref, cv_ref, u_ref, x0_ref, pg_ref, x_ref,
                  ex_ref, skip_ref, bg_ref, wa_ref, wb_ref, wc_ref, wo_ref, o_ref):
    lses = [l0_ref[...], l1_ref[...], l2_ref[...]]
    mx = jnp.maximum(jnp.maximum(lses[0], lses[1]), lses[2])
    num = None
    den = None
    for lse, oa in zip(lses, (oa0_ref, oa1_ref, oa2_ref)):
        w = _dot(jnp.exp(lse - mx).astype(BF16), ex_ref[...])
        num = w * oa[...].astype(F32) if num is None else num + w * oa[...].astype(F32)
        den = w if den is None else den + w
    ya = (num / den).astype(BF16)
    u = u_ref[...].astype(F32)
    yc = (x0_ref[...].astype(F32) * (cv_ref[...].astype(F32) + skip_ref[...] * u)).astype(BF16)
    mixed = None
    for i, (y, w_ref) in enumerate(((ya, wa_ref), (yb_ref[...], wb_ref), (yc, wc_ref))):
        sl = slice(i * D_MODEL, (i + 1) * D_MODEL)
        gate = jax.nn.sigmoid(pg_ref[:, sl].astype(F32) + bg_ref[:, sl])
        term = gate * _dot(y, w_ref[...])
        mixed = term if mixed is None else mixed + term
    o_ref[...] = x_ref[...] + _dot(mixed.astype(BF16), wo_ref[...])


def _merge(oas, lses, yb, cv, u, x0, proj, x2d, expand, skip, bg, wa, wb, wc, wo):
    t = x2d.shape[0]
    tm = 512
    row = lambda i: (i, 0)
    const = lambda i: (0, 0)
    half = pl.BlockSpec((tm, 512), row)
    return pl.pallas_call(
        _merge_kernel,
        grid=(t // tm,),
        in_specs=[
            half, half, half,
            pl.BlockSpec((tm, 2 * LANE), row), pl.BlockSpec((tm, 2 * LANE), row), pl.BlockSpec((tm, 2 * LANE), row),
            half, half, half, half,
            pl.BlockSpec((tm, COLS_G), lambda i: (i, OFF_G // COLS_G)),
            pl.BlockSpec((tm, D_MODEL), row),
            pl.BlockSpec((2 * LANE, WIDTH_A), const),
            pl.BlockSpec((1, WIDTH_C), const), pl.BlockSpec((1, COLS_G), const),
            pl.BlockSpec((WIDTH_A, D_MODEL), const), pl.BlockSpec((WIDTH_B, D_MODEL), const),
            pl.BlockSpec((WIDTH_C, D_MODEL), const), pl.BlockSpec((D_MODEL, D_MODEL), const),
        ],
        out_specs=pl.BlockSpec((tm, D_MODEL), row),
        out_shape=jax.ShapeDtypeStruct((t, D_MODEL), F32),
        compiler_params=_cparams(("parallel",)),
        name="branch_merge",
    )(*oas, *lses, yb, cv, u, x0, proj, x2d, expand, skip, bg, wa, wb, wc, wo)


def _ffn_up_kernel(x_ref, prev_ref, next_ref, g_ref, w_ref, cw_ref, cb_ref, o_ref, h_ref, *, per_seq, halo):
    tm = x_ref.shape[0]
    i = pl.program_id(0) % per_seq

    def norm(x):
        return x * lax.rsqrt(jnp.mean(x * x, axis=-1, keepdims=True) + EPS) * g_ref[...]

    h_ref[0:halo, :] = jnp.where(i == 0, 0.0, norm(prev_ref[...])).astype(BF16)
    h_ref[halo:halo + tm, :] = norm(x_ref[...]).astype(BF16)
    h_ref[halo + tm:, :] = jnp.where(i == per_seq - 1, 0.0, norm(next_ref[...])).astype(BF16)
    cs = 2 * LANE
    nsub = w_ref.shape[1] // cs
    rows = tm // 2
    n = rows + 2 * halo
    stages = [(s, r) for s in range(nsub) for r in range(2)]

    def dot(stage):
        s, r = stage
        return _dot(h_ref[r * rows:r * rows + n, :], w_ref[:, s * cs:(s + 1) * cs])

    y = dot(stages[0])
    for k, (s, r) in enumerate(stages):
        y_next = dot(stages[k + 1]) if k + 1 < len(stages) else None
        sl = slice(s * cs, (s + 1) * cs)
        ym = pltpu.roll(y, 1, 0)[halo:halo + rows]
        yp = pltpu.roll(y, n - 1, 0)[halo:halo + rows]
        c = ym * cw_ref[0:1, sl] + y[halo:halo + rows] * cw_ref[1:2, sl] + yp * cw_ref[2:3, sl] + cb_ref[:, sl]
        o_ref[r * rows:(r + 1) * rows, s * LANE:(s + 1) * LANE] = (
            jax.nn.gelu(c[:, :LANE]) * c[:, LANE:]).astype(o_ref.dtype)
        y = y_next


def _ffn_up(x2d, gain, w, cw, cb, l):
    t = x2d.shape[0]
    tm, halo = 512, 16
    per_seq = l // tm
    nh = tm // halo
    const = lambda i: (0, 0)
    return pl.pallas_call(
        functools.partial(_ffn_up_kernel, per_seq=per_seq, halo=halo),
        grid=(t // tm,),
        in_specs=[
            pl.BlockSpec((tm, D_MODEL), lambda i: (i, 0)),
            pl.BlockSpec((halo, D_MODEL), lambda i: (jnp.maximum(i * nh - 1, 0), 0)),
            pl.BlockSpec((halo, D_MODEL), lambda i: (jnp.minimum((i + 1) * nh, t // halo - 1), 0)),
            pl.BlockSpec((1, D_MODEL), const),
            pl.BlockSpec((D_MODEL, 2 * D_FF), const),
            pl.BlockSpec((3, 2 * D_FF), const),
            pl.BlockSpec((1, 2 * D_FF), const),
        ],
        out_specs=pl.BlockSpec((tm, D_FF), lambda i: (i, 0)),
        out_shape=jax.ShapeDtypeStruct((t, D_FF), BF16),
        scratch_shapes=[pltpu.VMEM((tm + 2 * halo, D_MODEL), BF16)],
        compiler_params=_cparams(("parallel",)),
        name="ffn_up",
    )(x2d, x2d, x2d, gain, w, cw, cb)


def _ffn_down_kernel(x_ref, a_ref, w_ref, o_ref):
    o_ref[...] = x_ref[...] + _dot(a_ref[...], w_ref[...])


def _ffn_down(x2d, act, w):
    t = x2d.shape[0]
    tm = 512
    return pl.pallas_call(
        _ffn_down_kernel,
        grid=(t // tm,),
        in_specs=[
            pl.BlockSpec((tm, D_MODEL), lambda i: (i, 0)),
            pl.BlockSpec((tm, D_FF), lambda i: (i, 0)),
            pl.BlockSpec((D_FF, D_MODEL), lambda i: (0, 0)),
        ],
        out_specs=pl.BlockSpec((tm, D_MODEL), lambda i: (i, 0)),
        out_shape=jax.ShapeDtypeStruct((t, D_MODEL), F32),
        compiler_params=_cparams(("parallel",)),
        name="ffn_down",
    )(x2d, act, w)


def _layer_params(p, l):
    w_in = p['w_in'][l]
    a_end, b_end, c_end = COLS_A, COLS_A + COLS_B, COLS_A + COLS_B + COLS_C
    w_r = jnp.concatenate([w_in[:, :a_end], w_in[:, b_end:c_end], w_in[:, c_end:], w_in[:, a_end:b_end],
                           jnp.zeros((D_MODEL, PROJ_W - OFF_B - COLS_B), F32)], axis=1).astype(BF16)
    slot_pad = SLOT_B - QK_B

    def slots(w):
        return jnp.pad(w, [(0, 0)] * (w.ndim - 1) + [(0, slot_pad)]).reshape(*w.shape[:-2], HEADS_B * SLOT_B)

    wq = slots(p['b_w_uq'][l].reshape(Q_RANK, HEADS_B, QK_B)).astype(BF16)
    wkv = p['b_w_ukv'][l].reshape(KV_RANK, HEADS_B, NOPE_B + V_B)
    wk_nope = jnp.pad(wkv[:, :, :NOPE_B], ((0, 0), (0, 0), (0, SLOT_B - NOPE_B))).reshape(KV_RANK, -1)
    place = jnp.pad(jnp.eye(ROPE_B, dtype=F32), ((0, 0), (NOPE_B, SLOT_B - QK_B)))
    wk_pe = jnp.tile(place, (1, HEADS_B))
    wk = jnp.concatenate([wk_nope, wk_pe, jnp.zeros((256 - KV_RANK - ROPE_B, HEADS_B * SLOT_B), F32)], axis=0)
    wv = wkv[:, :, NOPE_B:].reshape(KV_RANK, WIDTH_B)
    gq = jnp.tile(jnp.pad(p['b_q_g'][l], (0, slot_pad)), HEADS_B)[None] * (QK_B ** -0.5 * LOG2E)
    gk = jnp.tile(jnp.pad(p['b_k_g'][l], (0, slot_pad)), HEADS_B)[None]
    w_up = p['w_up'][l]
    cw = p['ffn_conv_w'][l]
    cb = p['ffn_conv_b'][l]

    def interleave(w):
        r = w.shape[0]
        return w.reshape(r, 2, D_FF // LANE, LANE).transpose(0, 2, 1, 3).reshape(r, 2 * D_FF)

    return dict(
        norm_attn_g=p['norm_attn_g'][l][None], w_r=w_r,
        a_qg=[jnp.tile(p['a_q_g'][l, g], 4)[None] * (HEAD_DIM_A ** -0.5 * LOG2E) for g in range(3)],
        a_kg=[jnp.tile(p['a_k_g'][l, g], 4)[None] for g in range(3)],
        gql=p['b_q_lat_g'][l][None], gkl=p['b_kv_lat_g'][l][None],
        wq=wq, wk=wk.astype(BF16), wv=wv.astype(BF16), gq=gq, gk=gk,
        c_conv_w=p['c_conv_w'][l], c_conv_b=p['c_conv_b'][l][None],
        w1t=p['c_w1'][l][0:1].T, w1c=p['c_w1'][l][1:17].T, w1s=-p['c_w1'][l][17:33].T,
        b1=p['c_b1'][l][:, None], w2t=p['c_w2'][l].T, b2=p['c_b2'][l][:, None],
        w3t=p['c_w3'][l].T.reshape(2, WIDTH_C, HYENA_HID), fr=p['c_freq'][l][:, None],
        skip=p['c_skip'][l][None], bg=p['b_gate'][l][None],
        wa=p['w_br_a'][l].astype(BF16), wb=p['w_br_b'][l].astype(BF16), wc=p['w_br_c'][l].astype(BF16),
        wo=p['w_out'][l].astype(BF16), norm_ffn_g=p['norm_ffn_g'][l][None],
        w_up=interleave(w_up).astype(BF16), ffn_cw=interleave(cw), ffn_cb=interleave(cb[None]),
        w_down=p['w_down'][l].astype(BF16),
    )


def _seq_constants(l):
    half = ROPE_B // 2
    pos = jnp.arange(l, dtype=F32)
    inv = ROPE_THETA ** (-jnp.arange(half, dtype=F32) / half)
    ang = pos[:, None] * inv[None, :]
    cos, sin = jnp.cos(ang), jnp.sin(ang)
    one = jnp.ones((l, NOPE_B), F32)
    zn = jnp.zeros((l, NOPE_B), F32)
    zh = jnp.zeros((l, half), F32)
    zp = jnp.zeros((l, SLOT_B - QK_B), F32)
    rc = jnp.concatenate([one, cos, cos, zp], axis=1)
    ra = jnp.concatenate([zn, -sin, zh, zp], axis=1)
    rb = jnp.concatenate([zn, zh, sin, zp], axis=1)
    m = jnp.arange(2 * l)
    p_idx = jnp.where(m < l, m, 2 * l - m)
    tlin = jnp.linspace(0.0, 1.0, l, dtype=F32)
    tt = tlin[jnp.minimum(p_idx, l - 1)]
    angf = (2.0 * math.pi / l) * p_idx.astype(F32)
    valid = (m != l).astype(F32)
    bands = (HYENA_EMB - 1) // 2
    fcol = jnp.linspace(1e-4, bands - 1, bands, dtype=F32)[:, None]
    deltas = jnp.abs(jnp.linspace(math.log(HYENA_TARGET) / HYENA_FAST_DECAY,
                                  math.log(HYENA_TARGET) / HYENA_SLOW_DECAY, WIDTH_C, dtype=F32))[:, None]
    return dict(rc=rc, ra=ra, rb=rb, pos=jnp.stack([tt, angf, valid]), fcol=fcol, deltas=deltas)


def _static_tables():
    hidx = jnp.arange(2 * LANE)
    head_of_row = jnp.where(hidx % LANE < 4, (hidx // LANE) * 4 + hidx % LANE, -1)
    expand = (head_of_row[:, None] == (jnp.arange(WIDTH_A) // HEAD_DIM_A)[None, :]).astype(BF16)
    slopes = jnp.exp2(-ALIBI_MAX * (jnp.arange(HEADS_A, dtype=F32) + 1.0) / HEADS_A)
    return expand, slopes


def _layer(x2d, lp, sc, expand, slopes, b, l):
    proj = _inproj(x2d, lp['norm_attn_g'], lp['w_r'])
    oas, lses = [], []
    for g, dil in enumerate(DILATIONS):
        o, lse = _mixer_a_group(proj, lp['a_qg'][g], lp['a_kg'][g], -slopes * (dil * LOG2E), b, l, g, dil)
        oas.append(o)
        lses.append(lse)
    qp, kp, vt = _mla_prep(proj, lp['gql'], lp['gkl'], lp['wq'], lp['wk'], lp['wv'], lp['gq'], lp['gk'],
                           sc['rc'], sc['ra'], sc['rb'], b, l)
    yb = _mla_attn(qp, kp, vt, b, l)
    u, x0 = _hyena_prep(proj, lp['c_conv_w'], lp['c_conv_b'], b, l)
    kf = _hyena_filter(sc['pos'], sc['fcol'], lp['w1t'], lp['w1c'], lp['w1s'], lp['b1'], lp['w2t'], lp['b2'],
                       lp['w3t'], lp['fr'], sc['deltas'], l)
    nb = l // TOEP
    u_t = jnp.transpose(u.reshape(b, nb, TOEP, WIDTH_C), (3, 1, 0, 2)).reshape(WIDTH_C, nb * b, TOEP)
    cv_t = _hyena_conv(kf, u_t, nb, b)
    cv = jnp.transpose(cv_t.reshape(WIDTH_C, nb, b, TOEP), (2, 1, 3, 0)).reshape(b * l, WIDTH_C)
    xm = _merge(oas, lses, yb, cv, u, x0, proj, x2d, expand, lp['skip'], lp['bg'],
                lp['wa'], lp['wb'], lp['wc'], lp['wo'])
    act = _ffn_up(xm, lp['norm_ffn_g'], lp['w_up'], lp['ffn_cw'], lp['ffn_cb'], l)
    return _ffn_down(xm, act, lp['w_down'])


def _trunk(x, layer_params, expand, slopes):
    b, l, _ = x.shape
    sc = _seq_constants(l)
    y = x.reshape(b * l, D_MODEL)
    for lp in layer_params:
        y = _layer(y, lp, sc, expand, slopes, b, l)
    return y.reshape(b, l, D_MODEL)


def kernel(x_prompt, x_sample, norm_attn_g, w_in, b_gate, a_q_g, a_k_g, b_q_lat_g, b_kv_lat_g, b_w_uq, b_w_ukv, b_q_g, b_k_g, c_conv_w, c_conv_b, c_w1, c_b1, c_w2, c_b2, c_w3, c_freq, c_skip, w_br_a, w_br_b, w_br_c, w_out, norm_ffn_g, w_up, ffn_conv_w, ffn_conv_b, w_down):
    p = dict(norm_attn_g=norm_attn_g, w_in=w_in, b_gate=b_gate, a_q_g=a_q_g, a_k_g=a_k_g, b_q_lat_g=b_q_lat_g,
             b_kv_lat_g=b_kv_lat_g, b_w_uq=b_w_uq, b_w_ukv=b_w_ukv, b_q_g=b_q_g, b_k_g=b_k_g, c_conv_w=c_conv_w,
             c_conv_b=c_conv_b, c_w1=c_w1, c_b1=c_b1, c_w2=c_w2, c_b2=c_b2, c_w3=c_w3, c_freq=c_freq,
             c_skip=c_skip, w_br_a=w_br_a, w_br_b=w_br_b, w_br_c=w_br_c, w_out=w_out, norm_ffn_g=norm_ffn_g,
             w_up=w_up, ffn_conv_w=ffn_conv_w, ffn_conv_b=ffn_conv_b, w_down=w_down)
    layer_params = [_layer_params(p, l) for l in range(DEPTH)]
    expand, slopes = _static_tables()
    return _trunk(x_prompt, layer_params, expand, slopes), _trunk(x_sample, layer_params, expand, slopes)
```

```python
import functools
import math

import jax
import jax.numpy as jnp
from jax import lax
from jax.experimental import pallas as pl
from jax.experimental.pallas import tpu as pltpu

F32 = jnp.float32
BF16 = jnp.bfloat16

D_MODEL = 1024
DEPTH = 2
EPS = 1e-6
HEADS_A = 8
HEAD_DIM_A = 64
DILATIONS = (1, 4, 16)
HALF_A = 64
WIDTH_A = HEADS_A * HEAD_DIM_A
ALIBI_MAX = 8.0
HEADS_B = 8
NOPE_B = 64
ROPE_B = 32
QK_B = NOPE_B + ROPE_B
V_B = 64
Q_RANK = 256
KV_RANK = 128
ROPE_THETA = 10000.0
WIDTH_B = HEADS_B * V_B
WIDTH_C = 512
HYENA_EMB = 33
HYENA_HID = 64
HYENA_FAST_DECAY = 0.3
HYENA_SLOW_DECAY = 1.5
HYENA_TARGET = 1e-2
D_FF = 2816
COLS_A = 3 * 3 * WIDTH_A
COLS_B = Q_RANK + KV_RANK + ROPE_B
COLS_C = 3 * WIDTH_C
COLS_G = 3 * D_MODEL
OFF_A = 0
OFF_C = COLS_A
OFF_G = OFF_C + COLS_C
OFF_B = OFF_G + COLS_G
PROJ_W = OFF_B + 512
LANE = 128
SLOT_B = 128
VROWS = 80
TOEP = 256
LOG2E = math.log2(math.e)
LN2 = math.log(2.0)
MASKED = 1e32
VMEM_LIMIT = 48 * 1024 * 1024


def _cparams(sem):
    return pltpu.CompilerParams(dimension_semantics=sem, vmem_limit_bytes=VMEM_LIMIT)


def _nt_dot(a, b):
    return lax.dot_general(a, b, (((1,), (1,)), ((), ())), preferred_element_type=F32)


def _dot(a, b):
    return jnp.dot(a, b, preferred_element_type=F32)


def _dot_exact(a, b):
    return jnp.dot(a, b, preferred_element_type=F32, precision=lax.Precision.HIGHEST)


def _inproj_kernel(x_ref, g_ref, w_ref, o_ref, h_ref):
    @pl.when(pl.program_id(1) == 0)
    def _():
        x = x_ref[...]
        ms = jnp.mean(x * x, axis=-1, keepdims=True)
        h_ref[...] = (x * lax.rsqrt(ms + EPS) * g_ref[...]).astype(BF16)

    o_ref[...] = _dot(h_ref[...], w_ref[...]).astype(o_ref.dtype)


def _inproj(x2d, gain, w_r):
    t = x2d.shape[0]
    tm, tn = 1024, PROJ_W // 4
    return pl.pallas_call(
        _inproj_kernel,
        grid=(t // tm, PROJ_W // tn),
        in_specs=[
            pl.BlockSpec((tm, D_MODEL), lambda i, j: (i, 0)),
            pl.BlockSpec((1, D_MODEL), lambda i, j: (0, 0)),
            pl.BlockSpec((D_MODEL, tn), lambda i, j: (0, j)),
        ],
        out_specs=pl.BlockSpec((tm, tn), lambda i, j: (i, j)),
        out_shape=jax.ShapeDtypeStruct((t, PROJ_W), BF16),
        scratch_shapes=[pltpu.VMEM((tm, D_MODEL), BF16)],
        compiler_params=_cparams(("parallel", "arbitrary")),
        name="inproj",
    )(x2d, gain, w_r)


def _dil_attn_kernel(negc_ref, q_ref, k_ref, v_ref, qg_ref, kg_ref, o_ref, lse_ref,
                     qn_ref, kn_ref, vn_ref, stg_ref, ost_ref, bias_ref, s_ref, p_ref, *, l, dil):
    hb = pl.program_id(1)
    lu = l // dil
    cw = q_ref.shape[-1]
    win = min(2 * LANE, lu)
    nq = lu // LANE
    row = lax.broadcasted_iota(jnp.int32, (cw, cw), 0) // HEAD_DIM_A
    col = lax.broadcasted_iota(jnp.int32, (cw, cw), 1) // HEAD_DIM_A
    seg_ones = jnp.where(row == col, 1.0, 0.0).astype(BF16)

    chunk = 512
    per = chunk // dil

    def norm_body(c, carry):
        r0 = pl.multiple_of(c * chunk, chunk)
        for src, gain, dst in ((q_ref, qg_ref, qn_ref), (k_ref, kg_ref, kn_ref), (v_ref, None, vn_ref)):
            if gain is None and dil == 1:
                continue
            x = src[0, pl.ds(r0, chunk), :].astype(F32)
            if gain is not None:
                ssq = _dot((x * x).astype(BF16), seg_ones)
                x = x * lax.rsqrt(ssq * (1.0 / HEAD_DIM_A) + EPS) * gain[...]
            if dil == 1:
                dst[pl.ds(r0, chunk), :] = x.astype(BF16)
            else:
                for h in range(cw // LANE):
                    stg_ref[h] = x[:, h * LANE:(h + 1) * LANE]
                for r in range(dil):
                    d0 = pl.multiple_of(r * lu + c * per, per)
                    for h in range(cw // LANE):
                        dst[pl.ds(d0, per), h * LANE:(h + 1) * LANE] = (
                            stg_ref[h, pl.ds(r, per, stride=dil), :].astype(BF16))
        return carry

    lax.fori_loop(0, l // chunk, norm_body, 0)

    lane_lo = lax.broadcasted_iota(jnp.int32, (1, LANE), 1) < HEAD_DIM_A
    npair = cw // LANE
    tiles_per_step = 2
    j_k = lax.broadcasted_iota(jnp.int32, (win, LANE), 0)
    i_q = lax.broadcasted_iota(jnp.int32, (win, LANE), 1)
    for variant in range(3):
        absrel = jnp.abs(j_k - i_q - variant * HALF_A).astype(F32)
        absrel = jnp.where(absrel <= float(HALF_A), absrel, MASKED)
        for pair in range(npair):
            bias_ref[variant * npair + pair] = jnp.concatenate(
                [absrel * negc_ref[4 * hb + 2 * pair + a] for a in range(2)], axis=1)
    row8 = lax.broadcasted_iota(jnp.int32, (8, LANE), 0)

    nsteps = l // (LANE * tiles_per_step)

    def coords(idx):
        r = idx // nq
        q0 = pl.multiple_of((idx % nq) * LANE, LANE)
        w0 = pl.multiple_of(jnp.clip(q0 - HALF_A, 0, lu - win), HALF_A)
        return r, q0, w0, pl.multiple_of(r * lu, LANE)

    def issue_scores(step, slot):
        for u in range(tiles_per_step):
            r, q0, w0, base = coords(step * tiles_per_step + u)
            q = qn_ref[pl.ds(base + q0, LANE), :]
            kw = kn_ref[pl.ds(base + w0, win), :]
            for pair in range(npair):
                qp = q[:, pair * LANE:(pair + 1) * LANE]
                zero = jnp.zeros_like(qp)
                qs = jnp.concatenate([jnp.where(lane_lo, qp, zero), jnp.where(lane_lo, zero, qp)], axis=0)
                s_ref[slot, u * npair + pair] = _nt_dot(kw[:, pair * LANE:(pair + 1) * LANE], qs)

    def softmax(step, slot):
        for u in range(tiles_per_step):
            idx = step * tiles_per_step + u
            r, q0, w0, base = coords(idx)
            variant = (q0 - w0) // HALF_A
            lse_tile = jnp.zeros((8, LANE), F32)
            for pair in range(npair):
                st = s_ref[slot, u * npair + pair] + bias_ref[variant * npair + pair]
                m = jnp.max(st, axis=0, keepdims=True)
                p = jnp.exp2(st - m)
                den = jnp.sum(p, axis=0, keepdims=True)
                p_ref[slot, u * npair + pair] = (p * (1.0 / den)).astype(BF16)
                lse = (m + jnp.log2(den)) * LN2
                for a in range(2):
                    lse_tile = jnp.where(row8 == 2 * pair + a, lse[:, a * LANE:(a + 1) * LANE], lse_tile)
            lse_ref[0, 0, idx] = lse_tile

    def issue_values(step, slot):
        for u in range(tiles_per_step):
            r, q0, w0, base = coords(step * tiles_per_step + u)
            vw = v_ref[0, pl.ds(w0, win), :] if dil == 1 else vn_ref[pl.ds(base + w0, win), :]
            outs = []
            for pair in range(npair):
                vt = vw[:, pair * LANE:(pair + 1) * LANE].astype(F32).T.astype(BF16)
                res = _dot(vt, p_ref[slot, u * npair + pair])
                o_t = jnp.concatenate([res[:HEAD_DIM_A, :LANE], res[HEAD_DIM_A:, LANE:]], axis=0)
                outs.append(o_t.T)
            if dil == 1:
                o_ref[0, pl.ds(q0, LANE), :] = jnp.concatenate(outs, axis=1).astype(o_ref.dtype)
            else:
                for h, o_pair in enumerate(outs):
                    ost_ref[h, pl.ds(r + q0 * dil, LANE, stride=dil), :] = o_pair

    p_ref[...] = jnp.zeros_like(p_ref)
    issue_scores(jnp.int32(0), 0)

    def step_body(j, carry):
        k = 2 * j
        issue_scores(k + 1, 1)
        issue_values(jnp.maximum(k - 1, 0), 1)
        softmax(k, 0)
        issue_scores(jnp.minimum(k + 2, nsteps - 1), 0)
        issue_values(k, 0)
        softmax(k + 1, 1)
        return carry

    lax.fori_loop(0, nsteps // 2, step_body, 0)
    issue_values(jnp.int32(nsteps - 1), 1)

    if dil != 1:
        def out_body(c, carry):
            r0 = pl.multiple_of(c * chunk, chunk)
            for h in range(cw // LANE):
                o_ref[0, pl.ds(r0, chunk), h * LANE:(h + 1) * LANE] = (
                    ost_ref[h, pl.ds(r0, chunk), :].astype(o_ref.dtype))
            return carry

        lax.fori_loop(0, l // chunk, out_body, 0)


def _mixer_a_group(proj, qg, kg, negc, b, l, g, dil):
    pv = proj.reshape(b, l, PROJ_W)
    cq, ck, cv = (OFF_A + (0 + g) * 512) // 256, (OFF_A + (3 + g) * 512) // 256, (OFF_A + (6 + g) * 512) // 256

    def in_spec(c0):
        return pl.BlockSpec((1, l, 256), lambda bi, hb: (bi, 0, c0 + hb))

    stage_rows = (8, 8) if dil == 1 else (512, l)
    nt = l // LANE
    win = min(2 * LANE, l // dil)
    o, lse = pl.pallas_call(
        functools.partial(_dil_attn_kernel, l=l, dil=dil),
        grid=(b, 2),
        in_specs=[
            pl.BlockSpec(memory_space=pltpu.SMEM),
            in_spec(cq), in_spec(ck), in_spec(cv),
            pl.BlockSpec((1, 256), lambda bi, hb: (0, 0)),
            pl.BlockSpec((1, 256), lambda bi, hb: (0, 0)),
        ],
        out_specs=[
            pl.BlockSpec((1, l, 256), lambda bi, hb: (bi, 0, hb)),
            pl.BlockSpec((1, 1, nt, 8, LANE), lambda bi, hb: (bi, hb, 0, 0, 0)),
        ],
        out_shape=[
            jax.ShapeDtypeStruct((b, l, WIDTH_A), BF16),
            jax.ShapeDtypeStruct((b, 2, nt, 8, LANE), F32),
        ],
        scratch_shapes=[pltpu.VMEM((l, 256), BF16), pltpu.VMEM((l, 256), BF16),
                        pltpu.VMEM((l if dil != 1 else 16, 256), BF16),
                        pltpu.VMEM((2, stage_rows[0], LANE), F32), pltpu.VMEM((2, stage_rows[1], LANE), F32),
                        pltpu.VMEM((6, win, 2 * LANE), F32),
                        pltpu.VMEM((2, 4, win, 2 * LANE), F32), pltpu.VMEM((2, 4, win, 2 * LANE), BF16)],
        compiler_params=_cparams(("parallel", "arbitrary")),
        name=f"dilated_attention_g{g}",
    )(negc, pv, pv, pv, qg, kg)
    lse = lse[:, :, :, :4, :].reshape(b, 2, dil, nt // dil, 4, LANE)
    lse = lse.transpose(0, 1, 4, 3, 5, 2).reshape(b, HEADS_A, l)
    return o.reshape(b * l, WIDTH_A), lse


def _mla_prep_kernel(pb_ref, gql_ref, gkl_ref, wq_ref, wk_ref, wv_ref, gq_ref, gk_ref, rc_ref, ra_ref, rb_ref,
                     sel_ref, exp_ref, qt_ref, k_ref, vt_ref):
    c = pb_ref[...].astype(F32)
    cq = c[:, :Q_RANK]
    cqn = (cq * lax.rsqrt(jnp.mean(cq * cq, axis=-1, keepdims=True) + EPS) * gql_ref[...]).astype(BF16)
    ckv = c[:, Q_RANK:Q_RANK + KV_RANK]
    ckvn = (ckv * lax.rsqrt(jnp.mean(ckv * ckv, axis=-1, keepdims=True) + EPS) * gkl_ref[...]).astype(BF16)
    q = _dot(cqn, wq_ref[...])
    k = _dot(jnp.concatenate([ckvn, pb_ref[:, Q_RANK + KV_RANK:]], axis=1), wk_ref[...])
    v = _dot(ckvn, wv_ref[...])
    rc, ra, rb = rc_ref[...], ra_ref[...], rb_ref[...]

    lane = lax.broadcasted_iota(jnp.int32, (1, LANE), 1)
    half = ROPE_B // 2

    def finish(x):
        ssq = _dot((x * x).astype(BF16), sel_ref[...])
        r = lax.rsqrt(ssq * (1.0 / QK_B) + EPS)
        r_hi = r.astype(BF16)
        r_lo = (r - r_hi.astype(F32)).astype(BF16)
        scale = _dot(jnp.where(lane < HEADS_B, r_hi, r_lo), exp_ref[...])
        return x * scale

    def rope(xn):
        return xn * rc + pltpu.roll(xn, SLOT_B - half, 1) * ra + pltpu.roll(xn, half, 1) * rb

    qn = finish(q) * gq_ref[...]
    kn = finish(k) * gk_ref[...]
    for h in range(HEADS_B):
        sl = slice(h * SLOT_B, (h + 1) * SLOT_B)
        qt_ref[0, sl, :] = rope(qn[:, sl]).T.astype(qt_ref.dtype)
        k_ref[:, sl] = rope(kn[:, sl]).astype(k_ref.dtype)
    vt_ref[0] = v.T.astype(vt_ref.dtype)


def _mla_prep(proj, gql, gkl, wq, wk, wv, gq, gk, rc, ra, rb, b, l):
    t = b * l
    tm = 512
    per_seq = l // tm
    const = lambda i: (0, 0)
    slot_of_lane = jnp.arange(HEADS_B * SLOT_B) // SLOT_B
    col = jnp.arange(LANE)
    sel = ((col[None, :] % HEADS_B == slot_of_lane[:, None]) & (col[None, :] < 2 * HEADS_B)).astype(BF16)
    spread = sel.T
    return pl.pallas_call(
        _mla_prep_kernel,
        grid=(t // tm,),
        in_specs=[
            pl.BlockSpec((tm, 512), lambda i: (i, OFF_B // 512)),
            pl.BlockSpec((1, Q_RANK), const), pl.BlockSpec((1, KV_RANK), const),
            pl.BlockSpec((Q_RANK, HEADS_B * SLOT_B), const),
            pl.BlockSpec((256, HEADS_B * SLOT_B), const),
            pl.BlockSpec((KV_RANK, WIDTH_B), const),
            pl.BlockSpec((1, HEADS_B * SLOT_B), const), pl.BlockSpec((1, HEADS_B * SLOT_B), const),
            pl.BlockSpec((tm, SLOT_B), lambda i: (i % per_seq, 0)),
            pl.BlockSpec((tm, SLOT_B), lambda i: (i % per_seq, 0)),
            pl.BlockSpec((tm, SLOT_B), lambda i: (i % per_seq, 0)),
            pl.BlockSpec((HEADS_B * SLOT_B, LANE), const), pl.BlockSpec((LANE, HEADS_B * SLOT_B), const),
        ],
        out_specs=[
            pl.BlockSpec((1, HEADS_B * SLOT_B, tm), lambda i: (i // per_seq, 0, i % per_seq)),
            pl.BlockSpec((tm, HEADS_B * SLOT_B), lambda i: (i, 0)),
            pl.BlockSpec((1, WIDTH_B, tm), lambda i: (i // per_seq, 0, i % per_seq)),
        ],
        out_shape=[
            jax.ShapeDtypeStruct((b, HEADS_B * SLOT_B, l), BF16),
            jax.ShapeDtypeStruct((t, HEADS_B * SLOT_B), BF16),
            jax.ShapeDtypeStruct((b, WIDTH_B, l), BF16),
        ],
        compiler_params=_cparams(("parallel",)),
        name="latent_prep",
    )(proj, gql, gkl, wq, wk, wv, gq, gk, rc, ra, rb, sel, spread)


def _mla_attn_kernel(qt_ref, k_ref, vt_ref, o_ref, vaug_ref, *, tk):
    nkv = vaug_ref.shape[1]
    tq = qt_ref.shape[2]

    @pl.when(pl.program_id(2) == 0)
    def _():
        ones_rows = jnp.where(lax.broadcasted_iota(jnp.int32, (VROWS - V_B, tk), 0) == 0, 1.0, 0.0).astype(BF16)
        for a in range(2):
            for j in range(nkv):
                vaug_ref[a, j, 0:V_B, :] = vt_ref[0, a * V_B:(a + 1) * V_B, j * tk:(j + 1) * tk]
                vaug_ref[a, j, V_B:VROWS, :] = ones_rows

    qw = 2 * LANE
    ntile = tq // qw

    def scores(c, j):
        return [_dot(k_ref[0, j * tk:(j + 1) * tk, a * SLOT_B:(a + 1) * SLOT_B],
                     qt_ref[0, a * SLOT_B:(a + 1) * SLOT_B, c * qw:(c + 1) * qw]) for a in range(2)]

    def values(j, ps):
        return [_dot(vaug_ref[a, j], ps[a]) for a in range(2)]

    st = scores(0, 0)
    for c in range(ntile):
        m = [jnp.full((1, qw), -jnp.inf, F32) for _ in range(2)]
        acc = [jnp.zeros((VROWS, qw), F32) for _ in range(2)]
        ps = None
        for j in range(nkv):
            if j + 1 < nkv:
                st_next = scores(c, j + 1)
            else:
                st_next = scores(c + 1, 0) if c + 1 < ntile else None
            pv = values(j - 1, ps) if j > 0 else None
            ps = []
            for a in range(2):
                mn = jnp.maximum(m[a], jnp.max(st[a], axis=0, keepdims=True))
                ps.append(jnp.exp2(st[a] - mn).astype(BF16))
                if pv is not None:
                    acc[a] = (acc[a] + pv[a]) * jnp.exp2(m[a] - mn)
                m[a] = mn
            st = st_next
        pv = values(nkv - 1, ps)
        outs = []
        for a in range(2):
            tot = acc[a] + pv[a]
            outs.append(tot[0:V_B] * (1.0 / tot[V_B:V_B + 1]))
        o_ref[0, c * qw:(c + 1) * qw, :] = jnp.concatenate(outs, axis=0).T.astype(o_ref.dtype)


def _mla_attn(qt, kp, vt, b, l):
    tq, tk = 1024, 256
    k3 = kp.reshape(b, l, HEADS_B * SLOT_B)
    out = pl.pallas_call(
        functools.partial(_mla_attn_kernel, tk=tk),
        grid=(b, HEADS_B // 2, l // tq),
        in_specs=[
            pl.BlockSpec((1, 2 * SLOT_B, tq), lambda bi, p, qi: (bi, p, qi)),
            pl.BlockSpec((1, l, 2 * SLOT_B), lambda bi, p, qi: (bi, 0, p)),
            pl.BlockSpec((1, 2 * V_B, l), lambda bi, p, qi: (bi, p, 0)),
        ],
        out_specs=pl.BlockSpec((1, tq, 2 * V_B), lambda bi, p, qi: (bi, qi, p)),
        out_shape=jax.ShapeDtypeStruct((b, l, WIDTH_B), BF16),
        scratch_shapes=[pltpu.VMEM((2, l // tk, VROWS, tk), BF16)],
        compiler_params=_cparams(("parallel", "parallel", "arbitrary")),
        name="latent_attention",
    )(qt, k3, vt)
    return out.reshape(b * l, WIDTH_B)


def _shift_rows(x, prev_row, next_row):
    n = x.shape[0]
    rid = lax.broadcasted_iota(jnp.int32, x.shape, 0)
    xm = jnp.where(rid == 0, prev_row, pltpu.roll(x, 1, 0))
    xp = jnp.where(rid == n - 1, next_row, pltpu.roll(x, n - 1, 0))
    return xm, xp


def _hyena_prep_kernel(pc_ref, prev_ref, next_ref, w_ref, b_ref, u_ref, x0_ref, *, per_seq):
    i = pl.program_id(0) % per_seq
    x = pc_ref[...].astype(F32)
    hr = prev_ref.shape[0]
    prev_row = jnp.where(i == 0, 0.0, prev_ref[...].astype(F32)[hr - 1:hr, :])
    next_row = jnp.where(i == per_seq - 1, 0.0, next_ref[...].astype(F32)[0:1, :])
    xm, xp = _shift_rows(x, prev_row, next_row)
    y = xm * w_ref[0:1, :] + x * w_ref[1:2, :] + xp * w_ref[2:3, :] + b_ref[...]
    x0_ref[...] = y[:, :WIDTH_C].astype(x0_ref.dtype)
    u_ref[...] = (y[:, 2 * WIDTH_C:] * y[:, WIDTH_C:2 * WIDTH_C]).astype(u_ref.dtype)


def _hyena_prep(proj, conv_w, conv_b, b, l):
    t = b * l
    tm, hr = 512, 16
    per_seq = l // tm
    nh = tm // hr
    cb = OFF_C // COLS_C
    return pl.pallas_call(
        functools.partial(_hyena_prep_kernel, per_seq=per_seq),
        grid=(t // tm,),
        in_specs=[
            pl.BlockSpec((tm, COLS_C), lambda i: (i, cb)),
            pl.BlockSpec((hr, COLS_C), lambda i: (jnp.maximum(i * nh - 1, 0), cb)),
            pl.BlockSpec((hr, COLS_C), lambda i: (jnp.minimum((i + 1) * nh, t // hr - 1), cb)),
            pl.BlockSpec((3, COLS_C), lambda i: (0, 0)),
            pl.BlockSpec((1, COLS_C), lambda i: (0, 0)),
        ],
        out_specs=[pl.BlockSpec((tm, WIDTH_C), lambda i: (i, 0)), pl.BlockSpec((tm, WIDTH_C), lambda i: (i, 0))],
        out_shape=[jax.ShapeDtypeStruct((t, WIDTH_C), BF16), jax.ShapeDtypeStruct((t, WIDTH_C), BF16)],
        compiler_params=_cparams(("parallel",)),
        name="hyena_prep",
    )(proj, proj, proj, conv_w, conv_b)


def _hyena_filter_kernel(pos_ref, fcol_ref, w1t_ref, w1c_ref, w1s_ref, b1_ref, w2_ref, b2_ref, w3_ref, fr_ref,
                         dl_ref, o_ref):
    tt = pos_ref[0:1, :]
    ang = pos_ref[1:2, :]
    valid = pos_ref[2:3, :]
    arg = fcol_ref[...] * ang
    pre1 = w1t_ref[...] * tt + _dot_exact(w1c_ref[...], jnp.cos(arg)) + _dot_exact(w1s_ref[...], jnp.sin(arg))
    fr = fr_ref[...]
    hid = jnp.sin(fr * (pre1 + b1_ref[...]))
    hid = jnp.sin(fr * (_dot_exact(w2_ref[...], hid) + b2_ref[...]))
    filt = _dot_exact(w3_ref[0], hid)
    o_ref[...] = filt * jnp.exp(-dl_ref[...] * tt) * valid


def _hyena_filter(pos, fcol, w1t, w1c, w1s, b1, w2t, b2, w3t, fr, dl, l):
    n = 2048
    const = lambda j: (0, 0)
    return pl.pallas_call(
        _hyena_filter_kernel,
        grid=(2 * l // n,),
        in_specs=[
            pl.BlockSpec((3, n), lambda j: (0, j)),
            pl.BlockSpec((16, 1), const),
            pl.BlockSpec((HYENA_HID, 1), const), pl.BlockSpec((HYENA_HID, 16), const),
            pl.BlockSpec((HYENA_HID, 16), const), pl.BlockSpec((HYENA_HID, 1), const),
            pl.BlockSpec((HYENA_HID, HYENA_HID), const), pl.BlockSpec((HYENA_HID, 1), const),
            pl.BlockSpec((1, WIDTH_C, HYENA_HID), lambda j: (j // (l // n), 0, 0)),
            pl.BlockSpec((HYENA_HID, 1), const), pl.BlockSpec((WIDTH_C, 1), const),
        ],
        out_specs=pl.BlockSpec((WIDTH_C, n), lambda j: (0, j)),
        out_shape=jax.ShapeDtypeStruct((WIDTH_C, 2 * l), F32),
        compiler_params=_cparams(("parallel",)),
        name="hyena_filter",
    )(pos, fcol, w1t, w1c, w1s, b1, w2t, b2, w3t, fr, dl)


def _hyena_conv_kernel(kf_ref, u_ref, o_ref, uf_ref, acc_ref, *, nb, bsz):
    p = TOEP
    n2 = 2 * nb * p

    def channel(c, carry):
        krow = kf_ref[pl.ds(c, 1), :]
        uf_ref[...] = u_ref[c].astype(F32)
        acc_ref[...] = jnp.zeros_like(acc_ref)
        for d in range(-(nb - 1), nb):
            a0 = (d * p) % n2
            b0 = ((d - 1) * p) % n2
            seg = jnp.concatenate([krow[:, a0:a0 + p], krow[:, b0:b0 + p]], axis=1)
            rolled = pltpu.roll(jnp.broadcast_to(seg, (p, 2 * p)), 0, 1, stride=1, stride_axis=0)
            toep = rolled[:, :p].astype(BF16)
            rows = (nb - abs(d)) * bsz
            src = 0 if d >= 0 else -d * bsz
            dst = d * bsz if d >= 0 else 0
            acc_ref[dst:dst + rows, :] += _dot(uf_ref[src:src + rows, :].astype(BF16), toep)
        o_ref[c] = acc_ref[...].astype(o_ref.dtype)
        return carry

    lax.fori_loop(0, kf_ref.shape[0], channel, 0)


def _hyena_conv(kf, u_t, nb, bsz):
    cblk = 8
    rows = nb * bsz
    return pl.pallas_call(
        functools.partial(_hyena_conv_kernel, nb=nb, bsz=bsz),
        grid=(WIDTH_C // cblk,),
        in_specs=[
            pl.BlockSpec((cblk, 2 * nb * TOEP), lambda i: (i, 0)),
            pl.BlockSpec((cblk, rows, TOEP), lambda i: (i, 0, 0)),
        ],
        out_specs=pl.BlockSpec((cblk, rows, TOEP), lambda i: (i, 0, 0)),
        out_shape=jax.ShapeDtypeStruct((WIDTH_C, rows, TOEP), BF16),
        scratch_shapes=[pltpu.VMEM((rows, TOEP), F32), pltpu.VMEM((rows, TOEP), F32)],
        compiler_params=_cparams(("parallel",)),
        name="hyena_conv",
    )(kf, u_t)


def _merge_kernel(oa0_ref, oa1_ref, oa2_ref, l0_ref, l1_ref, l2_ref, yb_ref, cv_ref, u_ref, x0_ref, pg_ref, x_ref,
                  ex_ref, skip_ref, bg_ref, wa_ref, wb_ref, wc_ref, wo_ref, o_ref):
    lses = [l0_ref[0], l1_ref[0], l2_ref[0]]
    mx = jnp.maximum(jnp.maximum(lses[0], lses[1]), lses[2])
    pad = jnp.zeros((LANE - HEADS_A, mx.shape[1]), F32)
    num = None
    den = None
    for lse, oa in zip(lses, (oa0_ref, oa1_ref, oa2_ref)):
        wt = jnp.concatenate([jnp.exp(lse - mx), pad], axis=0).T
        w = _dot(wt.astype(BF16), ex_ref[...])
        num = w * oa[...].astype(F32) if num is None else num + w * oa[...].astype(F32)
        den = w if den is None else den + w
    ya = (num / den).astype(BF16)
    u = u_ref[...].astype(F32)
    yc = (x0_ref[...].astype(F32) * (cv_ref[...].astype(F32) + skip_ref[...] * u)).astype(BF16)
    mixed = None
    for i, (y, w_ref) in enumerate(((ya, wa_ref), (yb_ref[...], wb_ref), (yc, wc_ref))):
        sl = slice(i * D_MODEL, (i + 1) * D_MODEL)
        gate = jax.nn.sigmoid(pg_ref[:, sl].astype(F32) + bg_ref[:, sl])
        term = gate * _dot(y, w_ref[...])
        mixed = term if mixed is None else mixed + term
    o_ref[...] = x_ref[...] + _dot(mixed.astype(BF16), wo_ref[...])


def _merge(oas, lses, yb, cv, u, x0, proj, x2d, expand, skip, bg, wa, wb, wc, wo, l):
    t = x2d.shape[0]
    tm = 512
    per_seq = l // tm
    row = lambda i: (i, 0)
    const = lambda i: (0, 0)
    half = pl.BlockSpec((tm, 512), row)
    lse_spec = pl.BlockSpec((1, HEADS_A, tm), lambda i: (i // per_seq, 0, i % per_seq))
    return pl.pallas_call(
        _merge_kernel,
        grid=(t // tm,),
        in_specs=[
            half, half, half,
            lse_spec, lse_spec, lse_spec,
            half, half, half, half,
            pl.BlockSpec((tm, COLS_G), lambda i: (i, OFF_G // COLS_G)),
            pl.BlockSpec((tm, D_MODEL), row),
            pl.BlockSpec((LANE, WIDTH_A), const),
            pl.BlockSpec((1, WIDTH_C), const), pl.BlockSpec((1, COLS_G), const),
            pl.BlockSpec((WIDTH_A, D_MODEL), const), pl.BlockSpec((WIDTH_B, D_MODEL), const),
            pl.BlockSpec((WIDTH_C, D_MODEL), const), pl.BlockSpec((D_MODEL, D_MODEL), const),
        ],
        out_specs=pl.BlockSpec((tm, D_MODEL), row),
        out_shape=jax.ShapeDtypeStruct((t, D_MODEL), F32),
        compiler_params=_cparams(("parallel",)),
        name="branch_merge",
    )(*oas, *lses, yb, cv, u, x0, proj, x2d, expand, skip, bg, wa, wb, wc, wo)


def _ffn_up_kernel(x_ref, prev_ref, next_ref, g_ref, wa_ref, wg_ref, cwa_ref, cwg_ref, cba_ref, cbg_ref, o_ref,
                   h_ref, *, per_seq, halo):
    tm = x_ref.shape[0]

    @pl.when(pl.program_id(1) == 0)
    def _():
        i = pl.program_id(0) % per_seq

        def norm(x):
            return x * lax.rsqrt(jnp.mean(x * x, axis=-1, keepdims=True) + EPS) * g_ref[...]

        h_ref[0:halo, :] = jnp.where(i == 0, 0.0, norm(prev_ref[...])).astype(BF16)
        h_ref[halo:halo + tm, :] = norm(x_ref[...]).astype(BF16)
        h_ref[halo + tm:, :] = jnp.where(i == per_seq - 1, 0.0, norm(next_ref[...])).astype(BF16)

    h = h_ref[...]
    n = h.shape[0]

    def conv(w_ref, cw_ref, cb_ref):
        y = _dot(h, w_ref[...])
        ym = pltpu.roll(y, 1, 0)[halo:halo + tm]
        yp = pltpu.roll(y, n - 1, 0)[halo:halo + tm]
        return ym * cw_ref[0:1, :] + y[halo:halo + tm] * cw_ref[1:2, :] + yp * cw_ref[2:3, :] + cb_ref[...]

    a = conv(wa_ref, cwa_ref, cba_ref)
    g = conv(wg_ref, cwg_ref, cbg_ref)
    o_ref[...] = (jax.nn.gelu(a) * g).astype(o_ref.dtype)


def _ffn_up(x2d, gain, wa, wg, cwa, cwg, cba, cbg, l):
    t = x2d.shape[0]
    tm, tn, halo = 512, D_FF // 2, 16
    per_seq = l // tm
    nh = tm // halo
    const = lambda i, j: (0, 0)
    col = lambda i, j: (0, j)
    return pl.pallas_call(
        functools.partial(_ffn_up_kernel, per_seq=per_seq, halo=halo),
        grid=(t // tm, D_FF // tn),
        in_specs=[
            pl.BlockSpec((tm, D_MODEL), lambda i, j: (i, 0)),
            pl.BlockSpec((halo, D_MODEL), lambda i, j: (jnp.maximum(i * nh - 1, 0), 0)),
            pl.BlockSpec((halo, D_MODEL), lambda i, j: (jnp.minimum((i + 1) * nh, t // halo - 1), 0)),
            pl.BlockSpec((1, D_MODEL), const),
            pl.BlockSpec((D_MODEL, tn), col), pl.BlockSpec((D_MODEL, tn), col),
            pl.BlockSpec((3, tn), col), pl.BlockSpec((3, tn), col),
            pl.BlockSpec((1, tn), col), pl.BlockSpec((1, tn), col),
        ],
        out_specs=pl.BlockSpec((tm, tn), lambda i, j: (i, j)),
        out_shape=jax.ShapeDtypeStruct((t, D_FF), BF16),
        scratch_shapes=[pltpu.VMEM((tm + 2 * halo, D_MODEL), BF16)],
        compiler_params=_cparams(("parallel", "arbitrary")),
        name="ffn_up",
    )(x2d, x2d, x2d, gain, wa, wg, cwa, cwg, cba, cbg)


def _ffn_down_kernel(x_ref, a_ref, w_ref, o_ref):
    o_ref[...] = x_ref[...] + _dot(a_ref[...], w_ref[...])


def _ffn_down(x2d, act, w):
    t = x2d.shape[0]
    tm = 512
    return pl.pallas_call(
        _ffn_down_kernel,
        grid=(t // tm,),
        in_specs=[
            pl.BlockSpec((tm, D_MODEL), lambda i: (i, 0)),
            pl.BlockSpec((tm, D_FF), lambda i: (i, 0)),
            pl.BlockSpec((D_FF, D_MODEL), lambda i: (0, 0)),
        ],
        out_specs=pl.BlockSpec((tm, D_MODEL), lambda i: (i, 0)),
        out_shape=jax.ShapeDtypeStruct((t, D_MODEL), F32),
        compiler_params=_cparams(("parallel",)),
        name="ffn_down",
    )(x2d, act, w)


def _layer_params(p, l):
    w_in = p['w_in'][l]
    a_end, b_end, c_end = COLS_A, COLS_A + COLS_B, COLS_A + COLS_B + COLS_C
    w_r = jnp.concatenate([w_in[:, :a_end], w_in[:, b_end:c_end], w_in[:, c_end:], w_in[:, a_end:b_end],
                           jnp.zeros((D_MODEL, PROJ_W - OFF_B - COLS_B), F32)], axis=1).astype(BF16)
    slot_pad = SLOT_B - QK_B

    def slots(w):
        return jnp.pad(w, [(0, 0)] * (w.ndim - 1) + [(0, slot_pad)]).reshape(*w.shape[:-2], HEADS_B * SLOT_B)

    wq = slots(p['b_w_uq'][l].reshape(Q_RANK, HEADS_B, QK_B)).astype(BF16)
    wkv = p['b_w_ukv'][l].reshape(KV_RANK, HEADS_B, NOPE_B + V_B)
    wk_nope = jnp.pad(wkv[:, :, :NOPE_B], ((0, 0), (0, 0), (0, SLOT_B - NOPE_B))).reshape(KV_RANK, -1)
    place = jnp.pad(jnp.eye(ROPE_B, dtype=F32), ((0, 0), (NOPE_B, SLOT_B - QK_B)))
    wk_pe = jnp.tile(place, (1, HEADS_B))
    wk = jnp.concatenate([wk_nope, wk_pe, jnp.zeros((256 - KV_RANK - ROPE_B, HEADS_B * SLOT_B), F32)], axis=0)
    wv = wkv[:, :, NOPE_B:].reshape(KV_RANK, WIDTH_B)
    gq = jnp.tile(jnp.pad(p['b_q_g'][l], (0, slot_pad)), HEADS_B)[None] * (QK_B ** -0.5 * LOG2E)
    gk = jnp.tile(jnp.pad(p['b_k_g'][l], (0, slot_pad)), HEADS_B)[None]
    w_up = p['w_up'][l]
    cw = p['ffn_conv_w'][l]
    cb = p['ffn_conv_b'][l]
    return dict(
        norm_attn_g=p['norm_attn_g'][l][None], w_r=w_r,
        a_qg=[jnp.tile(p['a_q_g'][l, g], 4)[None] * (HEAD_DIM_A ** -0.5 * LOG2E) for g in range(3)],
        a_kg=[jnp.tile(p['a_k_g'][l, g], 4)[None] for g in range(3)],
        gql=p['b_q_lat_g'][l][None], gkl=p['b_kv_lat_g'][l][None],
        wq=wq, wk=wk.astype(BF16), wv=wv.astype(BF16), gq=gq, gk=gk,
        c_conv_w=p['c_conv_w'][l], c_conv_b=p['c_conv_b'][l][None],
        w1t=p['c_w1'][l][0:1].T, w1c=p['c_w1'][l][1:17].T, w1s=-p['c_w1'][l][17:33].T,
        b1=p['c_b1'][l][:, None], w2t=p['c_w2'][l].T, b2=p['c_b2'][l][:, None],
        w3t=p['c_w3'][l].T.reshape(2, WIDTH_C, HYENA_HID), fr=p['c_freq'][l][:, None],
        skip=p['c_skip'][l][None], bg=p['b_gate'][l][None],
        wa=p['w_br_a'][l].astype(BF16), wb=p['w_br_b'][l].astype(BF16), wc=p['w_br_c'][l].astype(BF16),
        wo=p['w_out'][l].astype(BF16), norm_ffn_g=p['norm_ffn_g'][l][None],
        w_up_a=w_up[:, :D_FF].astype(BF16), w_up_g=w_up[:, D_FF:].astype(BF16),
        cwa=cw[:, :D_FF], cwg=cw[:, D_FF:], cba=cb[None, :D_FF], cbg=cb[None, D_FF:],
        w_down=p['w_down'][l].astype(BF16),
    )


def _seq_constants(l):
    half = ROPE_B // 2
    pos = jnp.arange(l, dtype=F32)
    inv = ROPE_THETA ** (-jnp.arange(half, dtype=F32) / half)
    ang = pos[:, None] * inv[None, :]
    cos, sin = jnp.cos(ang), jnp.sin(ang)
    one = jnp.ones((l, NOPE_B), F32)
    zn = jnp.zeros((l, NOPE_B), F32)
    zh = jnp.zeros((l, half), F32)
    zp = jnp.zeros((l, SLOT_B - QK_B), F32)
    rc = jnp.concatenate([one, cos, cos, zp], axis=1)
    ra = jnp.concatenate([zn, -sin, zh, zp], axis=1)
    rb = jnp.concatenate([zn, zh, sin, zp], axis=1)
    m = jnp.arange(2 * l)
    p_idx = jnp.where(m < l, m, 2 * l - m)
    tlin = jnp.linspace(0.0, 1.0, l, dtype=F32)
    tt = tlin[jnp.minimum(p_idx, l - 1)]
    angf = (2.0 * math.pi / l) * p_idx.astype(F32)
    valid = (m != l).astype(F32)
    bands = (HYENA_EMB - 1) // 2
    fcol = jnp.linspace(1e-4, bands - 1, bands, dtype=F32)[:, None]
    deltas = jnp.abs(jnp.linspace(math.log(HYENA_TARGET) / HYENA_FAST_DECAY,
                                  math.log(HYENA_TARGET) / HYENA_SLOW_DECAY, WIDTH_C, dtype=F32))[:, None]
    return dict(rc=rc, ra=ra, rb=rb, pos=jnp.stack([tt, angf, valid]), fcol=fcol, deltas=deltas)


def _static_tables():
    expand = (jnp.arange(LANE)[:, None] == (jnp.arange(WIDTH_A) // HEAD_DIM_A)[None, :]).astype(BF16)
    slopes = jnp.exp2(-ALIBI_MAX * (jnp.arange(HEADS_A, dtype=F32) + 1.0) / HEADS_A)
    return expand, slopes


def _layer(x2d, lp, sc, expand, slopes, b, l):
    proj = _inproj(x2d, lp['norm_attn_g'], lp['w_r'])
    oas, lses = [], []
    for g, dil in enumerate(DILATIONS):
        o, lse = _mixer_a_group(proj, lp['a_qg'][g], lp['a_kg'][g], -slopes * (dil * LOG2E), b, l, g, dil)
        oas.append(o)
        lses.append(lse)
    qp, kp, vt = _mla_prep(proj, lp['gql'], lp['gkl'], lp['wq'], lp['wk'], lp['wv'], lp['gq'], lp['gk'],
                           sc['rc'], sc['ra'], sc['rb'], b, l)
    yb = _mla_attn(qp, kp, vt, b, l)
    u, x0 = _hyena_prep(proj, lp['c_conv_w'], lp['c_conv_b'], b, l)
    kf = _hyena_filter(sc['pos'], sc['fcol'], lp['w1t'], lp['w1c'], lp['w1s'], lp['b1'], lp['w2t'], lp['b2'],
                       lp['w3t'], lp['fr'], sc['deltas'], l)
    nb = l // TOEP
    u_t = jnp.transpose(u.reshape(b, nb, TOEP, WIDTH_C), (3, 1, 0, 2)).reshape(WIDTH_C, nb * b, TOEP)
    cv_t = _hyena_conv(kf, u_t, nb, b)
    cv = jnp.transpose(cv_t.reshape(WIDTH_C, nb, b, TOEP), (2, 1, 3, 0)).reshape(b * l, WIDTH_C)
    xm = _merge(oas, lses, yb, cv, u, x0, proj, x2d, expand, lp['skip'], lp['bg'],
                lp['wa'], lp['wb'], lp['wc'], lp['wo'], l)
    act = _ffn_up(xm, lp['norm_ffn_g'], lp['w_up_a'], lp['w_up_g'], lp['cwa'], lp['cwg'], lp['cba'], lp['cbg'], l)
    return _ffn_down(xm, act, lp['w_down'])


def _trunk(x, layer_params, expand, slopes):
    b, l, _ = x.shape
    sc = _seq_constants(l)
    y = x.reshape(b * l, D_MODEL)
    for lp in layer_params:
        y = _layer(y, lp, sc, expand, slopes, b, l)
    return y.reshape(b, l, D_MODEL)


def kernel(x_prompt, x_sample, norm_attn_g, w_in, b_gate, a_q_g, a_k_g, b_q_lat_g, b_kv_lat_g, b_w_uq, b_w_ukv, b_q_g, b_k_g, c_conv_w, c_conv_b, c_w1, c_b1, c_w2, c_b2, c_w3, c_freq, c_skip, w_br_a, w_br_b, w_br_c, w_out, norm_ffn_g, w_up, ffn_conv_w, ffn_conv_b, w_down):
    p = dict(norm_attn_g=norm_attn_g, w_in=w_in, b_gate=b_gate, a_q_g=a_q_g, a_k_g=a_k_g, b_q_lat_g=b_q_lat_g,
             b_kv_lat_g=b_kv_lat_g, b_w_uq=b_w_uq, b_w_ukv=b_w_ukv, b_q_g=b_q_g, b_k_g=b_k_g, c_conv_w=c_conv_w,
             c_conv_b=c_conv_b, c_w1=c_w1, c_b1=c_b1, c_w2=c_w2, c_b2=c_b2, c_w3=c_w3, c_freq=c_freq,
             c_skip=c_skip, w_br_a=w_br_a, w_br_b=w_br_b, w_br_c=w_br_c, w_out=w_out, norm_ffn_g=norm_ffn_g,
             w_up=w_up, ffn_conv_w=ffn_conv_w, ffn_conv_b=ffn_conv_b, w_down=w_down)
    layer_params = [_layer_params(p, l) for l in range(DEPTH)]
    expand, slopes = _static_tables()
    return _trunk(x_prompt, layer_params, expand, slopes), _trunk(x_sample, layer_params, expand, slopes)
```

```python
import functools
import math

import jax
import jax.numpy as jnp
from jax import lax
from jax.experimental import pallas as pl
from jax.experimental.pallas import tpu as pltpu

F32 = jnp.float32
BF16 = jnp.bfloat16

D_MODEL = 1024
DEPTH = 2
EPS = 1e-6
HEADS_A = 8
HEAD_DIM_A = 64
DILATIONS = (1, 4, 16)
HALF_A = 64
WIDTH_A = HEADS_A * HEAD_DIM_A
ALIBI_MAX = 8.0
HEADS_B = 8
NOPE_B = 64
ROPE_B = 32
QK_B = NOPE_B + ROPE_B
V_B = 64
Q_RANK = 256
KV_RANK = 128
ROPE_THETA = 10000.0
WIDTH_B = HEADS_B * V_B
WIDTH_C = 512
HYENA_EMB = 33
HYENA_HID = 64
HYENA_FAST_DECAY = 0.3
HYENA_SLOW_DECAY = 1.5
HYENA_TARGET = 1e-2
D_FF = 2816
COLS_A = 3 * 3 * WIDTH_A
COLS_B = Q_RANK + KV_RANK + ROPE_B
COLS_C = 3 * WIDTH_C
COLS_G = 3 * D_MODEL
OFF_A = 0
OFF_C = COLS_A
OFF_G = OFF_C + COLS_C
OFF_B = OFF_G + COLS_G
PROJ_W = OFF_B + 512
LANE = 128
SLOT_B = 128
VROWS = 80
TOEP = 256
LOG2E = math.log2(math.e)
LN2 = math.log(2.0)
MASKED = 1e32
VMEM_LIMIT = 48 * 1024 * 1024


def _cparams(sem):
    return pltpu.CompilerParams(dimension_semantics=sem, vmem_limit_bytes=VMEM_LIMIT)


def _nt_dot(a, b):
    return lax.dot_general(a, b, (((1,), (1,)), ((), ())), preferred_element_type=F32)


def _dot(a, b):
    return jnp.dot(a, b, preferred_element_type=F32)


def _dot_exact(a, b):
    return jnp.dot(a, b, preferred_element_type=F32, precision=lax.Precision.HIGHEST)


def _inproj_kernel(x_ref, g_ref, w_ref, o_ref, h_ref):
    @pl.when(pl.program_id(1) == 0)
    def _():
        x = x_ref[...]
        ms = jnp.mean(x * x, axis=-1, keepdims=True)
        h_ref[...] = (x * lax.rsqrt(ms + EPS) * g_ref[...]).astype(BF16)

    o_ref[...] = _dot(h_ref[...], w_ref[...]).astype(o_ref.dtype)


def _inproj(x2d, gain, w_r):
    t = x2d.shape[0]
    tm, tn = 1024, PROJ_W // 4
    return pl.pallas_call(
        _inproj_kernel,
        grid=(t // tm, PROJ_W // tn),
        in_specs=[
            pl.BlockSpec((tm, D_MODEL), lambda i, j: (i, 0)),
            pl.BlockSpec((1, D_MODEL), lambda i, j: (0, 0)),
            pl.BlockSpec((D_MODEL, tn), lambda i, j: (0, j)),
        ],
        out_specs=pl.BlockSpec((tm, tn), lambda i, j: (i, j)),
        out_shape=jax.ShapeDtypeStruct((t, PROJ_W), BF16),
        scratch_shapes=[pltpu.VMEM((tm, D_MODEL), BF16)],
        compiler_params=_cparams(("parallel", "arbitrary")),
        name="inproj",
    )(x2d, gain, w_r)


def _dil_attn_kernel(negc_ref, q_ref, k_ref, v_ref, qg_ref, kg_ref, o_ref, lse_ref,
                     qn_ref, kn_ref, vn_ref, stg_ref, ost_ref, bias_ref, s_ref, p_ref, r_ref, *, l, dil):
    hb = pl.program_id(1)
    lu = l // dil
    cw = q_ref.shape[-1]
    win = min(2 * LANE, lu)
    nq = lu // LANE
    row = lax.broadcasted_iota(jnp.int32, (cw, cw), 0) // HEAD_DIM_A
    col = lax.broadcasted_iota(jnp.int32, (cw, cw), 1) // HEAD_DIM_A
    seg_mean = jnp.where(row == col, 1.0 / HEAD_DIM_A, 0.0).astype(BF16)

    chunk = 512
    per = chunk // dil

    def norm_body(c, carry):
        r0 = pl.multiple_of(c * chunk, chunk)
        for src, gain, dst in ((q_ref, qg_ref, qn_ref), (k_ref, kg_ref, kn_ref), (v_ref, None, vn_ref)):
            if gain is None and dil == 1:
                continue
            x = src[0, pl.ds(r0, chunk), :].astype(F32)
            if gain is not None:
                msq = _dot((x * x).astype(BF16), seg_mean)
                x = x * lax.rsqrt(msq + EPS) * gain[...]
            if dil == 1:
                dst[pl.ds(r0, chunk), :] = x.astype(BF16)
            else:
                for h in range(cw // LANE):
                    stg_ref[h] = x[:, h * LANE:(h + 1) * LANE]
                for r in range(dil):
                    d0 = pl.multiple_of(r * lu + c * per, per)
                    for h in range(cw // LANE):
                        dst[pl.ds(d0, per), h * LANE:(h + 1) * LANE] = (
                            stg_ref[h, pl.ds(r, per, stride=dil), :].astype(BF16))
        return carry

    lax.fori_loop(0, l // chunk, norm_body, 0)

    lane_lo = lax.broadcasted_iota(jnp.int32, (1, LANE), 1) < HEAD_DIM_A
    npair = cw // LANE
    tiles_per_step = 2
    j_k = lax.broadcasted_iota(jnp.int32, (win, LANE), 0)
    i_q = lax.broadcasted_iota(jnp.int32, (win, LANE), 1)
    for variant in range(3):
        absrel = jnp.abs(j_k - i_q - variant * HALF_A).astype(F32)
        absrel = jnp.where(absrel <= float(HALF_A), absrel, MASKED)
        for pair in range(npair):
            bias_ref[variant * npair + pair] = jnp.concatenate(
                [absrel * negc_ref[4 * hb + 2 * pair + a] for a in range(2)], axis=1)
    row8 = lax.broadcasted_iota(jnp.int32, (8, LANE), 0)

    nsteps = l // (LANE * tiles_per_step)

    def coords(idx):
        r = idx // nq
        q0 = pl.multiple_of((idx % nq) * LANE, LANE)
        w0 = pl.multiple_of(jnp.clip(q0 - HALF_A, 0, lu - win), HALF_A)
        return r, q0, w0, pl.multiple_of(r * lu, LANE)

    def issue_scores(step, slot):
        for u in range(tiles_per_step):
            r, q0, w0, base = coords(step * tiles_per_step + u)
            q = qn_ref[pl.ds(base + q0, LANE), :]
            kw = kn_ref[pl.ds(base + w0, win), :]
            for pair in range(npair):
                qp = q[:, pair * LANE:(pair + 1) * LANE]
                zero = jnp.zeros_like(qp)
                qs = jnp.concatenate([jnp.where(lane_lo, qp, zero), jnp.where(lane_lo, zero, qp)], axis=0)
                s_ref[slot, u * npair + pair] = _nt_dot(kw[:, pair * LANE:(pair + 1) * LANE], qs)

    def softmax(step, slot):
        for u in range(tiles_per_step):
            idx = step * tiles_per_step + u
            r, q0, w0, base = coords(idx)
            variant = (q0 - w0) // HALF_A
            lse_tile = jnp.zeros((8, LANE), F32)
            for pair in range(npair):
                st = s_ref[slot, u * npair + pair] + bias_ref[variant * npair + pair]
                m = jnp.max(st, axis=0, keepdims=True)
                p = jnp.exp2(st - m)
                den = jnp.sum(p, axis=0, keepdims=True)
                p_ref[slot, u * npair + pair] = p.astype(BF16)
                r_ref[slot, u * npair + pair] = 1.0 / den
                lse = (m + jnp.log2(den)) * LN2
                for a in range(2):
                    lse_tile = jnp.where(row8 == 2 * pair + a, lse[:, a * LANE:(a + 1) * LANE], lse_tile)
            lse_ref[0, 0, idx] = lse_tile

    def issue_values(step, slot):
        for u in range(tiles_per_step):
            r, q0, w0, base = coords(step * tiles_per_step + u)
            vw = v_ref[0, pl.ds(w0, win), :] if dil == 1 else vn_ref[pl.ds(base + w0, win), :]
            outs = []
            for pair in range(npair):
                vt = vw[:, pair * LANE:(pair + 1) * LANE].T
                res = _dot(vt, p_ref[slot, u * npair + pair])
                rden = r_ref[slot, u * npair + pair]
                o_t = jnp.concatenate([res[:HEAD_DIM_A, :LANE] * rden[:, :LANE],
                                       res[HEAD_DIM_A:, LANE:] * rden[:, LANE:]], axis=0)
                outs.append(o_t.T)
            if dil == 1:
                o_ref[0, pl.ds(q0, LANE), :] = jnp.concatenate(outs, axis=1).astype(o_ref.dtype)
            else:
                for h, o_pair in enumerate(outs):
                    ost_ref[h, pl.ds(r + q0 * dil, LANE, stride=dil), :] = o_pair

    p_ref[...] = jnp.zeros_like(p_ref)
    r_ref[...] = jnp.zeros_like(r_ref)
    issue_scores(jnp.int32(0), 0)

    per_trip = 8

    def step_body(j, carry):
        for i in range(per_trip):
            k = per_trip * j + i
            issue_scores(jnp.minimum(k + 1, nsteps - 1), (i + 1) % 2)
            issue_values(jnp.maximum(k - 1, 0), (i + 1) % 2)
            softmax(k, i % 2)
        return carry

    lax.fori_loop(0, nsteps // per_trip, step_body, 0)
    issue_values(jnp.int32(nsteps - 1), 1)

    if dil != 1:
        def out_body(c, carry):
            r0 = pl.multiple_of(c * chunk, chunk)
            for h in range(cw // LANE):
                o_ref[0, pl.ds(r0, chunk), h * LANE:(h + 1) * LANE] = (
                    ost_ref[h, pl.ds(r0, chunk), :].astype(o_ref.dtype))
            return carry

        lax.fori_loop(0, l // chunk, out_body, 0)


def _mixer_a_group(proj, qg, kg, negc, b, l, g, dil):
    pv = proj.reshape(b, l, PROJ_W)
    cq, ck, cv = (OFF_A + (0 + g) * 512) // 256, (OFF_A + (3 + g) * 512) // 256, (OFF_A + (6 + g) * 512) // 256

    def in_spec(c0):
        return pl.BlockSpec((1, l, 256), lambda bi, hb: (bi, 0, c0 + hb))

    nt = l // LANE
    win = min(2 * LANE, l // dil)
    o, lse = pl.pallas_call(
        functools.partial(_dil_attn_kernel, l=l, dil=dil),
        grid=(b, 2),
        in_specs=[
            pl.BlockSpec(memory_space=pltpu.SMEM),
            in_spec(cq), in_spec(ck), in_spec(cv),
            pl.BlockSpec((1, 256), lambda bi, hb: (0, 0)),
            pl.BlockSpec((1, 256), lambda bi, hb: (0, 0)),
        ],
        out_specs=[
            pl.BlockSpec((1, l, 256), lambda bi, hb: (bi, 0, hb)),
            pl.BlockSpec((1, 1, nt, 8, LANE), lambda bi, hb: (bi, hb, 0, 0, 0)),
        ],
        out_shape=[
            jax.ShapeDtypeStruct((b, l, WIDTH_A), BF16),
            jax.ShapeDtypeStruct((b, 2, nt, 8, LANE), F32),
        ],
        scratch_shapes=[pltpu.VMEM((l, 256), BF16), pltpu.VMEM((l, 256), BF16),
                        pltpu.VMEM((l if dil != 1 else 16, 256), BF16),
                        pltpu.VMEM((2, 512 if dil != 1 else 8, LANE), F32),
                        pltpu.VMEM((2, l if dil != 1 else 8, LANE), F32),
                        pltpu.VMEM((6, win, 2 * LANE), F32),
                        pltpu.VMEM((2, 4, win, 2 * LANE), F32), pltpu.VMEM((2, 4, win, 2 * LANE), BF16),
                        pltpu.VMEM((2, 4, 1, 2 * LANE), F32)],
        compiler_params=_cparams(("parallel", "arbitrary")),
        name=f"dilated_attention_g{g}",
    )(negc, pv, pv, pv, qg, kg)
    lse = lse[:, :, :, :4, :].reshape(b, 2, dil, nt // dil, 4, LANE)
    lse = lse.transpose(0, 1, 4, 3, 5, 2).reshape(b, HEADS_A, l)
    return o.reshape(b * l, WIDTH_A), lse


def _mla_prep_kernel(pb_ref, gql_ref, gkl_ref, wq_ref, wk_ref, wv_ref, gq_ref, gk_ref, rc_ref, ra_ref, rb_ref,
                     sel_ref, exp_ref, qt_ref, k_ref, vt_ref):
    c = pb_ref[...].astype(F32)
    cq = c[:, :Q_RANK]
    cqn = (cq * lax.rsqrt(jnp.mean(cq * cq, axis=-1, keepdims=True) + EPS) * gql_ref[...]).astype(BF16)
    ckv = c[:, Q_RANK:Q_RANK + KV_RANK]
    ckvn = (ckv * lax.rsqrt(jnp.mean(ckv * ckv, axis=-1, keepdims=True) + EPS) * gkl_ref[...]).astype(BF16)
    q = _dot(cqn, wq_ref[...])
    k = _dot(jnp.concatenate([ckvn, pb_ref[:, Q_RANK + KV_RANK:]], axis=1), wk_ref[...])
    v = _dot(ckvn, wv_ref[...])
    rc, ra, rb = rc_ref[...], ra_ref[...], rb_ref[...]

    lane = lax.broadcasted_iota(jnp.int32, (1, LANE), 1)
    half = ROPE_B // 2

    def finish(x):
        ssq = _dot((x * x).astype(BF16), sel_ref[...])
        r = lax.rsqrt(ssq * (1.0 / QK_B) + EPS)
        r_hi = r.astype(BF16)
        r_lo = (r - r_hi.astype(F32)).astype(BF16)
        scale = _dot(jnp.where(lane < HEADS_B, r_hi, r_lo), exp_ref[...])
        return x * scale

    def rope(xn):
        return xn * rc + pltpu.roll(xn, SLOT_B - half, 1) * ra + pltpu.roll(xn, half, 1) * rb

    qn = finish(q) * gq_ref[...]
    kn = finish(k) * gk_ref[...]
    for h in range(HEADS_B):
        sl = slice(h * SLOT_B, (h + 1) * SLOT_B)
        qt_ref[0, sl, :] = rope(qn[:, sl]).T.astype(qt_ref.dtype)
        k_ref[:, sl] = rope(kn[:, sl]).astype(k_ref.dtype)
    vt_ref[0] = v.T.astype(vt_ref.dtype)


def _mla_prep(proj, gql, gkl, wq, wk, wv, gq, gk, rc, ra, rb, b, l):
    t = b * l
    tm = 512
    per_seq = l // tm
    const = lambda i: (0, 0)
    slot_of_lane = jnp.arange(HEADS_B * SLOT_B) // SLOT_B
    col = jnp.arange(LANE)
    sel = ((col[None, :] % HEADS_B == slot_of_lane[:, None]) & (col[None, :] < 2 * HEADS_B)).astype(BF16)
    spread = sel.T
    return pl.pallas_call(
        _mla_prep_kernel,
        grid=(t // tm,),
        in_specs=[
            pl.BlockSpec((tm, 512), lambda i: (i, OFF_B // 512)),
            pl.BlockSpec((1, Q_RANK), const), pl.BlockSpec((1, KV_RANK), const),
            pl.BlockSpec((Q_RANK, HEADS_B * SLOT_B), const),
            pl.BlockSpec((256, HEADS_B * SLOT_B), const),
            pl.BlockSpec((KV_RANK, WIDTH_B), const),
            pl.BlockSpec((1, HEADS_B * SLOT_B), const), pl.BlockSpec((1, HEADS_B * SLOT_B), const),
            pl.BlockSpec((tm, SLOT_B), lambda i: (i % per_seq, 0)),
            pl.BlockSpec((tm, SLOT_B), lambda i: (i % per_seq, 0)),
            pl.BlockSpec((tm, SLOT_B), lambda i: (i % per_seq, 0)),
            pl.BlockSpec((HEADS_B * SLOT_B, LANE), const), pl.BlockSpec((LANE, HEADS_B * SLOT_B), const),
        ],
        out_specs=[
            pl.BlockSpec((1, HEADS_B * SLOT_B, tm), lambda i: (i // per_seq, 0, i % per_seq)),
            pl.BlockSpec((tm, HEADS_B * SLOT_B), lambda i: (i, 0)),
            pl.BlockSpec((1, WIDTH_B, tm), lambda i: (i // per_seq, 0, i % per_seq)),
        ],
        out_shape=[
            jax.ShapeDtypeStruct((b, HEADS_B * SLOT_B, l), BF16),
            jax.ShapeDtypeStruct((t, HEADS_B * SLOT_B), BF16),
            jax.ShapeDtypeStruct((b, WIDTH_B, l), BF16),
        ],
        compiler_params=_cparams(("parallel",)),
        name="latent_prep",
    )(proj, gql, gkl, wq, wk, wv, gq, gk, rc, ra, rb, sel, spread)


def _mla_attn_kernel(qt_ref, k_ref, vt_ref, o_ref, vaug_ref, *, tk):
    nkv = vaug_ref.shape[1]
    tq = qt_ref.shape[2]

    @pl.when(pl.program_id(2) == 0)
    def _():
        ones_rows = jnp.where(lax.broadcasted_iota(jnp.int32, (VROWS - V_B, tk), 0) == 0, 1.0, 0.0).astype(BF16)
        for a in range(2):
            for j in range(nkv):
                vaug_ref[a, j, 0:V_B, :] = vt_ref[0, a * V_B:(a + 1) * V_B, j * tk:(j + 1) * tk]
                vaug_ref[a, j, V_B:VROWS, :] = ones_rows

    qw = 2 * LANE
    ntile = tq // qw

    def scores(c, j):
        return [_dot(k_ref[0, j * tk:(j + 1) * tk, a * SLOT_B:(a + 1) * SLOT_B],
                     qt_ref[0, a * SLOT_B:(a + 1) * SLOT_B, c * qw:(c + 1) * qw]) for a in range(2)]

    def values(j, ps):
        return [_dot(vaug_ref[a, j], ps[a]) for a in range(2)]

    st = scores(0, 0)
    for c in range(ntile):
        m = [jnp.full((1, qw), -jnp.inf, F32) for _ in range(2)]
        acc = [jnp.zeros((VROWS, qw), F32) for _ in range(2)]
        ps = None
        for j in range(nkv):
            if j + 1 < nkv:
                st_next = scores(c, j + 1)
            else:
                st_next = scores(c + 1, 0) if c + 1 < ntile else None
            pv = values(j - 1, ps) if j > 0 else None
            ps = []
            for a in range(2):
                mn = jnp.maximum(m[a], jnp.max(st[a], axis=0, keepdims=True))
                ps.append(jnp.exp2(st[a] - mn).astype(BF16))
                if pv is not None:
                    acc[a] = (acc[a] + pv[a]) * jnp.exp2(m[a] - mn)
                m[a] = mn
            st = st_next
        pv = values(nkv - 1, ps)
        outs = []
        for a in range(2):
            tot = acc[a] + pv[a]
            outs.append(tot[0:V_B] * (1.0 / tot[V_B:V_B + 1]))
        o_ref[0, c * qw:(c + 1) * qw, :] = jnp.concatenate(outs, axis=0).T.astype(o_ref.dtype)


def _mla_attn(qt, kp, vt, b, l):
    tq, tk = 1024, 256
    k3 = kp.reshape(b, l, HEADS_B * SLOT_B)
    out = pl.pallas_call(
        functools.partial(_mla_attn_kernel, tk=tk),
        grid=(b, HEADS_B // 2, l // tq),
        in_specs=[
            pl.BlockSpec((1, 2 * SLOT_B, tq), lambda bi, p, qi: (bi, p, qi)),
            pl.BlockSpec((1, l, 2 * SLOT_B), lambda bi, p, qi: (bi, 0, p)),
            pl.BlockSpec((1, 2 * V_B, l), lambda bi, p, qi: (bi, p, 0)),
        ],
        out_specs=pl.BlockSpec((1, tq, 2 * V_B), lambda bi, p, qi: (bi, qi, p)),
        out_shape=jax.ShapeDtypeStruct((b, l, WIDTH_B), BF16),
        scratch_shapes=[pltpu.VMEM((2, l // tk, VROWS, tk), BF16)],
        compiler_params=_cparams(("parallel", "parallel", "arbitrary")),
        name="latent_attention",
    )(qt, k3, vt)
    return out.reshape(b * l, WIDTH_B)


def _shift_rows(x, prev_row, next_row):
    n = x.shape[0]
    rid = lax.broadcasted_iota(jnp.int32, x.shape, 0)
    xm = jnp.where(rid == 0, prev_row, pltpu.roll(x, 1, 0))
    xp = jnp.where(rid == n - 1, next_row, pltpu.roll(x, n - 1, 0))
    return xm, xp


def _hyena_prep_kernel(pc_ref, prev_ref, next_ref, w_ref, b_ref, u_ref, x0_ref, *, per_seq):
    i = pl.program_id(0) % per_seq
    x = pc_ref[...].astype(F32)
    hr = prev_ref.shape[0]
    prev_row = jnp.where(i == 0, 0.0, prev_ref[...].astype(F32)[hr - 1:hr, :])
    next_row = jnp.where(i == per_seq - 1, 0.0, next_ref[...].astype(F32)[0:1, :])
    xm, xp = _shift_rows(x, prev_row, next_row)
    y = xm * w_ref[0:1, :] + x * w_ref[1:2, :] + xp * w_ref[2:3, :] + b_ref[...]
    x0_ref[...] = y[:, :WIDTH_C].astype(x0_ref.dtype)
    u_ref[...] = (y[:, 2 * WIDTH_C:] * y[:, WIDTH_C:2 * WIDTH_C]).astype(u_ref.dtype)


def _hyena_prep(proj, conv_w, conv_b, b, l):
    t = b * l
    tm, hr = 512, 16
    per_seq = l // tm
    nh = tm // hr
    cb = OFF_C // COLS_C
    return pl.pallas_call(
        functools.partial(_hyena_prep_kernel, per_seq=per_seq),
        grid=(t // tm,),
        in_specs=[
            pl.BlockSpec((tm, COLS_C), lambda i: (i, cb)),
            pl.BlockSpec((hr, COLS_C), lambda i: (jnp.maximum(i * nh - 1, 0), cb)),
            pl.BlockSpec((hr, COLS_C), lambda i: (jnp.minimum((i + 1) * nh, t // hr - 1), cb)),
            pl.BlockSpec((3, COLS_C), lambda i: (0, 0)),
            pl.BlockSpec((1, COLS_C), lambda i: (0, 0)),
        ],
        out_specs=[pl.BlockSpec((tm, WIDTH_C), lambda i: (i, 0)), pl.BlockSpec((tm, WIDTH_C), lambda i: (i, 0))],
        out_shape=[jax.ShapeDtypeStruct((t, WIDTH_C), BF16), jax.ShapeDtypeStruct((t, WIDTH_C), BF16)],
        compiler_params=_cparams(("parallel",)),
        name="hyena_prep",
    )(proj, proj, proj, conv_w, conv_b)


def _hyena_filter_kernel(pos_ref, fcol_ref, w1t_ref, w1c_ref, w1s_ref, b1_ref, w2_ref, b2_ref, w3_ref, fr_ref,
                         dl_ref, o_ref):
    tt = pos_ref[0:1, :]
    ang = pos_ref[1:2, :]
    valid = pos_ref[2:3, :]
    arg = fcol_ref[...] * ang
    pre1 = w1t_ref[...] * tt + _dot_exact(w1c_ref[...], jnp.cos(arg)) + _dot_exact(w1s_ref[...], jnp.sin(arg))
    fr = fr_ref[...]
    hid = jnp.sin(fr * (pre1 + b1_ref[...]))
    hid = jnp.sin(fr * (_dot_exact(w2_ref[...], hid) + b2_ref[...]))
    filt = _dot_exact(w3_ref[0], hid)
    o_ref[...] = filt * jnp.exp(-dl_ref[...] * tt) * valid


def _hyena_filter(pos, fcol, w1t, w1c, w1s, b1, w2t, b2, w3t, fr, dl, l):
    n = 2048
    const = lambda j: (0, 0)
    return pl.pallas_call(
        _hyena_filter_kernel,
        grid=(2 * l // n,),
        in_specs=[
            pl.BlockSpec((3, n), lambda j: (0, j)),
            pl.BlockSpec((16, 1), const),
            pl.BlockSpec((HYENA_HID, 1), const), pl.BlockSpec((HYENA_HID, 16), const),
            pl.BlockSpec((HYENA_HID, 16), const), pl.BlockSpec((HYENA_HID, 1), const),
            pl.BlockSpec((HYENA_HID, HYENA_HID), const), pl.BlockSpec((HYENA_HID, 1), const),
            pl.BlockSpec((1, WIDTH_C, HYENA_HID), lambda j: (j // (l // n), 0, 0)),
            pl.BlockSpec((HYENA_HID, 1), const), pl.BlockSpec((WIDTH_C, 1), const),
        ],
        out_specs=pl.BlockSpec((WIDTH_C, n), lambda j: (0, j)),
        out_shape=jax.ShapeDtypeStruct((WIDTH_C, 2 * l), F32),
        compiler_params=_cparams(("parallel",)),
        name="hyena_filter",
    )(pos, fcol, w1t, w1c, w1s, b1, w2t, b2, w3t, fr, dl)


def _hyena_conv_kernel(kf_ref, u_ref, o_ref, uf_ref, acc_ref, *, nb, bsz):
    p = TOEP
    n2 = 2 * nb * p

    def channel(c, carry):
        krow = kf_ref[pl.ds(c, 1), :]
        packed_rows = bsz % 16 == 0
        if not packed_rows:
            uf_ref[...] = u_ref[c].astype(F32)
        acc_ref[...] = jnp.zeros_like(acc_ref)
        for d in range(-(nb - 1), nb):
            a0 = (d * p) % n2
            b0 = ((d - 1) * p) % n2
            seg = jnp.concatenate([krow[:, a0:a0 + p], krow[:, b0:b0 + p]], axis=1)
            rolled = pltpu.roll(jnp.broadcast_to(seg, (p, 2 * p)), 0, 1, stride=1, stride_axis=0)
            toep = rolled[:, :p].astype(BF16)
            rows = (nb - abs(d)) * bsz
            src = 0 if d >= 0 else -d * bsz
            dst = d * bsz if d >= 0 else 0
            lhs = u_ref[c, src:src + rows, :] if packed_rows else uf_ref[src:src + rows, :].astype(BF16)
            acc_ref[dst:dst + rows, :] += _dot(lhs, toep)
        o_ref[c] = acc_ref[...].astype(o_ref.dtype)
        return carry

    lax.fori_loop(0, kf_ref.shape[0], channel, 0)


def _hyena_conv(kf, u_t, nb, bsz):
    cblk = 8
    rows = nb * bsz
    return pl.pallas_call(
        functools.partial(_hyena_conv_kernel, nb=nb, bsz=bsz),
        grid=(WIDTH_C // cblk,),
        in_specs=[
            pl.BlockSpec((cblk, 2 * nb * TOEP), lambda i: (i, 0)),
            pl.BlockSpec((cblk, rows, TOEP), lambda i: (i, 0, 0)),
        ],
        out_specs=pl.BlockSpec((cblk, rows, TOEP), lambda i: (i, 0, 0)),
        out_shape=jax.ShapeDtypeStruct((WIDTH_C, rows, TOEP), BF16),
        scratch_shapes=[pltpu.VMEM((rows, TOEP), F32), pltpu.VMEM((rows, TOEP), F32)],
        compiler_params=_cparams(("parallel",)),
        name="hyena_conv",
    )(kf, u_t)


def _merge_kernel(oa0_ref, oa1_ref, oa2_ref, l0_ref, l1_ref, l2_ref, yb_ref, cv_ref, u_ref, x0_ref, pg_ref, x_ref,
                  ex_ref, skip_ref, bg_ref, wa_ref, wb_ref, wc_ref, wo_ref, o_ref):
    lses = [l0_ref[0], l1_ref[0], l2_ref[0]]
    mx = jnp.maximum(jnp.maximum(lses[0], lses[1]), lses[2])
    pad = jnp.zeros((LANE - HEADS_A, mx.shape[1]), F32)
    num = None
    den = None
    for lse, oa in zip(lses, (oa0_ref, oa1_ref, oa2_ref)):
        wt = jnp.concatenate([jnp.exp(lse - mx), pad], axis=0).T
        w = _dot(wt.astype(BF16), ex_ref[...])
        num = w * oa[...].astype(F32) if num is None else num + w * oa[...].astype(F32)
        den = w if den is None else den + w
    ya = (num / den).astype(BF16)
    u = u_ref[...].astype(F32)
    yc = (x0_ref[...].astype(F32) * (cv_ref[...].astype(F32) + skip_ref[...] * u)).astype(BF16)
    mixed = None
    for i, (y, w_ref) in enumerate(((ya, wa_ref), (yb_ref[...], wb_ref), (yc, wc_ref))):
        sl = slice(i * D_MODEL, (i + 1) * D_MODEL)
        gate = jax.nn.sigmoid(pg_ref[:, sl].astype(F32) + bg_ref[:, sl])
        term = gate * _dot(y, w_ref[...])
        mixed = term if mixed is None else mixed + term
    o_ref[...] = x_ref[...] + _dot(mixed.astype(BF16), wo_ref[...])


def _merge(oas, lses, yb, cv, u, x0, proj, x2d, expand, skip, bg, wa, wb, wc, wo, l):
    t = x2d.shape[0]
    tm = 512
    per_seq = l // tm
    row = lambda i: (i, 0)
    const = lambda i: (0, 0)
    half = pl.BlockSpec((tm, 512), row)
    lse_spec = pl.BlockSpec((1, HEADS_A, tm), lambda i: (i // per_seq, 0, i % per_seq))
    return pl.pallas_call(
        _merge_kernel,
        grid=(t // tm,),
        in_specs=[
            half, half, half,
            lse_spec, lse_spec, lse_spec,
            half, half, half, half,
            pl.BlockSpec((tm, COLS_G), lambda i: (i, OFF_G // COLS_G)),
            pl.BlockSpec((tm, D_MODEL), row),
            pl.BlockSpec((LANE, WIDTH_A), const),
            pl.BlockSpec((1, WIDTH_C), const), pl.BlockSpec((1, COLS_G), const),
            pl.BlockSpec((WIDTH_A, D_MODEL), const), pl.BlockSpec((WIDTH_B, D_MODEL), const),
            pl.BlockSpec((WIDTH_C, D_MODEL), const), pl.BlockSpec((D_MODEL, D_MODEL), const),
        ],
        out_specs=pl.BlockSpec((tm, D_MODEL), row),
        out_shape=jax.ShapeDtypeStruct((t, D_MODEL), F32),
        compiler_params=_cparams(("parallel",)),
        name="branch_merge",
    )(*oas, *lses, yb, cv, u, x0, proj, x2d, expand, skip, bg, wa, wb, wc, wo)


def _ffn_up_kernel(x_ref, prev_ref, next_ref, g_ref, wa_ref, wg_ref, cwa_ref, cwg_ref, cba_ref, cbg_ref, o_ref,
                   h_ref, *, per_seq, halo):
    tm = x_ref.shape[0]

    @pl.when(pl.program_id(1) == 0)
    def _():
        i = pl.program_id(0) % per_seq

        def norm(x):
            return x * lax.rsqrt(jnp.mean(x * x, axis=-1, keepdims=True) + EPS) * g_ref[...]

        h_ref[0:halo, :] = jnp.where(i == 0, 0.0, norm(prev_ref[...])).astype(BF16)
        h_ref[halo:halo + tm, :] = norm(x_ref[...]).astype(BF16)
        h_ref[halo + tm:, :] = jnp.where(i == per_seq - 1, 0.0, norm(next_ref[...])).astype(BF16)

    h = h_ref[...]
    n = h.shape[0]

    def conv(w_ref, cw_ref, cb_ref):
        y = _dot(h, w_ref[...])
        ym = pltpu.roll(y, 1, 0)[halo:halo + tm]
        yp = pltpu.roll(y, n - 1, 0)[halo:halo + tm]
        return ym * cw_ref[0:1, :] + y[halo:halo + tm] * cw_ref[1:2, :] + yp * cw_ref[2:3, :] + cb_ref[...]

    a = conv(wa_ref, cwa_ref, cba_ref)
    g = conv(wg_ref, cwg_ref, cbg_ref)
    o_ref[...] = (jax.nn.gelu(a) * g).astype(o_ref.dtype)


def _ffn_up(x2d, gain, wa, wg, cwa, cwg, cba, cbg, l):
    t = x2d.shape[0]
    tm, tn, halo = 512, D_FF // 2, 16
    per_seq = l // tm
    nh = tm // halo
    const = lambda i, j: (0, 0)
    col = lambda i, j: (0, j)
    return pl.pallas_call(
        functools.partial(_ffn_up_kernel, per_seq=per_seq, halo=halo),
        grid=(t // tm, D_FF // tn),
        in_specs=[
            pl.BlockSpec((tm, D_MODEL), lambda i, j: (i, 0)),
            pl.BlockSpec((halo, D_MODEL), lambda i, j: (jnp.maximum(i * nh - 1, 0), 0)),
            pl.BlockSpec((halo, D_MODEL), lambda i, j: (jnp.minimum((i + 1) * nh, t // halo - 1), 0)),
            pl.BlockSpec((1, D_MODEL), const),
            pl.BlockSpec((D_MODEL, tn), col), pl.BlockSpec((D_MODEL, tn), col),
            pl.BlockSpec((3, tn), col), pl.BlockSpec((3, tn), col),
            pl.BlockSpec((1, tn), col), pl.BlockSpec((1, tn), col),
        ],
        out_specs=pl.BlockSpec((tm, tn), lambda i, j: (i, j)),
        out_shape=jax.ShapeDtypeStruct((t, D_FF), BF16),
        scratch_shapes=[pltpu.VMEM((tm + 2 * halo, D_MODEL), BF16)],
        compiler_params=_cparams(("parallel", "arbitrary")),
        name="ffn_up",
    )(x2d, x2d, x2d, gain, wa, wg, cwa, cwg, cba, cbg)


def _ffn_down_kernel(x_ref, a_ref, w_ref, o_ref):
    o_ref[...] = x_ref[...] + _dot(a_ref[...], w_ref[...])


def _ffn_down(x2d, act, w):
    t = x2d.shape[0]
    tm = 512
    return pl.pallas_call(
        _ffn_down_kernel,
        grid=(t // tm,),
        in_specs=[
            pl.BlockSpec((tm, D_MODEL), lambda i: (i, 0)),
            pl.BlockSpec((tm, D_FF), lambda i: (i, 0)),
            pl.BlockSpec((D_FF, D_MODEL), lambda i: (0, 0)),
        ],
        out_specs=pl.BlockSpec((tm, D_MODEL), lambda i: (i, 0)),
        out_shape=jax.ShapeDtypeStruct((t, D_MODEL), F32),
        compiler_params=_cparams(("parallel",)),
        name="ffn_down",
    )(x2d, act, w)


def _layer_params(p, l):
    w_in = p['w_in'][l]
    a_end, b_end, c_end = COLS_A, COLS_A + COLS_B, COLS_A + COLS_B + COLS_C
    w_r = jnp.concatenate([w_in[:, :a_end], w_in[:, b_end:c_end], w_in[:, c_end:], w_in[:, a_end:b_end],
                           jnp.zeros((D_MODEL, PROJ_W - OFF_B - COLS_B), F32)], axis=1).astype(BF16)
    slot_pad = SLOT_B - QK_B

    def slots(w):
        return jnp.pad(w, [(0, 0)] * (w.ndim - 1) + [(0, slot_pad)]).reshape(*w.shape[:-2], HEADS_B * SLOT_B)

    wq = slots(p['b_w_uq'][l].reshape(Q_RANK, HEADS_B, QK_B)).astype(BF16)
    wkv = p['b_w_ukv'][l].reshape(KV_RANK, HEADS_B, NOPE_B + V_B)
    wk_nope = jnp.pad(wkv[:, :, :NOPE_B], ((0, 0), (0, 0), (0, SLOT_B - NOPE_B))).reshape(KV_RANK, -1)
    place = jnp.pad(jnp.eye(ROPE_B, dtype=F32), ((0, 0), (NOPE_B, SLOT_B - QK_B)))
    wk_pe = jnp.tile(place, (1, HEADS_B))
    wk = jnp.concatenate([wk_nope, wk_pe, jnp.zeros((256 - KV_RANK - ROPE_B, HEADS_B * SLOT_B), F32)], axis=0)
    wv = wkv[:, :, NOPE_B:].reshape(KV_RANK, WIDTH_B)
    gq = jnp.tile(jnp.pad(p['b_q_g'][l], (0, slot_pad)), HEADS_B)[None] * (QK_B ** -0.5 * LOG2E)
    gk = jnp.tile(jnp.pad(p['b_k_g'][l], (0, slot_pad)), HEADS_B)[None]
    w_up = p['w_up'][l]
    cw = p['ffn_conv_w'][l]
    cb = p['ffn_conv_b'][l]
    return dict(
        norm_attn_g=p['norm_attn_g'][l][None], w_r=w_r,
        a_qg=[jnp.tile(p['a_q_g'][l, g], 4)[None] * (HEAD_DIM_A ** -0.5 * LOG2E) for g in range(3)],
        a_kg=[jnp.tile(p['a_k_g'][l, g], 4)[None] for g in range(3)],
        gql=p['b_q_lat_g'][l][None], gkl=p['b_kv_lat_g'][l][None],
        wq=wq, wk=wk.astype(BF16), wv=wv.astype(BF16), gq=gq, gk=gk,
        c_conv_w=p['c_conv_w'][l], c_conv_b=p['c_conv_b'][l][None],
        w1t=p['c_w1'][l][0:1].T, w1c=p['c_w1'][l][1:17].T, w1s=-p['c_w1'][l][17:33].T,
        b1=p['c_b1'][l][:, None], w2t=p['c_w2'][l].T, b2=p['c_b2'][l][:, None],
        w3t=p['c_w3'][l].T.reshape(2, WIDTH_C, HYENA_HID), fr=p['c_freq'][l][:, None],
        skip=p['c_skip'][l][None], bg=p['b_gate'][l][None],
        wa=p['w_br_a'][l].astype(BF16), wb=p['w_br_b'][l].astype(BF16), wc=p['w_br_c'][l].astype(BF16),
        wo=p['w_out'][l].astype(BF16), norm_ffn_g=p['norm_ffn_g'][l][None],
        w_up_a=w_up[:, :D_FF].astype(BF16), w_up_g=w_up[:, D_FF:].astype(BF16),
        cwa=cw[:, :D_FF], cwg=cw[:, D_FF:], cba=cb[None, :D_FF], cbg=cb[None, D_FF:],
        w_down=p['w_down'][l].astype(BF16),
    )


def _seq_constants(l):
    half = ROPE_B // 2
    pos = jnp.arange(l, dtype=F32)
    inv = ROPE_THETA ** (-jnp.arange(half, dtype=F32) / half)
    ang = pos[:, None] * inv[None, :]
    cos, sin = jnp.cos(ang), jnp.sin(ang)
    one = jnp.ones((l, NOPE_B), F32)
    zn = jnp.zeros((l, NOPE_B), F32)
    zh = jnp.zeros((l, half), F32)
    zp = jnp.zeros((l, SLOT_B - QK_B), F32)
    rc = jnp.concatenate([one, cos, cos, zp], axis=1)
    ra = jnp.concatenate([zn, -sin, zh, zp], axis=1)
    rb = jnp.concatenate([zn, zh, sin, zp], axis=1)
    m = jnp.arange(2 * l)
    p_idx = jnp.where(m < l, m, 2 * l - m)
    tlin = jnp.linspace(0.0, 1.0, l, dtype=F32)
    tt = tlin[jnp.minimum(p_idx, l - 1)]
    angf = (2.0 * math.pi / l) * p_idx.astype(F32)
    valid = (m != l).astype(F32)
    bands = (HYENA_EMB - 1) // 2
    fcol = jnp.linspace(1e-4, bands - 1, bands, dtype=F32)[:, None]
    deltas = jnp.abs(jnp.linspace(math.log(HYENA_TARGET) / HYENA_FAST_DECAY,
                                  math.log(HYENA_TARGET) / HYENA_SLOW_DECAY, WIDTH_C, dtype=F32))[:, None]
    return dict(rc=rc, ra=ra, rb=rb, pos=jnp.stack([tt, angf, valid]), fcol=fcol, deltas=deltas)


def _static_tables():
    expand = (jnp.arange(LANE)[:, None] == (jnp.arange(WIDTH_A) // HEAD_DIM_A)[None, :]).astype(BF16)
    slopes = jnp.exp2(-ALIBI_MAX * (jnp.arange(HEADS_A, dtype=F32) + 1.0) / HEADS_A)
    return expand, slopes


def _layer(x2d, lp, sc, expand, slopes, b, l):
    proj = _inproj(x2d, lp['norm_attn_g'], lp['w_r'])
    oas, lses = [], []
    for g, dil in enumerate(DILATIONS):
        o, lse = _mixer_a_group(proj, lp['a_qg'][g], lp['a_kg'][g], -slopes * (dil * LOG2E), b, l, g, dil)
        oas.append(o)
        lses.append(lse)
    qp, kp, vt = _mla_prep(proj, lp['gql'], lp['gkl'], lp['wq'], lp['wk'], lp['wv'], lp['gq'], lp['gk'],
                           sc['rc'], sc['ra'], sc['rb'], b, l)
    yb = _mla_attn(qp, kp, vt, b, l)
    u, x0 = _hyena_prep(proj, lp['c_conv_w'], lp['c_conv_b'], b, l)
    kf = _hyena_filter(sc['pos'], sc['fcol'], lp['w1t'], lp['w1c'], lp['w1s'], lp['b1'], lp['w2t'], lp['b2'],
                       lp['w3t'], lp['fr'], sc['deltas'], l)
    nb = l // TOEP
    u_t = jnp.transpose(u.reshape(b, nb, TOEP, WIDTH_C), (3, 1, 0, 2)).reshape(WIDTH_C, nb * b, TOEP)
    cv_t = _hyena_conv(kf, u_t, nb, b)
    cv = jnp.transpose(cv_t.reshape(WIDTH_C, nb, b, TOEP), (2, 1, 3, 0)).reshape(b * l, WIDTH_C)
    xm = _merge(oas, lses, yb, cv, u, x0, proj, x2d, expand, lp['skip'], lp['bg'],
                lp['wa'], lp['wb'], lp['wc'], lp['wo'], l)
    act = _ffn_up(xm, lp['norm_ffn_g'], lp['w_up_a'], lp['w_up_g'], lp['cwa'], lp['cwg'], lp['cba'], lp['cbg'], l)
    return _ffn_down(xm, act, lp['w_down'])


def _trunk(x, layer_params, expand, slopes):
    b, l, _ = x.shape
    sc = _seq_constants(l)
    y = x.reshape(b * l, D_MODEL)
    for lp in layer_params:
        y = _layer(y, lp, sc, expand, slopes, b, l)
    return y.reshape(b, l, D_MODEL)


def kernel(x_prompt, x_sample, norm_attn_g, w_in, b_gate, a_q_g, a_k_g, b_q_lat_g, b_kv_lat_g, b_w_uq, b_w_ukv, b_q_g, b_k_g, c_conv_w, c_conv_b, c_w1, c_b1, c_w2, c_b2, c_w3, c_freq, c_skip, w_br_a, w_br_b, w_br_c, w_out, norm_ffn_g, w_up, ffn_conv_w, ffn_conv_b, w_down):
    p = dict(norm_attn_g=norm_attn_g, w_in=w_in, b_gate=b_gate, a_q_g=a_q_g, a_k_g=a_k_g, b_q_lat_g=b_q_lat_g,
             b_kv_lat_g=b_kv_lat_g, b_w_uq=b_w_uq, b_w_ukv=b_w_ukv, b_q_g=b_q_g, b_k_g=b_k_g, c_conv_w=c_conv_w,
             c_conv_b=c_conv_b, c_w1=c_w1, c_b1=c_b1, c_w2=c_w2, c_b2=c_b2, c_w3=c_w3, c_freq=c_freq,
             c_skip=c_skip, w_br_a=w_br_a, w_br_b=w_br_b, w_br_c=w_br_c, w_out=w_out, norm_ffn_g=norm_ffn_g,
             w_up=w_up, ffn_conv_w=ffn_conv_w, ffn_conv_b=ffn_conv_b, w_down=w_down)
    layer_params = [_layer_params(p, l) for l in range(DEPTH)]
    expand, slopes = _static_tables()
    return _trunk(x_prompt, layer_params, expand, slopes), _trunk(x_sample, layer_params, expand, slopes)
```

```python
import functools
import math

import jax
import jax.numpy as jnp
from jax import lax
from jax.experimental import pallas as pl
from jax.experimental.pallas import tpu as pltpu

F32 = jnp.float32
BF16 = jnp.bfloat16

D_MODEL = 1024
DEPTH = 2
EPS = 1e-6
HEADS_A = 8
HEAD_DIM_A = 64
DILATIONS = (1, 4, 16)
HALF_A = 64
WIDTH_A = HEADS_A * HEAD_DIM_A
ALIBI_MAX = 8.0
HEADS_B = 8
NOPE_B = 64
ROPE_B = 32
QK_B = NOPE_B + ROPE_B
V_B = 64
Q_RANK = 256
KV_RANK = 128
ROPE_THETA = 10000.0
WIDTH_B = HEADS_B * V_B
WIDTH_C = 512
HYENA_EMB = 33
HYENA_HID = 64
HYENA_FAST_DECAY = 0.3
HYENA_SLOW_DECAY = 1.5
HYENA_TARGET = 1e-2
D_FF = 2816
COLS_A = 3 * 3 * WIDTH_A
COLS_B = Q_RANK + KV_RANK + ROPE_B
COLS_C = 3 * WIDTH_C
COLS_G = 3 * D_MODEL
OFF_A = 0
OFF_C = COLS_A
OFF_G = OFF_C + COLS_C
OFF_B = OFF_G + COLS_G
PROJ_W = OFF_B + 512
LANE = 128
SLOT_B = 128
VROWS = 80
TOEP = 256
LOG2E = math.log2(math.e)
LN2 = math.log(2.0)
MASKED = 1e32
VMEM_LIMIT = 48 * 1024 * 1024


def _cparams(sem):
    return pltpu.CompilerParams(dimension_semantics=sem, vmem_limit_bytes=VMEM_LIMIT)


def _nt_dot(a, b):
    return lax.dot_general(a, b, (((1,), (1,)), ((), ())), preferred_element_type=F32)


def _dot(a, b):
    return jnp.dot(a, b, preferred_element_type=F32)


def _dot_exact(a, b):
    return jnp.dot(a, b, preferred_element_type=F32, precision=lax.Precision.HIGHEST)


def _inproj_kernel(x_ref, g_ref, w_ref, o_ref, h_ref):
    @pl.when(pl.program_id(1) == 0)
    def _():
        x = x_ref[...]
        ms = jnp.mean(x * x, axis=-1, keepdims=True)
        h_ref[...] = (x * lax.rsqrt(ms + EPS) * g_ref[...]).astype(BF16)

    o_ref[...] = _dot(h_ref[...], w_ref[...]).astype(o_ref.dtype)


def _inproj(x2d, gain, w_r):
    t = x2d.shape[0]
    tm, tn = 1024, PROJ_W // 4
    return pl.pallas_call(
        _inproj_kernel,
        grid=(t // tm, PROJ_W // tn),
        in_specs=[
            pl.BlockSpec((tm, D_MODEL), lambda i, j: (i, 0)),
            pl.BlockSpec((1, D_MODEL), lambda i, j: (0, 0)),
            pl.BlockSpec((D_MODEL, tn), lambda i, j: (0, j)),
        ],
        out_specs=pl.BlockSpec((tm, tn), lambda i, j: (i, j)),
        out_shape=jax.ShapeDtypeStruct((t, PROJ_W), BF16),
        scratch_shapes=[pltpu.VMEM((tm, D_MODEL), BF16)],
        compiler_params=_cparams(("parallel", "arbitrary")),
        name="inproj",
    )(x2d, gain, w_r)


def _dil_attn_kernel(negc_ref, q_ref, k_ref, v_ref, qg_ref, kg_ref, o_ref, lse_ref,
                     qn_ref, kn_ref, vn_ref, stg_ref, ost_ref, bias_ref, s_ref, p_ref, r_ref, *, l, dil):
    hb = pl.program_id(1)
    lu = l // dil
    cw = q_ref.shape[-1]
    win = min(2 * LANE, lu)
    nq = lu // LANE
    row = lax.broadcasted_iota(jnp.int32, (cw, cw), 0) // HEAD_DIM_A
    col = lax.broadcasted_iota(jnp.int32, (cw, cw), 1) // HEAD_DIM_A
    seg_mean = jnp.where(row == col, 1.0 / HEAD_DIM_A, 0.0).astype(BF16)

    chunk = 512
    per = chunk // dil

    def norm_body(c, carry):
        r0 = pl.multiple_of(c * chunk, chunk)
        for src, gain, dst in ((q_ref, qg_ref, qn_ref), (k_ref, kg_ref, kn_ref), (v_ref, None, vn_ref)):
            if gain is None and dil == 1:
                continue
            x = src[0, pl.ds(r0, chunk), :].astype(F32)
            if gain is not None:
                msq = _dot((x * x).astype(BF16), seg_mean)
                x = x * lax.rsqrt(msq + EPS) * gain[...]
            if dil == 1:
                dst[pl.ds(r0, chunk), :] = x.astype(BF16)
            else:
                for h in range(cw // LANE):
                    stg_ref[h] = x[:, h * LANE:(h + 1) * LANE]
                for r in range(dil):
                    d0 = pl.multiple_of(r * lu + c * per, per)
                    for h in range(cw // LANE):
                        dst[pl.ds(d0, per), h * LANE:(h + 1) * LANE] = (
                            stg_ref[h, pl.ds(r, per, stride=dil), :].astype(BF16))
        return carry

    lax.fori_loop(0, l // chunk, norm_body, 0)

    lane_lo = lax.broadcasted_iota(jnp.int32, (1, LANE), 1) < HEAD_DIM_A
    npair = cw // LANE
    tiles_per_step = 2
    j_k = lax.broadcasted_iota(jnp.int32, (win, LANE), 0)
    i_q = lax.broadcasted_iota(jnp.int32, (win, LANE), 1)
    for variant in range(3):
        absrel = jnp.abs(j_k - i_q - variant * HALF_A).astype(F32)
        absrel = jnp.where(absrel <= float(HALF_A), absrel, MASKED)
        for pair in range(npair):
            bias_ref[variant * npair + pair] = jnp.concatenate(
                [absrel * negc_ref[4 * hb + 2 * pair + a] for a in range(2)], axis=1)
    row8 = lax.broadcasted_iota(jnp.int32, (8, LANE), 0)

    nsteps = l // (LANE * tiles_per_step)

    def coords(idx):
        r = idx // nq
        q0 = pl.multiple_of((idx % nq) * LANE, LANE)
        w0 = pl.multiple_of(jnp.clip(q0 - HALF_A, 0, lu - win), HALF_A)
        return r, q0, w0, pl.multiple_of(r * lu, LANE)

    def issue_scores(step, slot):
        for u in range(tiles_per_step):
            r, q0, w0, base = coords(step * tiles_per_step + u)
            q = qn_ref[pl.ds(base + q0, LANE), :]
            kw = kn_ref[pl.ds(base + w0, win), :]
            for pair in range(npair):
                qp = q[:, pair * LANE:(pair + 1) * LANE]
                zero = jnp.zeros_like(qp)
                qs = jnp.concatenate([jnp.where(lane_lo, qp, zero), jnp.where(lane_lo, zero, qp)], axis=0)
                s_ref[slot, u * npair + pair] = _nt_dot(kw[:, pair * LANE:(pair + 1) * LANE], qs)

    def softmax(step, slot):
        for u in range(tiles_per_step):
            idx = step * tiles_per_step + u
            r, q0, w0, base = coords(idx)
            variant = (q0 - w0) // HALF_A
            lse_tile = jnp.zeros((8, LANE), F32)
            for pair in range(npair):
                st = s_ref[slot, u * npair + pair] + bias_ref[variant * npair + pair]
                m = jnp.max(st, axis=0, keepdims=True)
                p = jnp.exp2(st - m)
                den = jnp.sum(p, axis=0, keepdims=True)
                p_ref[slot, u * npair + pair] = p.astype(BF16)
                r_ref[slot, u * npair + pair] = 1.0 / den
                lse = (m + jnp.log2(den)) * LN2
                for a in range(2):
                    lse_tile = jnp.where(row8 == 2 * pair + a, lse[:, a * LANE:(a + 1) * LANE], lse_tile)
            lse_ref[0, 0, idx] = lse_tile

    def issue_values(step, slot):
        for u in range(tiles_per_step):
            r, q0, w0, base = coords(step * tiles_per_step + u)
            vw = v_ref[0, pl.ds(w0, win), :] if dil == 1 else vn_ref[pl.ds(base + w0, win), :]
            outs = []
            for pair in range(npair):
                vt = vw[:, pair * LANE:(pair + 1) * LANE].T
                res = _dot(vt, p_ref[slot, u * npair + pair])
                rden = r_ref[slot, u * npair + pair]
                o_t = jnp.concatenate([res[:HEAD_DIM_A, :LANE] * rden[:, :LANE],
                                       res[HEAD_DIM_A:, LANE:] * rden[:, LANE:]], axis=0)
                outs.append(o_t.T)
            if dil == 1:
                o_ref[0, pl.ds(q0, LANE), :] = jnp.concatenate(outs, axis=1).astype(o_ref.dtype)
            else:
                for h, o_pair in enumerate(outs):
                    ost_ref[h, pl.ds(r + q0 * dil, LANE, stride=dil), :] = o_pair

    p_ref[...] = jnp.zeros_like(p_ref)
    r_ref[...] = jnp.zeros_like(r_ref)
    issue_scores(jnp.int32(0), 0)

    per_trip = 8

    def step_body(j, carry):
        for i in range(per_trip):
            k = per_trip * j + i
            issue_scores(jnp.minimum(k + 1, nsteps - 1), (i + 1) % 2)
            issue_values(jnp.maximum(k - 1, 0), (i + 1) % 2)
            softmax(k, i % 2)
        return carry

    lax.fori_loop(0, nsteps // per_trip, step_body, 0)
    issue_values(jnp.int32(nsteps - 1), 1)

    if dil != 1:
        def out_body(c, carry):
            r0 = pl.multiple_of(c * chunk, chunk)
            for h in range(cw // LANE):
                o_ref[0, pl.ds(r0, chunk), h * LANE:(h + 1) * LANE] = (
                    ost_ref[h, pl.ds(r0, chunk), :].astype(o_ref.dtype))
            return carry

        lax.fori_loop(0, l // chunk, out_body, 0)


def _mixer_a_group(proj, qg, kg, negc, b, l, g, dil):
    pv = proj.reshape(b, l, PROJ_W)
    cq, ck, cv = (OFF_A + (0 + g) * 512) // 256, (OFF_A + (3 + g) * 512) // 256, (OFF_A + (6 + g) * 512) // 256

    def in_spec(c0):
        return pl.BlockSpec((1, l, 256), lambda bi, hb: (bi, 0, c0 + hb))

    nt = l // LANE
    win = min(2 * LANE, l // dil)
    o, lse = pl.pallas_call(
        functools.partial(_dil_attn_kernel, l=l, dil=dil),
        grid=(b, 2),
        in_specs=[
            pl.BlockSpec(memory_space=pltpu.SMEM),
            in_spec(cq), in_spec(ck), in_spec(cv),
            pl.BlockSpec((1, 256), lambda bi, hb: (0, 0)),
            pl.BlockSpec((1, 256), lambda bi, hb: (0, 0)),
        ],
        out_specs=[
            pl.BlockSpec((1, l, 256), lambda bi, hb: (bi, 0, hb)),
            pl.BlockSpec((1, 1, nt, 8, LANE), lambda bi, hb: (bi, hb, 0, 0, 0)),
        ],
        out_shape=[
            jax.ShapeDtypeStruct((b, l, WIDTH_A), BF16),
            jax.ShapeDtypeStruct((b, 2, nt, 8, LANE), F32),
        ],
        scratch_shapes=[pltpu.VMEM((l, 256), BF16), pltpu.VMEM((l, 256), BF16),
                        pltpu.VMEM((l if dil != 1 else 16, 256), BF16),
                        pltpu.VMEM((2, 512 if dil != 1 else 8, LANE), F32),
                        pltpu.VMEM((2, l if dil != 1 else 8, LANE), F32),
                        pltpu.VMEM((6, win, 2 * LANE), F32),
                        pltpu.VMEM((2, 4, win, 2 * LANE), F32), pltpu.VMEM((2, 4, win, 2 * LANE), BF16),
                        pltpu.VMEM((2, 4, 1, 2 * LANE), F32)],
        compiler_params=_cparams(("parallel", "arbitrary")),
        name=f"dilated_attention_g{g}",
    )(negc, pv, pv, pv, qg, kg)
    lse = lse[:, :, :, :4, :].reshape(b, 2, dil, nt // dil, 4, LANE)
    lse = lse.transpose(0, 1, 4, 3, 5, 2).reshape(b, HEADS_A, l)
    return o.reshape(b * l, WIDTH_A), lse


def _mla_prep_kernel(pb_ref, gql_ref, gkl_ref, wq_ref, wk_ref, wv_ref, gq_ref, gk_ref, rc_ref, rs_ref,
                     sel_ref, exp_ref, qt_ref, k_ref, vt_ref):
    width = HEADS_B * SLOT_B
    c = pb_ref[...].astype(F32)
    cq = c[:, :Q_RANK]
    cqn = (cq * lax.rsqrt(jnp.mean(cq * cq, axis=-1, keepdims=True) + EPS) * gql_ref[...]).astype(BF16)
    ckv = c[:, Q_RANK:Q_RANK + KV_RANK]
    ckvn = (ckv * lax.rsqrt(jnp.mean(ckv * ckv, axis=-1, keepdims=True) + EPS) * gkl_ref[...]).astype(BF16)
    q2 = _dot(cqn, wq_ref[...])
    k2 = _dot(jnp.concatenate([ckvn, pb_ref[:, Q_RANK + KV_RANK:]], axis=1), wk_ref[...])
    v = _dot(ckvn, wv_ref[...])
    q, k = q2[:, :width], k2[:, :width]
    rc, rs = rc_ref[...], rs_ref[...]

    lane = lax.broadcasted_iota(jnp.int32, (1, LANE), 1)

    ssqs = [_dot((x * x).astype(BF16), sel_ref[...]) for x in (q, k)]
    packed = []
    for ssq in ssqs:
        r = lax.rsqrt(ssq * (1.0 / QK_B) + EPS)
        r_hi = r.astype(BF16)
        r_lo = (r - r_hi.astype(F32)).astype(BF16)
        packed.append(jnp.where(lane < HEADS_B, r_hi, r_lo))
    scales = [_dot(pk, exp_ref[...]) for pk in packed]

    qg = q2 * gq_ref[...]
    kg = k2 * gk_ref[...]
    for h in range(HEADS_B):
        sl = slice(h * SLOT_B, (h + 1) * SLOT_B)
        sw = slice(width + h * SLOT_B, width + (h + 1) * SLOT_B)
        qt_ref[0, sl, :] = ((qg[:, sl] * rc + qg[:, sw] * rs) * scales[0][:, sl]).T.astype(qt_ref.dtype)
        k_ref[:, sl] = ((kg[:, sl] * rc + kg[:, sw] * rs) * scales[1][:, sl]).astype(k_ref.dtype)
    vt_ref[0] = v.T.astype(vt_ref.dtype)


def _swap_rotary_halves(w):
    s = w.reshape(*w.shape[:-1], HEADS_B, SLOT_B)
    half = ROPE_B // 2
    s = jnp.concatenate([s[..., :NOPE_B], s[..., NOPE_B + half:QK_B], s[..., NOPE_B:NOPE_B + half], s[..., QK_B:]],
                        axis=-1)
    return s.reshape(w.shape)


def _mla_prep(proj, gql, gkl, wq, wk, wv, gq, gk, rc, rs, b, l):
    t = b * l
    tm = 512
    per_seq = l // tm
    const = lambda i: (0, 0)
    wq, wk, gq, gk = [jnp.concatenate([w, _swap_rotary_halves(w)], axis=1) for w in (wq, wk, gq, gk)]
    slot_of_lane = jnp.arange(HEADS_B * SLOT_B) // SLOT_B
    col = jnp.arange(LANE)
    sel = ((col[None, :] % HEADS_B == slot_of_lane[:, None]) & (col[None, :] < 2 * HEADS_B)).astype(BF16)
    spread = sel.T
    return pl.pallas_call(
        _mla_prep_kernel,
        grid=(t // tm,),
        in_specs=[
            pl.BlockSpec((tm, 512), lambda i: (i, OFF_B // 512)),
            pl.BlockSpec((1, Q_RANK), const), pl.BlockSpec((1, KV_RANK), const),
            pl.BlockSpec((Q_RANK, 2 * HEADS_B * SLOT_B), const),
            pl.BlockSpec((256, 2 * HEADS_B * SLOT_B), const),
            pl.BlockSpec((KV_RANK, WIDTH_B), const),
            pl.BlockSpec((1, 2 * HEADS_B * SLOT_B), const), pl.BlockSpec((1, 2 * HEADS_B * SLOT_B), const),
            pl.BlockSpec((tm, SLOT_B), lambda i: (i % per_seq, 0)),
            pl.BlockSpec((tm, SLOT_B), lambda i: (i % per_seq, 0)),
            pl.BlockSpec((HEADS_B * SLOT_B, LANE), const), pl.BlockSpec((LANE, HEADS_B * SLOT_B), const),
        ],
        out_specs=[
            pl.BlockSpec((1, HEADS_B * SLOT_B, tm), lambda i: (i // per_seq, 0, i % per_seq)),
            pl.BlockSpec((tm, HEADS_B * SLOT_B), lambda i: (i, 0)),
            pl.BlockSpec((1, WIDTH_B, tm), lambda i: (i // per_seq, 0, i % per_seq)),
        ],
        out_shape=[
            jax.ShapeDtypeStruct((b, HEADS_B * SLOT_B, l), BF16),
            jax.ShapeDtypeStruct((t, HEADS_B * SLOT_B), BF16),
            jax.ShapeDtypeStruct((b, WIDTH_B, l), BF16),
        ],
        compiler_params=_cparams(("parallel",)),
        name="latent_prep",
    )(proj, gql, gkl, wq, wk, wv, gq, gk, rc, rs, sel, spread)


def _mla_attn_kernel(qt_ref, k_ref, vt_ref, o_ref, vaug_ref, *, tk):
    nkv = vaug_ref.shape[1]
    tq = qt_ref.shape[2]

    @pl.when(pl.program_id(2) == 0)
    def _():
        ones_rows = jnp.where(lax.broadcasted_iota(jnp.int32, (VROWS - V_B, tk), 0) == 0, 1.0, 0.0).astype(BF16)
        for a in range(2):
            for j in range(nkv):
                vaug_ref[a, j, 0:V_B, :] = vt_ref[0, a * V_B:(a + 1) * V_B, j * tk:(j + 1) * tk]
                vaug_ref[a, j, V_B:VROWS, :] = ones_rows

    qw = 2 * LANE
    ntile = tq // qw

    def scores(c, j):
        return [_dot(k_ref[0, j * tk:(j + 1) * tk, a * SLOT_B:(a + 1) * SLOT_B],
                     qt_ref[0, a * SLOT_B:(a + 1) * SLOT_B, c * qw:(c + 1) * qw]) for a in range(2)]

    def values(j, ps):
        return [_dot(vaug_ref[a, j], ps[a]) for a in range(2)]

    st = scores(0, 0)
    for c in range(ntile):
        m = [jnp.full((1, qw), -jnp.inf, F32) for _ in range(2)]
        acc = [jnp.zeros((VROWS, qw), F32) for _ in range(2)]
        ps = None
        for j in range(nkv):
            if j + 1 < nkv:
                st_next = scores(c, j + 1)
            else:
                st_next = scores(c + 1, 0) if c + 1 < ntile else None
            pv = values(j - 1, ps) if j > 0 else None
            ps = []
            for a in range(2):
                mn = jnp.maximum(m[a], jnp.max(st[a], axis=0, keepdims=True))
                ps.append(jnp.exp2(st[a] - mn).astype(BF16))
                if pv is not None:
                    acc[a] = (acc[a] + pv[a]) * jnp.exp2(m[a] - mn)
                m[a] = mn
            st = st_next
        pv = values(nkv - 1, ps)
        outs = []
        for a in range(2):
            tot = acc[a] + pv[a]
            outs.append(tot[0:V_B] * (1.0 / tot[V_B:V_B + 1]))
        o_ref[0, c * qw:(c + 1) * qw, :] = jnp.concatenate(outs, axis=0).T.astype(o_ref.dtype)


def _mla_attn(qt, kp, vt, b, l):
    tq, tk = 1024, 256
    k3 = kp.reshape(b, l, HEADS_B * SLOT_B)
    out = pl.pallas_call(
        functools.partial(_mla_attn_kernel, tk=tk),
        grid=(b, HEADS_B // 2, l // tq),
        in_specs=[
            pl.BlockSpec((1, 2 * SLOT_B, tq), lambda bi, p, qi: (bi, p, qi)),
            pl.BlockSpec((1, l, 2 * SLOT_B), lambda bi, p, qi: (bi, 0, p)),
            pl.BlockSpec((1, 2 * V_B, l), lambda bi, p, qi: (bi, p, 0)),
        ],
        out_specs=pl.BlockSpec((1, tq, 2 * V_B), lambda bi, p, qi: (bi, qi, p)),
        out_shape=jax.ShapeDtypeStruct((b, l, WIDTH_B), BF16),
        scratch_shapes=[pltpu.VMEM((2, l // tk, VROWS, tk), BF16)],
        compiler_params=_cparams(("parallel", "parallel", "arbitrary")),
        name="latent_attention",
    )(qt, k3, vt)
    return out.reshape(b * l, WIDTH_B)


def _shift_rows(x, prev_row, next_row):
    n = x.shape[0]
    rid = lax.broadcasted_iota(jnp.int32, x.shape, 0)
    xm = jnp.where(rid == 0, prev_row, pltpu.roll(x, 1, 0))
    xp = jnp.where(rid == n - 1, next_row, pltpu.roll(x, n - 1, 0))
    return xm, xp


def _hyena_prep_kernel(pc_ref, prev_ref, next_ref, w_ref, b_ref, u_ref, x0_ref, *, per_seq):
    i = pl.program_id(0) % per_seq
    x = pc_ref[...].astype(F32)
    hr = prev_ref.shape[0]
    prev_row = jnp.where(i == 0, 0.0, prev_ref[...].astype(F32)[hr - 1:hr, :])
    next_row = jnp.where(i == per_seq - 1, 0.0, next_ref[...].astype(F32)[0:1, :])
    xm, xp = _shift_rows(x, prev_row, next_row)
    y = xm * w_ref[0:1, :] + x * w_ref[1:2, :] + xp * w_ref[2:3, :] + b_ref[...]
    x0_ref[...] = y[:, :WIDTH_C].astype(x0_ref.dtype)
    u_ref[...] = (y[:, 2 * WIDTH_C:] * y[:, WIDTH_C:2 * WIDTH_C]).astype(u_ref.dtype)


def _hyena_prep(proj, conv_w, conv_b, b, l):
    t = b * l
    tm, hr = 512, 16
    per_seq = l // tm
    nh = tm // hr
    cb = OFF_C // COLS_C
    return pl.pallas_call(
        functools.partial(_hyena_prep_kernel, per_seq=per_seq),
        grid=(t // tm,),
        in_specs=[
            pl.BlockSpec((tm, COLS_C), lambda i: (i, cb)),
            pl.BlockSpec((hr, COLS_C), lambda i: (jnp.maximum(i * nh - 1, 0), cb)),
            pl.BlockSpec((hr, COLS_C), lambda i: (jnp.minimum((i + 1) * nh, t // hr - 1), cb)),
            pl.BlockSpec((3, COLS_C), lambda i: (0, 0)),
            pl.BlockSpec((1, COLS_C), lambda i: (0, 0)),
        ],
        out_specs=[pl.BlockSpec((tm, WIDTH_C), lambda i: (i, 0)), pl.BlockSpec((tm, WIDTH_C), lambda i: (i, 0))],
        out_shape=[jax.ShapeDtypeStruct((t, WIDTH_C), BF16), jax.ShapeDtypeStruct((t, WIDTH_C), BF16)],
        compiler_params=_cparams(("parallel",)),
        name="hyena_prep",
    )(proj, proj, proj, conv_w, conv_b)


def _hyena_filter_kernel(pos_ref, fcol_ref, w1t_ref, w1c_ref, w1s_ref, b1_ref, w2_ref, b2_ref, w3_ref, fr_ref,
                         dl_ref, o_ref):
    tt = pos_ref[0:1, :]
    ang = pos_ref[1:2, :]
    valid = pos_ref[2:3, :]
    arg = fcol_ref[...] * ang
    pre1 = w1t_ref[...] * tt + _dot_exact(w1c_ref[...], jnp.cos(arg)) + _dot_exact(w1s_ref[...], jnp.sin(arg))
    fr = fr_ref[...]
    hid = jnp.sin(fr * (pre1 + b1_ref[...]))
    hid = jnp.sin(fr * (_dot_exact(w2_ref[...], hid) + b2_ref[...]))
    filt = _dot_exact(w3_ref[0], hid)
    o_ref[...] = filt * jnp.exp(-dl_ref[...] * tt) * valid


def _hyena_filter(pos, fcol, w1t, w1c, w1s, b1, w2t, b2, w3t, fr, dl, l):
    n = 2048
    const = lambda j: (0, 0)
    return pl.pallas_call(
        _hyena_filter_kernel,
        grid=(2 * l // n,),
        in_specs=[
            pl.BlockSpec((3, n), lambda j: (0, j)),
            pl.BlockSpec((16, 1), const),
            pl.BlockSpec((HYENA_HID, 1), const), pl.BlockSpec((HYENA_HID, 16), const),
            pl.BlockSpec((HYENA_HID, 16), const), pl.BlockSpec((HYENA_HID, 1), const),
            pl.BlockSpec((HYENA_HID, HYENA_HID), const), pl.BlockSpec((HYENA_HID, 1), const),
            pl.BlockSpec((1, WIDTH_C, HYENA_HID), lambda j: (j // (l // n), 0, 0)),
            pl.BlockSpec((HYENA_HID, 1), const), pl.BlockSpec((WIDTH_C, 1), const),
        ],
        out_specs=pl.BlockSpec((WIDTH_C, n), lambda j: (0, j)),
        out_shape=jax.ShapeDtypeStruct((WIDTH_C, 2 * l), F32),
        compiler_params=_cparams(("parallel",)),
        name="hyena_filter",
    )(pos, fcol, w1t, w1c, w1s, b1, w2t, b2, w3t, fr, dl)


def _hyena_conv_kernel(kf_ref, u_ref, o_ref, uf_ref, acc_ref, *, nb, bsz):
    p = TOEP
    n2 = 2 * nb * p

    def channel(c, carry):
        krow = kf_ref[pl.ds(c, 1), :]
        packed_rows = bsz % 16 == 0
        if not packed_rows:
            uf_ref[...] = u_ref[c].astype(F32)
        acc_ref[...] = jnp.zeros_like(acc_ref)
        for d in range(-(nb - 1), nb):
            a0 = (d * p) % n2
            b0 = ((d - 1) * p) % n2
            seg = jnp.concatenate([krow[:, a0:a0 + p], krow[:, b0:b0 + p]], axis=1)
            rolled = pltpu.roll(jnp.broadcast_to(seg, (p, 2 * p)), 0, 1, stride=1, stride_axis=0)
            toep = rolled[:, :p].astype(BF16)
            rows = (nb - abs(d)) * bsz
            src = 0 if d >= 0 else -d * bsz
            dst = d * bsz if d >= 0 else 0
            lhs = u_ref[c, src:src + rows, :] if packed_rows else uf_ref[src:src + rows, :].astype(BF16)
            acc_ref[dst:dst + rows, :] += _dot(lhs, toep)
        o_ref[c] = acc_ref[...].astype(o_ref.dtype)
        return carry

    lax.fori_loop(0, kf_ref.shape[0], channel, 0)


def _hyena_conv(kf, u_t, nb, bsz):
    cblk = 8
    rows = nb * bsz
    return pl.pallas_call(
        functools.partial(_hyena_conv_kernel, nb=nb, bsz=bsz),
        grid=(WIDTH_C // cblk,),
        in_specs=[
            pl.BlockSpec((cblk, 2 * nb * TOEP), lambda i: (i, 0)),
            pl.BlockSpec((cblk, rows, TOEP), lambda i: (i, 0, 0)),
        ],
        out_specs=pl.BlockSpec((cblk, rows, TOEP), lambda i: (i, 0, 0)),
        out_shape=jax.ShapeDtypeStruct((WIDTH_C, rows, TOEP), BF16),
        scratch_shapes=[pltpu.VMEM((rows, TOEP), F32), pltpu.VMEM((rows, TOEP), F32)],
        compiler_params=_cparams(("parallel",)),
        name="hyena_conv",
    )(kf, u_t)


def _merge_kernel(oa0_ref, oa1_ref, oa2_ref, l0_ref, l1_ref, l2_ref, yb_ref, cv_ref, u_ref, x0_ref, pg_ref, x_ref,
                  ex_ref, skip_ref, bg_ref, wa_ref, wb_ref, wc_ref, wo_ref, o_ref):
    lses = [l0_ref[0], l1_ref[0], l2_ref[0]]
    mx = jnp.maximum(jnp.maximum(lses[0], lses[1]), lses[2])
    pad = jnp.zeros((LANE - HEADS_A, mx.shape[1]), F32)
    num = None
    den = None
    for lse, oa in zip(lses, (oa0_ref, oa1_ref, oa2_ref)):
        wt = jnp.concatenate([jnp.exp(lse - mx), pad], axis=0).T
        w = _dot(wt.astype(BF16), ex_ref[...])
        num = w * oa[...].astype(F32) if num is None else num + w * oa[...].astype(F32)
        den = w if den is None else den + w
    ya = (num / den).astype(BF16)
    u = u_ref[...].astype(F32)
    yc = (x0_ref[...].astype(F32) * (cv_ref[...].astype(F32) + skip_ref[...] * u)).astype(BF16)
    mixed = None
    for i, (y, w_ref) in enumerate(((ya, wa_ref), (yb_ref[...], wb_ref), (yc, wc_ref))):
        sl = slice(i * D_MODEL, (i + 1) * D_MODEL)
        gate = jax.nn.sigmoid(pg_ref[:, sl].astype(F32) + bg_ref[:, sl])
        term = gate * _dot(y, w_ref[...])
        mixed = term if mixed is None else mixed + term
    o_ref[...] = x_ref[...] + _dot(mixed.astype(BF16), wo_ref[...])


def _merge(oas, lses, yb, cv, u, x0, proj, x2d, expand, skip, bg, wa, wb, wc, wo, l):
    t = x2d.shape[0]
    tm = 512
    per_seq = l // tm
    row = lambda i: (i, 0)
    const = lambda i: (0, 0)
    half = pl.BlockSpec((tm, 512), row)
    lse_spec = pl.BlockSpec((1, HEADS_A, tm), lambda i: (i // per_seq, 0, i % per_seq))
    return pl.pallas_call(
        _merge_kernel,
        grid=(t // tm,),
        in_specs=[
            half, half, half,
            lse_spec, lse_spec, lse_spec,
            half, half, half, half,
            pl.BlockSpec((tm, COLS_G), lambda i: (i, OFF_G // COLS_G)),
            pl.BlockSpec((tm, D_MODEL), row),
            pl.BlockSpec((LANE, WIDTH_A), const),
            pl.BlockSpec((1, WIDTH_C), const), pl.BlockSpec((1, COLS_G), const),
            pl.BlockSpec((WIDTH_A, D_MODEL), const), pl.BlockSpec((WIDTH_B, D_MODEL), const),
            pl.BlockSpec((WIDTH_C, D_MODEL), const), pl.BlockSpec((D_MODEL, D_MODEL), const),
        ],
        out_specs=pl.BlockSpec((tm, D_MODEL), row),
        out_shape=jax.ShapeDtypeStruct((t, D_MODEL), F32),
        compiler_params=_cparams(("parallel",)),
        name="branch_merge",
    )(*oas, *lses, yb, cv, u, x0, proj, x2d, expand, skip, bg, wa, wb, wc, wo)


def _ffn_up_kernel(x_ref, prev_ref, next_ref, g_ref, wa_ref, wg_ref, cwa_ref, cwg_ref, cba_ref, cbg_ref, o_ref,
                   h_ref, *, per_seq, halo):
    tm = x_ref.shape[0]

    @pl.when(pl.program_id(1) == 0)
    def _():
        i = pl.program_id(0) % per_seq

        def norm(x):
            return x * lax.rsqrt(jnp.mean(x * x, axis=-1, keepdims=True) + EPS) * g_ref[...]

        h_ref[0:halo, :] = jnp.where(i == 0, 0.0, norm(prev_ref[...])).astype(BF16)
        h_ref[halo:halo + tm, :] = norm(x_ref[...]).astype(BF16)
        h_ref[halo + tm:, :] = jnp.where(i == per_seq - 1, 0.0, norm(next_ref[...])).astype(BF16)

    h = h_ref[...]
    n = h.shape[0]

    def conv(w_ref, cw_ref, cb_ref):
        y = _dot(h, w_ref[...])
        ym = pltpu.roll(y, 1, 0)[halo:halo + tm]
        yp = pltpu.roll(y, n - 1, 0)[halo:halo + tm]
        return ym * cw_ref[0:1, :] + y[halo:halo + tm] * cw_ref[1:2, :] + yp * cw_ref[2:3, :] + cb_ref[...]

    a = conv(wa_ref, cwa_ref, cba_ref)
    g = conv(wg_ref, cwg_ref, cbg_ref)
    o_ref[...] = (jax.nn.gelu(a) * g).astype(o_ref.dtype)


def _ffn_up(x2d, gain, wa, wg, cwa, cwg, cba, cbg, l):
    t = x2d.shape[0]
    tm, tn, halo = 512, D_FF // 2, 16
    per_seq = l // tm
    nh = tm // halo
    const = lambda i, j: (0, 0)
    col = lambda i, j: (0, j)
    return pl.pallas_call(
        functools.partial(_ffn_up_kernel, per_seq=per_seq, halo=halo),
        grid=(t // tm, D_FF // tn),
        in_specs=[
            pl.BlockSpec((tm, D_MODEL), lambda i, j: (i, 0)),
            pl.BlockSpec((halo, D_MODEL), lambda i, j: (jnp.maximum(i * nh - 1, 0), 0)),
            pl.BlockSpec((halo, D_MODEL), lambda i, j: (jnp.minimum((i + 1) * nh, t // halo - 1), 0)),
            pl.BlockSpec((1, D_MODEL), const),
            pl.BlockSpec((D_MODEL, tn), col), pl.BlockSpec((D_MODEL, tn), col),
            pl.BlockSpec((3, tn), col), pl.BlockSpec((3, tn), col),
            pl.BlockSpec((1, tn), col), pl.BlockSpec((1, tn), col),
        ],
        out_specs=pl.BlockSpec((tm, tn), lambda i, j: (i, j)),
        out_shape=jax.ShapeDtypeStruct((t, D_FF), BF16),
        scratch_shapes=[pltpu.VMEM((tm + 2 * halo, D_MODEL), BF16)],
        compiler_params=_cparams(("parallel", "arbitrary")),
        name="ffn_up",
    )(x2d, x2d, x2d, gain, wa, wg, cwa, cwg, cba, cbg)


def _ffn_down_kernel(x_ref, a_ref, w_ref, o_ref):
    o_ref[...] = x_ref[...] + _dot(a_ref[...], w_ref[...])


def _ffn_down(x2d, act, w):
    t = x2d.shape[0]
    tm = 512
    return pl.pallas_call(
        _ffn_down_kernel,
        grid=(t // tm,),
        in_specs=[
            pl.BlockSpec((tm, D_MODEL), lambda i: (i, 0)),
            pl.BlockSpec((tm, D_FF), lambda i: (i, 0)),
            pl.BlockSpec((D_FF, D_MODEL), lambda i: (0, 0)),
        ],
        out_specs=pl.BlockSpec((tm, D_MODEL), lambda i: (i, 0)),
        out_shape=jax.ShapeDtypeStruct((t, D_MODEL), F32),
        compiler_params=_cparams(("parallel",)),
        name="ffn_down",
    )(x2d, act, w)


def _layer_params(p, l):
    w_in = p['w_in'][l]
    a_end, b_end, c_end = COLS_A, COLS_A + COLS_B, COLS_A + COLS_B + COLS_C
    w_r = jnp.concatenate([w_in[:, :a_end], w_in[:, b_end:c_end], w_in[:, c_end:], w_in[:, a_end:b_end],
                           jnp.zeros((D_MODEL, PROJ_W - OFF_B - COLS_B), F32)], axis=1).astype(BF16)
    slot_pad = SLOT_B - QK_B

    def slots(w):
        return jnp.pad(w, [(0, 0)] * (w.ndim - 1) + [(0, slot_pad)]).reshape(*w.shape[:-2], HEADS_B * SLOT_B)

    wq = slots(p['b_w_uq'][l].reshape(Q_RANK, HEADS_B, QK_B)).astype(BF16)
    wkv = p['b_w_ukv'][l].reshape(KV_RANK, HEADS_B, NOPE_B + V_B)
    wk_nope = jnp.pad(wkv[:, :, :NOPE_B], ((0, 0), (0, 0), (0, SLOT_B - NOPE_B))).reshape(KV_RANK, -1)
    place = jnp.pad(jnp.eye(ROPE_B, dtype=F32), ((0, 0), (NOPE_B, SLOT_B - QK_B)))
    wk_pe = jnp.tile(place, (1, HEADS_B))
    wk = jnp.concatenate([wk_nope, wk_pe, jnp.zeros((256 - KV_RANK - ROPE_B, HEADS_B * SLOT_B), F32)], axis=0)
    wv = wkv[:, :, NOPE_B:].reshape(KV_RANK, WIDTH_B)
    gq = jnp.tile(jnp.pad(p['b_q_g'][l], (0, slot_pad)), HEADS_B)[None] * (QK_B ** -0.5 * LOG2E)
    gk = jnp.tile(jnp.pad(p['b_k_g'][l], (0, slot_pad)), HEADS_B)[None]
    w_up = p['w_up'][l]
    cw = p['ffn_conv_w'][l]
    cb = p['ffn_conv_b'][l]
    return dict(
        norm_attn_g=p['norm_attn_g'][l][None], w_r=w_r,
        a_qg=[jnp.tile(p['a_q_g'][l, g], 4)[None] * (HEAD_DIM_A ** -0.5 * LOG2E) for g in range(3)],
        a_kg=[jnp.tile(p['a_k_g'][l, g], 4)[None] for g in range(3)],
        gql=p['b_q_lat_g'][l][None], gkl=p['b_kv_lat_g'][l][None],
        wq=wq, wk=wk.astype(BF16), wv=wv.astype(BF16), gq=gq, gk=gk,
        c_conv_w=p['c_conv_w'][l], c_conv_b=p['c_conv_b'][l][None],
        w1t=p['c_w1'][l][0:1].T, w1c=p['c_w1'][l][1:17].T, w1s=-p['c_w1'][l][17:33].T,
        b1=p['c_b1'][l][:, None], w2t=p['c_w2'][l].T, b2=p['c_b2'][l][:, None],
        w3t=p['c_w3'][l].T.reshape(2, WIDTH_C, HYENA_HID), fr=p['c_freq'][l][:, None],
        skip=p['c_skip'][l][None], bg=p['b_gate'][l][None],
        wa=p['w_br_a'][l].astype(BF16), wb=p['w_br_b'][l].astype(BF16), wc=p['w_br_c'][l].astype(BF16),
        wo=p['w_out'][l].astype(BF16), norm_ffn_g=p['norm_ffn_g'][l][None],
        w_up_a=w_up[:, :D_FF].astype(BF16), w_up_g=w_up[:, D_FF:].astype(BF16),
        cwa=cw[:, :D_FF], cwg=cw[:, D_FF:], cba=cb[None, :D_FF], cbg=cb[None, D_FF:],
        w_down=p['w_down'][l].astype(BF16),
    )


def _seq_constants(l):
    half = ROPE_B // 2
    pos = jnp.arange(l, dtype=F32)
    inv = ROPE_THETA ** (-jnp.arange(half, dtype=F32) / half)
    ang = pos[:, None] * inv[None, :]
    cos, sin = jnp.cos(ang), jnp.sin(ang)
    one = jnp.ones((l, NOPE_B), F32)
    zn = jnp.zeros((l, NOPE_B), F32)
    zh = jnp.zeros((l, half), F32)
    zp = jnp.zeros((l, SLOT_B - QK_B), F32)
    rc = jnp.concatenate([one, cos, cos, zp], axis=1)
    rs = jnp.concatenate([zn, -sin, sin, zp], axis=1)
    m = jnp.arange(2 * l)
    p_idx = jnp.where(m < l, m, 2 * l - m)
    tlin = jnp.linspace(0.0, 1.0, l, dtype=F32)
    tt = tlin[jnp.minimum(p_idx, l - 1)]
    angf = (2.0 * math.pi / l) * p_idx.astype(F32)
    valid = (m != l).astype(F32)
    bands = (HYENA_EMB - 1) // 2
    fcol = jnp.linspace(1e-4, bands - 1, bands, dtype=F32)[:, None]
    deltas = jnp.abs(jnp.linspace(math.log(HYENA_TARGET) / HYENA_FAST_DECAY,
                                  math.log(HYENA_TARGET) / HYENA_SLOW_DECAY, WIDTH_C, dtype=F32))[:, None]
    return dict(rc=rc, rs=rs, pos=jnp.stack([tt, angf, valid]), fcol=fcol, deltas=deltas)


def _static_tables():
    expand = (jnp.arange(LANE)[:, None] == (jnp.arange(WIDTH_A) // HEAD_DIM_A)[None, :]).astype(BF16)
    slopes = jnp.exp2(-ALIBI_MAX * (jnp.arange(HEADS_A, dtype=F32) + 1.0) / HEADS_A)
    return expand, slopes


def _layer(x2d, lp, sc, expand, slopes, b, l):
    proj = _inproj(x2d, lp['norm_attn_g'], lp['w_r'])
    oas, lses = [], []
    for g, dil in enumerate(DILATIONS):
        o, lse = _mixer_a_group(proj, lp['a_qg'][g], lp['a_kg'][g], -slopes * (dil * LOG2E), b, l, g, dil)
        oas.append(o)
        lses.append(lse)
    qp, kp, vt = _mla_prep(proj, lp['gql'], lp['gkl'], lp['wq'], lp['wk'], lp['wv'], lp['gq'], lp['gk'],
                           sc['rc'], sc['rs'], b, l)
    yb = _mla_attn(qp, kp, vt, b, l)
    u, x0 = _hyena_prep(proj, lp['c_conv_w'], lp['c_conv_b'], b, l)
    kf = _hyena_filter(sc['pos'], sc['fcol'], lp['w1t'], lp['w1c'], lp['w1s'], lp['b1'], lp['w2t'], lp['b2'],
                       lp['w3t'], lp['fr'], sc['deltas'], l)
    nb = l // TOEP
    u_t = jnp.transpose(u.reshape(b, nb, TOEP, WIDTH_C), (3, 1, 0, 2)).reshape(WIDTH_C, nb * b, TOEP)
    cv_t = _hyena_conv(kf, u_t, nb, b)
    cv = jnp.transpose(cv_t.reshape(WIDTH_C, nb, b, TOEP), (2, 1, 3, 0)).reshape(b * l, WIDTH_C)
    xm = _merge(oas, lses, yb, cv, u, x0, proj, x2d, expand, lp['skip'], lp['bg'],
                lp['wa'], lp['wb'], lp['wc'], lp['wo'], l)
    act = _ffn_up(xm, lp['norm_ffn_g'], lp['w_up_a'], lp['w_up_g'], lp['cwa'], lp['cwg'], lp['cba'], lp['cbg'], l)
    return _ffn_down(xm, act, lp['w_down'])


def _trunk(x, layer_params, expand, slopes):
    b, l, _ = x.shape
    sc = _seq_constants(l)
    y = x.reshape(b * l, D_MODEL)
    for lp in layer_params:
        y = _layer(y, lp, sc, expand, slopes, b, l)
    return y.reshape(b, l, D_MODEL)


def kernel(x_prompt, x_sample, norm_attn_g, w_in, b_gate, a_q_g, a_k_g, b_q_lat_g, b_kv_lat_g, b_w_uq, b_w_ukv, b_q_g, b_k_g, c_conv_w, c_conv_b, c_w1, c_b1, c_w2, c_b2, c_w3, c_freq, c_skip, w_br_a, w_br_b, w_br_c, w_out, norm_ffn_g, w_up, ffn_conv_w, ffn_conv_b, w_down):
    p = dict(norm_attn_g=norm_attn_g, w_in=w_in, b_gate=b_gate, a_q_g=a_q_g, a_k_g=a_k_g, b_q_lat_g=b_q_lat_g,
             b_kv_lat_g=b_kv_lat_g, b_w_uq=b_w_uq, b_w_ukv=b_w_ukv, b_q_g=b_q_g, b_k_g=b_k_g, c_conv_w=c_conv_w,
             c_conv_b=c_conv_b, c_w1=c_w1, c_b1=c_b1, c_w2=c_w2, c_b2=c_b2, c_w3=c_w3, c_freq=c_freq,
             c_skip=c_skip, w_br_a=w_br_a, w_br_b=w_br_b, w_br_c=w_br_c, w_out=w_out, norm_ffn_g=norm_ffn_g,
             w_up=w_up, ffn_conv_w=ffn_conv_w, ffn_conv_b=ffn_conv_b, w_down=w_down)
    layer_params = [_layer_params(p, l) for l in range(DEPTH)]
    expand, slopes = _static_tables()
    return _trunk(x_prompt, layer_params, expand, slopes), _trunk(x_sample, layer_params, expand, slopes)
```

```python
import functools
import math

import jax
import jax.numpy as jnp
from jax import lax
from jax.experimental import pallas as pl
from jax.experimental.pallas import tpu as pltpu

F32 = jnp.float32
BF16 = jnp.bfloat16

D_MODEL = 1024
DEPTH = 2
EPS = 1e-6
HEADS_A = 8
HEAD_DIM_A = 64
DILATIONS = (1, 4, 16)
HALF_A = 64
WIDTH_A = HEADS_A * HEAD_DIM_A
ALIBI_MAX = 8.0
HEADS_B = 8
NOPE_B = 64
ROPE_B = 32
QK_B = NOPE_B + ROPE_B
V_B = 64
Q_RANK = 256
KV_RANK = 128
ROPE_THETA = 10000.0
WIDTH_B = HEADS_B * V_B
WIDTH_C = 512
HYENA_EMB = 33
HYENA_HID = 64
HYENA_FAST_DECAY = 0.3
HYENA_SLOW_DECAY = 1.5
HYENA_TARGET = 1e-2
D_FF = 2816
COLS_A = 3 * 3 * WIDTH_A
COLS_B = Q_RANK + KV_RANK + ROPE_B
COLS_C = 3 * WIDTH_C
COLS_G = 3 * D_MODEL
OFF_G = 0
OFF_A = COLS_G
OFF_B = OFF_A + COLS_A
PROJ_W = OFF_B + 512
LANE = 128
SLOT_B = 128
VROWS = 80
TOEP = 256
LOG2E = math.log2(math.e)
LN2 = math.log(2.0)
MASKED = 1e32
VMEM_LIMIT = 48 * 1024 * 1024


def _cparams(sem):
    return pltpu.CompilerParams(dimension_semantics=sem, vmem_limit_bytes=VMEM_LIMIT)


def _nt_dot(a, b):
    return lax.dot_general(a, b, (((1,), (1,)), ((), ())), preferred_element_type=F32)


def _dot(a, b):
    return jnp.dot(a, b, preferred_element_type=F32)


def _dot_exact(a, b):
    return jnp.dot(a, b, preferred_element_type=F32, precision=lax.Precision.HIGHEST)


def _inproj_kernel(x_ref, g_ref, w_ref, o_ref, h_ref):
    @pl.when(pl.program_id(1) == 0)
    def _():
        x = x_ref[...]
        ms = jnp.mean(x * x, axis=-1, keepdims=True)
        h_ref[...] = (x * lax.rsqrt(ms + EPS) * g_ref[...]).astype(BF16)

    o_ref[...] = _dot(h_ref[...], w_ref[...]).astype(o_ref.dtype)


def _inproj(x2d, gain, w_r):
    t = x2d.shape[0]
    tm, tn = 1024, PROJ_W // 4
    return pl.pallas_call(
        _inproj_kernel,
        grid=(t // tm, PROJ_W // tn),
        in_specs=[
            pl.BlockSpec((tm, D_MODEL), lambda i, j: (i, 0)),
            pl.BlockSpec((1, D_MODEL), lambda i, j: (0, 0)),
            pl.BlockSpec((D_MODEL, tn), lambda i, j: (0, j)),
        ],
        out_specs=pl.BlockSpec((tm, tn), lambda i, j: (i, j)),
        out_shape=jax.ShapeDtypeStruct((t, PROJ_W), BF16),
        scratch_shapes=[pltpu.VMEM((tm, D_MODEL), BF16)],
        compiler_params=_cparams(("parallel", "arbitrary")),
        name="inproj",
    )(x2d, gain, w_r)


def _dil_attn_kernel(negc_ref, q_ref, k_ref, v_ref, qg_ref, kg_ref, o_ref, lse_ref,
                     qn_ref, kn_ref, vn_ref, stg_ref, stg2_ref, ost_ref, bias_ref, s_ref, p_ref, r_ref, *, l, dil):
    hb = pl.program_id(1)
    lu = l // dil
    cw = q_ref.shape[-1]
    win = min(2 * LANE, lu)
    nq = lu // LANE
    row = lax.broadcasted_iota(jnp.int32, (cw, cw), 0) // HEAD_DIM_A
    col = lax.broadcasted_iota(jnp.int32, (cw, cw), 1) // HEAD_DIM_A
    seg_mean = jnp.where(row == col, 1.0 / HEAD_DIM_A, 0.0).astype(BF16)

    chunk = 512
    per = chunk // dil
    step4 = 4

    def norm_body(c, carry):
        r0 = pl.multiple_of(c * chunk, chunk)
        for src, gain, dst in ((q_ref, qg_ref, qn_ref), (k_ref, kg_ref, kn_ref), (v_ref, None, vn_ref)):
            if gain is None and dil == 1:
                continue
            x = src[0, pl.ds(r0, chunk), :].astype(F32)
            if gain is not None:
                msq = _dot((x * x).astype(BF16), seg_mean)
                x = x * lax.rsqrt(msq + EPS) * gain[...]
            if dil == 1:
                dst[pl.ds(r0, chunk), :] = x.astype(BF16)
            else:
                for h in range(cw // LANE):
                    stg_ref[h] = x[:, h * LANE:(h + 1) * LANE]
                src = stg_ref
                if dil > step4:
                    sub = chunk // step4
                    for b in range(step4):
                        for h in range(cw // LANE):
                            stg2_ref[h, b * sub:(b + 1) * sub, :] = stg_ref[h, pl.ds(b, sub, stride=step4), :]
                    src = stg2_ref
                for r in range(dil):
                    d0 = pl.multiple_of(r * lu + c * per, per)
                    start = r if dil <= step4 else (r % step4) * (chunk // step4) + r // step4
                    for h in range(cw // LANE):
                        dst[pl.ds(d0, per), h * LANE:(h + 1) * LANE] = (
                            src[h, pl.ds(start, per, stride=min(dil, step4)), :].astype(BF16))
        return carry

    lax.fori_loop(0, l // chunk, norm_body, 0)

    lane_lo = lax.broadcasted_iota(jnp.int32, (1, LANE), 1) < HEAD_DIM_A
    npair = cw // LANE
    tiles_per_step = 2
    j_k = lax.broadcasted_iota(jnp.int32, (win, LANE), 0)
    i_q = lax.broadcasted_iota(jnp.int32, (win, LANE), 1)
    for variant in range(3):
        absrel = jnp.abs(j_k - i_q - variant * HALF_A).astype(F32)
        absrel = jnp.where(absrel <= float(HALF_A), absrel, MASKED)
        for pair in range(npair):
            bias_ref[variant * npair + pair] = jnp.concatenate(
                [absrel * negc_ref[4 * hb + 2 * pair + a] for a in range(2)], axis=1)
    row8 = lax.broadcasted_iota(jnp.int32, (8, LANE), 0)

    nsteps = l // (LANE * tiles_per_step)

    def coords(idx):
        r = idx // nq
        q0 = pl.multiple_of((idx % nq) * LANE, LANE)
        w0 = pl.multiple_of(jnp.clip(q0 - HALF_A, 0, lu - win), HALF_A)
        return r, q0, w0, pl.multiple_of(r * lu, LANE)

    def issue_scores(step, slot):
        for u in range(tiles_per_step):
            r, q0, w0, base = coords(step * tiles_per_step + u)
            q = qn_ref[pl.ds(base + q0, LANE), :]
            kw = kn_ref[pl.ds(base + w0, win), :]
            for pair in range(npair):
                qp = q[:, pair * LANE:(pair + 1) * LANE]
                zero = jnp.zeros_like(qp)
                qs = jnp.concatenate([jnp.where(lane_lo, qp, zero), jnp.where(lane_lo, zero, qp)], axis=0)
                s_ref[slot, u * npair + pair] = _nt_dot(kw[:, pair * LANE:(pair + 1) * LANE], qs)

    def softmax(step, slot):
        for u in range(tiles_per_step):
            idx = step * tiles_per_step + u
            r, q0, w0, base = coords(idx)
            variant = (q0 - w0) // HALF_A
            lse_tile = jnp.zeros((8, LANE), F32)
            for pair in range(npair):
                st = s_ref[slot, u * npair + pair] + bias_ref[variant * npair + pair]
                m = jnp.max(st, axis=0, keepdims=True)
                p = jnp.exp2(st - m)
                den = jnp.sum(p, axis=0, keepdims=True)
                p_ref[slot, u * npair + pair] = p.astype(BF16)
                r_ref[slot, u * npair + pair] = 1.0 / den
                lse = (m + jnp.log2(den)) * LN2
                for a in range(2):
                    lse_tile = jnp.where(row8 == 2 * pair + a, lse[:, a * LANE:(a + 1) * LANE], lse_tile)
            lse_ref[0, 0, idx] = lse_tile

    def issue_values(step, slot):
        for u in range(tiles_per_step):
            r, q0, w0, base = coords(step * tiles_per_step + u)
            vw = v_ref[0, pl.ds(w0, win), :] if dil == 1 else vn_ref[pl.ds(base + w0, win), :]
            outs = []
            for pair in range(npair):
                vt = vw[:, pair * LANE:(pair + 1) * LANE].T
                res = _dot(vt, p_ref[slot, u * npair + pair])
                rden = r_ref[slot, u * npair + pair]
                o_t = jnp.concatenate([res[:HEAD_DIM_A, :LANE] * rden[:, :LANE],
                                       res[HEAD_DIM_A:, LANE:] * rden[:, LANE:]], axis=0)
                outs.append(o_t.T)
            if dil == 1:
                o_ref[0, pl.ds(q0, LANE), :] = jnp.concatenate(outs, axis=1).astype(o_ref.dtype)
            else:
                for h, o_pair in enumerate(outs):
                    ost_ref[h, pl.ds(r + q0 * dil, LANE, stride=dil), :] = o_pair

    p_ref[...] = jnp.zeros_like(p_ref)
    r_ref[...] = jnp.zeros_like(r_ref)
    issue_scores(jnp.int32(0), 0)

    per_trip = 8

    def step_body(j, carry):
        for i in range(per_trip):
            k = per_trip * j + i
            issue_scores(jnp.minimum(k + 1, nsteps - 1), (i + 1) % 2)
            issue_values(jnp.maximum(k - 1, 0), (i + 1) % 2)
            softmax(k, i % 2)
        return carry

    lax.fori_loop(0, nsteps // per_trip, step_body, 0)
    issue_values(jnp.int32(nsteps - 1), 1)

    if dil != 1:
        def out_body(c, carry):
            r0 = pl.multiple_of(c * chunk, chunk)
            for h in range(cw // LANE):
                o_ref[0, pl.ds(r0, chunk), h * LANE:(h + 1) * LANE] = (
                    ost_ref[h, pl.ds(r0, chunk), :].astype(o_ref.dtype))
            return carry

        lax.fori_loop(0, l // chunk, out_body, 0)


def _mixer_a_group(proj, qg, kg, negc, b, l, g, dil):
    pv = proj.reshape(b, l, PROJ_W)
    cq, ck, cv = (OFF_A + (0 + g) * 512) // 256, (OFF_A + (3 + g) * 512) // 256, (OFF_A + (6 + g) * 512) // 256

    def in_spec(c0):
        return pl.BlockSpec((1, l, 256), lambda bi, hb: (bi, 0, c0 + hb))

    nt = l // LANE
    win = min(2 * LANE, l // dil)
    o, lse = pl.pallas_call(
        functools.partial(_dil_attn_kernel, l=l, dil=dil),
        grid=(b, 2),
        in_specs=[
            pl.BlockSpec(memory_space=pltpu.SMEM),
            in_spec(cq), in_spec(ck), in_spec(cv),
            pl.BlockSpec((1, 256), lambda bi, hb: (0, 0)),
            pl.BlockSpec((1, 256), lambda bi, hb: (0, 0)),
        ],
        out_specs=[
            pl.BlockSpec((1, l, 256), lambda bi, hb: (bi, 0, hb)),
            pl.BlockSpec((1, 1, nt, 8, LANE), lambda bi, hb: (bi, hb, 0, 0, 0)),
        ],
        out_shape=[
            jax.ShapeDtypeStruct((b, l, WIDTH_A), BF16),
            jax.ShapeDtypeStruct((b, 2, nt, 8, LANE), F32),
        ],
        scratch_shapes=[pltpu.VMEM((l, 256), BF16), pltpu.VMEM((l, 256), BF16),
                        pltpu.VMEM((l if dil != 1 else 16, 256), BF16),
                        pltpu.VMEM((2, 512 if dil != 1 else 8, LANE), F32),
                        pltpu.VMEM((2, 512 if dil > 4 else 8, LANE), F32),
                        pltpu.VMEM((2, l if dil != 1 else 8, LANE), F32),
                        pltpu.VMEM((6, win, 2 * LANE), F32),
                        pltpu.VMEM((2, 4, win, 2 * LANE), F32), pltpu.VMEM((2, 4, win, 2 * LANE), BF16),
                        pltpu.VMEM((2, 4, 1, 2 * LANE), F32)],
        compiler_params=_cparams(("parallel", "arbitrary")),
        name=f"dilated_attention_g{g}",
    )(negc, pv, pv, pv, qg, kg)
    lse = lse[:, :, :, :4, :].reshape(b, 2, dil, nt // dil, 4, LANE)
    lse = lse.transpose(0, 1, 4, 3, 5, 2).reshape(b, HEADS_A, l)
    return o.reshape(b * l, WIDTH_A), lse


def _mla_prep_kernel(pb_ref, gql_ref, gkl_ref, wq_ref, wk_ref, wv_ref, gq_ref, gk_ref, rc_ref, rs_ref,
                     sel_ref, exp_ref, qt_ref, k_ref, vt_ref):
    width = HEADS_B * SLOT_B
    c = pb_ref[...].astype(F32)
    cq = c[:, :Q_RANK]
    cqn = (cq * lax.rsqrt(jnp.mean(cq * cq, axis=-1, keepdims=True) + EPS) * gql_ref[...]).astype(BF16)
    ckv = c[:, Q_RANK:Q_RANK + KV_RANK]
    ckvn = (ckv * lax.rsqrt(jnp.mean(ckv * ckv, axis=-1, keepdims=True) + EPS) * gkl_ref[...]).astype(BF16)
    q2 = _dot(cqn, wq_ref[...])
    k2 = _dot(jnp.concatenate([ckvn, pb_ref[:, Q_RANK + KV_RANK:]], axis=1), wk_ref[...])
    v = _dot(ckvn, wv_ref[...])
    q, k = q2[:, :width], k2[:, :width]
    rc, rs = rc_ref[...], rs_ref[...]

    lane = lax.broadcasted_iota(jnp.int32, (1, LANE), 1)

    ssqs = [_dot((x * x).astype(BF16), sel_ref[...]) for x in (q, k)]
    packed = []
    for ssq in ssqs:
        r = lax.rsqrt(ssq * (1.0 / QK_B) + EPS)
        r_hi = r.astype(BF16)
        r_lo = (r - r_hi.astype(F32)).astype(BF16)
        packed.append(jnp.where(lane < HEADS_B, r_hi, r_lo))
    scales = [_dot(pk, exp_ref[...]) for pk in packed]

    qg = q2 * gq_ref[...]
    kg = k2 * gk_ref[...]
    for h in range(HEADS_B):
        sl = slice(h * SLOT_B, (h + 1) * SLOT_B)
        sw = slice(width + h * SLOT_B, width + (h + 1) * SLOT_B)
        qt_ref[0, sl, :] = ((qg[:, sl] * rc + qg[:, sw] * rs) * scales[0][:, sl]).T.astype(qt_ref.dtype)
        k_ref[:, sl] = ((kg[:, sl] * rc + kg[:, sw] * rs) * scales[1][:, sl]).astype(k_ref.dtype)
    vt_ref[0] = v.T.astype(vt_ref.dtype)


def _swap_rotary_halves(w):
    s = w.reshape(*w.shape[:-1], HEADS_B, SLOT_B)
    half = ROPE_B // 2
    s = jnp.concatenate([s[..., :NOPE_B], s[..., NOPE_B + half:QK_B], s[..., NOPE_B:NOPE_B + half], s[..., QK_B:]],
                        axis=-1)
    return s.reshape(w.shape)


def _mla_prep(proj, gql, gkl, wq, wk, wv, gq, gk, rc, rs, b, l):
    t = b * l
    tm = 512
    per_seq = l // tm
    const = lambda i: (0, 0)
    wq, wk, gq, gk = [jnp.concatenate([w, _swap_rotary_halves(w)], axis=1) for w in (wq, wk, gq, gk)]
    slot_of_lane = jnp.arange(HEADS_B * SLOT_B) // SLOT_B
    col = jnp.arange(LANE)
    sel = ((col[None, :] % HEADS_B == slot_of_lane[:, None]) & (col[None, :] < 2 * HEADS_B)).astype(BF16)
    spread = sel.T
    return pl.pallas_call(
        _mla_prep_kernel,
        grid=(t // tm,),
        in_specs=[
            pl.BlockSpec((tm, 512), lambda i: (i, OFF_B // 512)),
            pl.BlockSpec((1, Q_RANK), const), pl.BlockSpec((1, KV_RANK), const),
            pl.BlockSpec((Q_RANK, 2 * HEADS_B * SLOT_B), const),
            pl.BlockSpec((256, 2 * HEADS_B * SLOT_B), const),
            pl.BlockSpec((KV_RANK, WIDTH_B), const),
            pl.BlockSpec((1, 2 * HEADS_B * SLOT_B), const), pl.BlockSpec((1, 2 * HEADS_B * SLOT_B), const),
            pl.BlockSpec((tm, SLOT_B), lambda i: (i % per_seq, 0)),
            pl.BlockSpec((tm, SLOT_B), lambda i: (i % per_seq, 0)),
            pl.BlockSpec((HEADS_B * SLOT_B, LANE), const), pl.BlockSpec((LANE, HEADS_B * SLOT_B), const),
        ],
        out_specs=[
            pl.BlockSpec((1, HEADS_B * SLOT_B, tm), lambda i: (i // per_seq, 0, i % per_seq)),
            pl.BlockSpec((tm, HEADS_B * SLOT_B), lambda i: (i, 0)),
            pl.BlockSpec((1, WIDTH_B, tm), lambda i: (i // per_seq, 0, i % per_seq)),
        ],
        out_shape=[
            jax.ShapeDtypeStruct((b, HEADS_B * SLOT_B, l), BF16),
            jax.ShapeDtypeStruct((t, HEADS_B * SLOT_B), BF16),
            jax.ShapeDtypeStruct((b, WIDTH_B, l), BF16),
        ],
        compiler_params=_cparams(("parallel",)),
        name="latent_prep",
    )(proj, gql, gkl, wq, wk, wv, gq, gk, rc, rs, sel, spread)


def _mla_attn_kernel(qt_ref, k_ref, vt_ref, o_ref, vaug_ref, *, tk):
    nkv = vaug_ref.shape[1]
    tq = qt_ref.shape[2]

    @pl.when(pl.program_id(2) == 0)
    def _():
        ones_rows = jnp.where(lax.broadcasted_iota(jnp.int32, (VROWS - V_B, tk), 0) == 0, 1.0, 0.0).astype(BF16)
        for a in range(2):
            for j in range(nkv):
                vaug_ref[a, j, 0:V_B, :] = vt_ref[0, a * V_B:(a + 1) * V_B, j * tk:(j + 1) * tk]
                vaug_ref[a, j, V_B:VROWS, :] = ones_rows

    qw = 2 * LANE
    ntile = tq // qw

    def scores(c, j):
        return [_dot(k_ref[0, j * tk:(j + 1) * tk, a * SLOT_B:(a + 1) * SLOT_B],
                     qt_ref[0, a * SLOT_B:(a + 1) * SLOT_B, c * qw:(c + 1) * qw]) for a in range(2)]

    def values(j, ps):
        return [_dot(vaug_ref[a, j], ps[a]) for a in range(2)]

    st = scores(0, 0)
    for c in range(ntile):
        m = [jnp.full((1, qw), -jnp.inf, F32) for _ in range(2)]
        acc = [jnp.zeros((VROWS, qw), F32) for _ in range(2)]
        ps = None
        for j in range(nkv):
            if j + 1 < nkv:
                st_next = scores(c, j + 1)
            else:
                st_next = scores(c + 1, 0) if c + 1 < ntile else None
            pv = values(j - 1, ps) if j > 0 else None
            ps = []
            for a in range(2):
                mn = jnp.maximum(m[a], jnp.max(st[a], axis=0, keepdims=True))
                ps.append(jnp.exp2(st[a] - mn).astype(BF16))
                if pv is not None:
                    acc[a] = (acc[a] + pv[a]) * jnp.exp2(m[a] - mn)
                m[a] = mn
            st = st_next
        pv = values(nkv - 1, ps)
        outs = []
        for a in range(2):
            tot = acc[a] + pv[a]
            outs.append(tot[0:V_B] * (1.0 / tot[V_B:V_B + 1]))
        o_ref[0, c * qw:(c + 1) * qw, :] = jnp.concatenate(outs, axis=0).T.astype(o_ref.dtype)


def _mla_attn(qt, kp, vt, b, l):
    tq, tk = 1024, 256
    k3 = kp.reshape(b, l, HEADS_B * SLOT_B)
    out = pl.pallas_call(
        functools.partial(_mla_attn_kernel, tk=tk),
        grid=(b, HEADS_B // 2, l // tq),
        in_specs=[
            pl.BlockSpec((1, 2 * SLOT_B, tq), lambda bi, p, qi: (bi, p, qi)),
            pl.BlockSpec((1, l, 2 * SLOT_B), lambda bi, p, qi: (bi, 0, p)),
            pl.BlockSpec((1, 2 * V_B, l), lambda bi, p, qi: (bi, p, 0)),
        ],
        out_specs=pl.BlockSpec((1, tq, 2 * V_B), lambda bi, p, qi: (bi, qi, p)),
        out_shape=jax.ShapeDtypeStruct((b, l, WIDTH_B), BF16),
        scratch_shapes=[pltpu.VMEM((2, l // tk, VROWS, tk), BF16)],
        compiler_params=_cparams(("parallel", "parallel", "arbitrary")),
        name="latent_attention",
    )(qt, k3, vt)
    return out.reshape(b * l, WIDTH_B)


def _hyena_proj_kernel(x_ref, prev_ref, next_ref, g_ref, w_ref, cw_ref, cb_ref, u_ref, x0_ref, *, per_seq, halo):
    tm = x_ref.shape[0]
    i = pl.program_id(0) % per_seq

    def norm(x):
        return x * lax.rsqrt(jnp.mean(x * x, axis=-1, keepdims=True) + EPS) * g_ref[...]

    h = jnp.concatenate([jnp.where(i == 0, 0.0, norm(prev_ref[...])).astype(BF16),
                         norm(x_ref[...]).astype(BF16),
                         jnp.where(i == per_seq - 1, 0.0, norm(next_ref[...])).astype(BF16)], axis=0)
    n = h.shape[0]
    y = _dot(h, w_ref[...])
    ym = pltpu.roll(y, 1, 0)[halo:halo + tm]
    yp = pltpu.roll(y, n - 1, 0)[halo:halo + tm]
    c = ym * cw_ref[0:1, :] + y[halo:halo + tm] * cw_ref[1:2, :] + yp * cw_ref[2:3, :] + cb_ref[...]
    x0_ref[...] = c[:, :WIDTH_C].astype(x0_ref.dtype)
    u_ref[...] = (c[:, 2 * WIDTH_C:] * c[:, WIDTH_C:2 * WIDTH_C]).astype(u_ref.dtype)


def _hyena_proj(x2d, gain, w_c, conv_w, conv_b, l):
    t = x2d.shape[0]
    tm, halo = 512, 16
    per_seq = l // tm
    nh = tm // halo
    const = lambda i: (0, 0)
    return pl.pallas_call(
        functools.partial(_hyena_proj_kernel, per_seq=per_seq, halo=halo),
        grid=(t // tm,),
        in_specs=[
            pl.BlockSpec((tm, D_MODEL), lambda i: (i, 0)),
            pl.BlockSpec((halo, D_MODEL), lambda i: (jnp.maximum(i * nh - 1, 0), 0)),
            pl.BlockSpec((halo, D_MODEL), lambda i: (jnp.minimum((i + 1) * nh, t // halo - 1), 0)),
            pl.BlockSpec((1, D_MODEL), const),
            pl.BlockSpec((D_MODEL, COLS_C), const),
            pl.BlockSpec((3, COLS_C), const),
            pl.BlockSpec((1, COLS_C), const),
        ],
        out_specs=[pl.BlockSpec((tm, WIDTH_C), lambda i: (i, 0)), pl.BlockSpec((tm, WIDTH_C), lambda i: (i, 0))],
        out_shape=[jax.ShapeDtypeStruct((t, WIDTH_C), BF16), jax.ShapeDtypeStruct((t, WIDTH_C), BF16)],
        compiler_params=_cparams(("parallel",)),
        name="hyena_proj",
    )(x2d, x2d, x2d, gain, w_c, conv_w, conv_b)


def _hyena_filter_kernel(pos_ref, fcol_ref, w1t_ref, w1c_ref, w1s_ref, b1_ref, w2_ref, b2_ref, w3_ref, fr_ref,
                         dl_ref, o_ref):
    tt = pos_ref[0:1, :]
    ang = pos_ref[1:2, :]
    valid = pos_ref[2:3, :]
    arg = fcol_ref[...] * ang
    pre1 = w1t_ref[...] * tt + _dot_exact(w1c_ref[...], jnp.cos(arg)) + _dot_exact(w1s_ref[...], jnp.sin(arg))
    fr = fr_ref[...]
    hid = jnp.sin(fr * (pre1 + b1_ref[...]))
    hid = jnp.sin(fr * (_dot_exact(w2_ref[...], hid) + b2_ref[...]))
    filt = _dot_exact(w3_ref[0], hid)
    o_ref[...] = filt * jnp.exp(-dl_ref[...] * tt) * valid


def _hyena_filter(pos, fcol, w1t, w1c, w1s, b1, w2t, b2, w3t, fr, dl, l):
    n = 2048
    const = lambda j: (0, 0)
    return pl.pallas_call(
        _hyena_filter_kernel,
        grid=(2 * l // n,),
        in_specs=[
            pl.BlockSpec((3, n), lambda j: (0, j)),
            pl.BlockSpec((16, 1), const),
            pl.BlockSpec((HYENA_HID, 1), const), pl.BlockSpec((HYENA_HID, 16), const),
            pl.BlockSpec((HYENA_HID, 16), const), pl.BlockSpec((HYENA_HID, 1), const),
            pl.BlockSpec((HYENA_HID, HYENA_HID), const), pl.BlockSpec((HYENA_HID, 1), const),
            pl.BlockSpec((1, WIDTH_C, HYENA_HID), lambda j: (j // (l // n), 0, 0)),
            pl.BlockSpec((HYENA_HID, 1), const), pl.BlockSpec((WIDTH_C, 1), const),
        ],
        out_specs=pl.BlockSpec((WIDTH_C, n), lambda j: (0, j)),
        out_shape=jax.ShapeDtypeStruct((WIDTH_C, 2 * l), F32),
        compiler_params=_cparams(("parallel",)),
        name="hyena_filter",
    )(pos, fcol, w1t, w1c, w1s, b1, w2t, b2, w3t, fr, dl)


def _hyena_conv_kernel(kf_ref, u_ref, o_ref, uf_ref, acc_ref, *, nb, bsz):
    p = TOEP
    n2 = 2 * nb * p

    def channel(c, carry):
        krow = kf_ref[pl.ds(c, 1), :]
        packed_rows = bsz % 16 == 0
        if not packed_rows:
            uf_ref[...] = u_ref[c].astype(F32)
        acc_ref[...] = jnp.zeros_like(acc_ref)
        for d in range(-(nb - 1), nb):
            a0 = (d * p) % n2
            b0 = ((d - 1) * p) % n2
            seg = jnp.concatenate([krow[:, a0:a0 + p], krow[:, b0:b0 + p]], axis=1)
            rolled = pltpu.roll(jnp.broadcast_to(seg, (p, 2 * p)), 0, 1, stride=1, stride_axis=0)
            toep = rolled[:, :p].astype(BF16)
            rows = (nb - abs(d)) * bsz
            src = 0 if d >= 0 else -d * bsz
            dst = d * bsz if d >= 0 else 0
            lhs = u_ref[c, src:src + rows, :] if packed_rows else uf_ref[src:src + rows, :].astype(BF16)
            acc_ref[dst:dst + rows, :] += _dot(lhs, toep)
        o_ref[c] = acc_ref[...].astype(o_ref.dtype)
        return carry

    lax.fori_loop(0, kf_ref.shape[0], channel, 0)


def _hyena_conv(kf, u_t, nb, bsz):
    cblk = 8
    rows = nb * bsz
    return pl.pallas_call(
        functools.partial(_hyena_conv_kernel, nb=nb, bsz=bsz),
        grid=(WIDTH_C // cblk,),
        in_specs=[
            pl.BlockSpec((cblk, 2 * nb * TOEP), lambda i: (i, 0)),
            pl.BlockSpec((cblk, rows, TOEP), lambda i: (i, 0, 0)),
        ],
        out_specs=pl.BlockSpec((cblk, rows, TOEP), lambda i: (i, 0, 0)),
        out_shape=jax.ShapeDtypeStruct((WIDTH_C, rows, TOEP), BF16),
        scratch_shapes=[pltpu.VMEM((rows, TOEP), F32), pltpu.VMEM((rows, TOEP), F32)],
        compiler_params=_cparams(("parallel",)),
        name="hyena_conv",
    )(kf, u_t)


def _merge_kernel(oa0_ref, oa1_ref, oa2_ref, l0_ref, l1_ref, l2_ref, yb_ref, cv_ref, u_ref, x0_ref, pg_ref, x_ref,
                  ex_ref, skip_ref, bg_ref, wa_ref, wb_ref, wc_ref, wo_ref, o_ref):
    lses = [l0_ref[0], l1_ref[0], l2_ref[0]]
    mx = jnp.maximum(jnp.maximum(lses[0], lses[1]), lses[2])
    pad = jnp.zeros((LANE - HEADS_A, mx.shape[1]), F32)
    num = None
    den = None
    for lse, oa in zip(lses, (oa0_ref, oa1_ref, oa2_ref)):
        wt = jnp.concatenate([jnp.exp(lse - mx), pad], axis=0).T
        w = _dot(wt.astype(BF16), ex_ref[...])
        num = w * oa[...].astype(F32) if num is None else num + w * oa[...].astype(F32)
        den = w if den is None else den + w
    ya = (num / den).astype(BF16)
    u = u_ref[...].astype(F32)
    yc = (x0_ref[...].astype(F32) * (cv_ref[...].astype(F32) + skip_ref[...] * u)).astype(BF16)
    mixed = None
    for i, (y, w_ref) in enumerate(((ya, wa_ref), (yb_ref[...], wb_ref), (yc, wc_ref))):
        sl = slice(i * D_MODEL, (i + 1) * D_MODEL)
        gate = jax.nn.sigmoid(pg_ref[:, sl].astype(F32) + bg_ref[:, sl])
        term = gate * _dot(y, w_ref[...])
        mixed = term if mixed is None else mixed + term
    o_ref[...] = x_ref[...] + _dot(mixed.astype(BF16), wo_ref[...])


def _merge(oas, lses, yb, cv, u, x0, proj, x2d, expand, skip, bg, wa, wb, wc, wo, l):
    t = x2d.shape[0]
    tm = 512
    per_seq = l // tm
    row = lambda i: (i, 0)
    const = lambda i: (0, 0)
    half = pl.BlockSpec((tm, 512), row)
    lse_spec = pl.BlockSpec((1, HEADS_A, tm), lambda i: (i // per_seq, 0, i % per_seq))
    return pl.pallas_call(
        _merge_kernel,
        grid=(t // tm,),
        in_specs=[
            half, half, half,
            lse_spec, lse_spec, lse_spec,
            half, half, half, half,
            pl.BlockSpec((tm, COLS_G), lambda i: (i, OFF_G // COLS_G)),
            pl.BlockSpec((tm, D_MODEL), row),
            pl.BlockSpec((LANE, WIDTH_A), const),
            pl.BlockSpec((1, WIDTH_C), const), pl.BlockSpec((1, COLS_G), const),
            pl.BlockSpec((WIDTH_A, D_MODEL), const), pl.BlockSpec((WIDTH_B, D_MODEL), const),
            pl.BlockSpec((WIDTH_C, D_MODEL), const), pl.BlockSpec((D_MODEL, D_MODEL), const),
        ],
        out_specs=pl.BlockSpec((tm, D_MODEL), row),
        out_shape=jax.ShapeDtypeStruct((t, D_MODEL), F32),
        compiler_params=_cparams(("parallel",)),
        name="branch_merge",
    )(*oas, *lses, yb, cv, u, x0, proj, x2d, expand, skip, bg, wa, wb, wc, wo)


def _ffn_up_kernel(x_ref, prev_ref, next_ref, g_ref, wa_ref, wg_ref, cwa_ref, cwg_ref, cba_ref, cbg_ref, o_ref,
                   h_ref, *, per_seq, halo):
    tm = x_ref.shape[0]

    @pl.when(pl.program_id(1) == 0)
    def _():
        i = pl.program_id(0) % per_seq

        def norm(x):
            return x * lax.rsqrt(jnp.mean(x * x, axis=-1, keepdims=True) + EPS) * g_ref[...]

        h_ref[0:halo, :] = jnp.where(i == 0, 0.0, norm(prev_ref[...])).astype(BF16)
        h_ref[halo:halo + tm, :] = norm(x_ref[...]).astype(BF16)
        h_ref[halo + tm:, :] = jnp.where(i == per_seq - 1, 0.0, norm(next_ref[...])).astype(BF16)

    h = h_ref[...]
    n = h.shape[0]

    def conv(y, cw_ref, cb_ref):
        ym = pltpu.roll(y, 1, 0)[halo:halo + tm]
        yp = pltpu.roll(y, n - 1, 0)[halo:halo + tm]
        return ym * cw_ref[0:1, :] + y[halo:halo + tm] * cw_ref[1:2, :] + yp * cw_ref[2:3, :] + cb_ref[...]

    ya = _dot(h, wa_ref[...])
    yg = _dot(h, wg_ref[...])
    a = conv(ya, cwa_ref, cba_ref)
    g = conv(yg, cwg_ref, cbg_ref)
    o_ref[...] = (jax.nn.gelu(a) * g).astype(o_ref.dtype)


def _ffn_up(x2d, gain, wa, wg, cwa, cwg, cba, cbg, l):
    t = x2d.shape[0]
    tm, tn, halo = 512, D_FF // 2, 16
    per_seq = l // tm
    nh = tm // halo
    const = lambda i, j: (0, 0)
    col = lambda i, j: (0, j)
    return pl.pallas_call(
        functools.partial(_ffn_up_kernel, per_seq=per_seq, halo=halo),
        grid=(t // tm, D_FF // tn),
        in_specs=[
            pl.BlockSpec((tm, D_MODEL), lambda i, j: (i, 0)),
            pl.BlockSpec((halo, D_MODEL), lambda i, j: (jnp.maximum(i * nh - 1, 0), 0)),
            pl.BlockSpec((halo, D_MODEL), lambda i, j: (jnp.minimum((i + 1) * nh, t // halo - 1), 0)),
            pl.BlockSpec((1, D_MODEL), const),
            pl.BlockSpec((D_MODEL, tn), col), pl.BlockSpec((D_MODEL, tn), col),
            pl.BlockSpec((3, tn), col), pl.BlockSpec((3, tn), col),
            pl.BlockSpec((1, tn), col), pl.BlockSpec((1, tn), col),
        ],
        out_specs=pl.BlockSpec((tm, tn), lambda i, j: (i, j)),
        out_shape=jax.ShapeDtypeStruct((t, D_FF), BF16),
        scratch_shapes=[pltpu.VMEM((tm + 2 * halo, D_MODEL), BF16)],
        compiler_params=_cparams(("parallel", "arbitrary")),
        name="ffn_up",
    )(x2d, x2d, x2d, gain, wa, wg, cwa, cwg, cba, cbg)


def _ffn_down_kernel(x_ref, a_ref, w_ref, o_ref):
    o_ref[...] = x_ref[...] + _dot(a_ref[...], w_ref[...])


def _ffn_down(x2d, act, w):
    t = x2d.shape[0]
    tm = 512
    return pl.pallas_call(
        _ffn_down_kernel,
        grid=(t // tm,),
        in_specs=[
            pl.BlockSpec((tm, D_MODEL), lambda i: (i, 0)),
            pl.BlockSpec((tm, D_FF), lambda i: (i, 0)),
            pl.BlockSpec((D_FF, D_MODEL), lambda i: (0, 0)),
        ],
        out_specs=pl.BlockSpec((tm, D_MODEL), lambda i: (i, 0)),
        out_shape=jax.ShapeDtypeStruct((t, D_MODEL), F32),
        compiler_params=_cparams(("parallel",)),
        name="ffn_down",
    )(x2d, act, w)


def _layer_params(p, l):
    w_in = p['w_in'][l]
    a_end, b_end, c_end = COLS_A, COLS_A + COLS_B, COLS_A + COLS_B + COLS_C
    w_r = jnp.concatenate([w_in[:, c_end:], w_in[:, :a_end], w_in[:, a_end:b_end],
                           jnp.zeros((D_MODEL, PROJ_W - OFF_B - COLS_B), F32)], axis=1).astype(BF16)
    w_c = w_in[:, b_end:c_end].astype(BF16)
    slot_pad = SLOT_B - QK_B

    def slots(w):
        return jnp.pad(w, [(0, 0)] * (w.ndim - 1) + [(0, slot_pad)]).reshape(*w.shape[:-2], HEADS_B * SLOT_B)

    wq = slots(p['b_w_uq'][l].reshape(Q_RANK, HEADS_B, QK_B)).astype(BF16)
    wkv = p['b_w_ukv'][l].reshape(KV_RANK, HEADS_B, NOPE_B + V_B)
    wk_nope = jnp.pad(wkv[:, :, :NOPE_B], ((0, 0), (0, 0), (0, SLOT_B - NOPE_B))).reshape(KV_RANK, -1)
    place = jnp.pad(jnp.eye(ROPE_B, dtype=F32), ((0, 0), (NOPE_B, SLOT_B - QK_B)))
    wk_pe = jnp.tile(place, (1, HEADS_B))
    wk = jnp.concatenate([wk_nope, wk_pe, jnp.zeros((256 - KV_RANK - ROPE_B, HEADS_B * SLOT_B), F32)], axis=0)
    wv = wkv[:, :, NOPE_B:].reshape(KV_RANK, WIDTH_B)
    gq = jnp.tile(jnp.pad(p['b_q_g'][l], (0, slot_pad)), HEADS_B)[None] * (QK_B ** -0.5 * LOG2E)
    gk = jnp.tile(jnp.pad(p['b_k_g'][l], (0, slot_pad)), HEADS_B)[None]
    w_up = p['w_up'][l]
    cw = p['ffn_conv_w'][l]
    cb = p['ffn_conv_b'][l]
    return dict(
        norm_attn_g=p['norm_attn_g'][l][None], w_r=w_r, w_c=w_c,
        a_qg=[jnp.tile(p['a_q_g'][l, g], 4)[None] * (HEAD_DIM_A ** -0.5 * LOG2E) for g in range(3)],
        a_kg=[jnp.tile(p['a_k_g'][l, g], 4)[None] for g in range(3)],
        gql=p['b_q_lat_g'][l][None], gkl=p['b_kv_lat_g'][l][None],
        wq=wq, wk=wk.astype(BF16), wv=wv.astype(BF16), gq=gq, gk=gk,
        c_conv_w=p['c_conv_w'][l], c_conv_b=p['c_conv_b'][l][None],
        w1t=p['c_w1'][l][0:1].T, w1c=p['c_w1'][l][1:17].T, w1s=-p['c_w1'][l][17:33].T,
        b1=p['c_b1'][l][:, None], w2t=p['c_w2'][l].T, b2=p['c_b2'][l][:, None],
        w3t=p['c_w3'][l].T.reshape(2, WIDTH_C, HYENA_HID), fr=p['c_freq'][l][:, None],
        skip=p['c_skip'][l][None], bg=p['b_gate'][l][None],
        wa=p['w_br_a'][l].astype(BF16), wb=p['w_br_b'][l].astype(BF16), wc=p['w_br_c'][l].astype(BF16),
        wo=p['w_out'][l].astype(BF16), norm_ffn_g=p['norm_ffn_g'][l][None],
        w_up_a=w_up[:, :D_FF].astype(BF16), w_up_g=w_up[:, D_FF:].astype(BF16),
        cwa=cw[:, :D_FF], cwg=cw[:, D_FF:], cba=cb[None, :D_FF], cbg=cb[None, D_FF:],
        w_down=p['w_down'][l].astype(BF16),
    )


def _seq_constants(l):
    half = ROPE_B // 2
    pos = jnp.arange(l, dtype=F32)
    inv = ROPE_THETA ** (-jnp.arange(half, dtype=F32) / half)
    ang = pos[:, None] * inv[None, :]
    cos, sin = jnp.cos(ang), jnp.sin(ang)
    one = jnp.ones((l, NOPE_B), F32)
    zn = jnp.zeros((l, NOPE_B), F32)
    zh = jnp.zeros((l, half), F32)
    zp = jnp.zeros((l, SLOT_B - QK_B), F32)
    rc = jnp.concatenate([one, cos, cos, zp], axis=1)
    rs = jnp.concatenate([zn, -sin, sin, zp], axis=1)
    m = jnp.arange(2 * l)
    p_idx = jnp.where(m < l, m, 2 * l - m)
    tlin = jnp.linspace(0.0, 1.0, l, dtype=F32)
    tt = tlin[jnp.minimum(p_idx, l - 1)]
    angf = (2.0 * math.pi / l) * p_idx.astype(F32)
    valid = (m != l).astype(F32)
    bands = (HYENA_EMB - 1) // 2
    fcol = jnp.linspace(1e-4, bands - 1, bands, dtype=F32)[:, None]
    deltas = jnp.abs(jnp.linspace(math.log(HYENA_TARGET) / HYENA_FAST_DECAY,
                                  math.log(HYENA_TARGET) / HYENA_SLOW_DECAY, WIDTH_C, dtype=F32))[:, None]
    return dict(rc=rc, rs=rs, pos=jnp.stack([tt, angf, valid]), fcol=fcol, deltas=deltas)


def _static_tables():
    expand = (jnp.arange(LANE)[:, None] == (jnp.arange(WIDTH_A) // HEAD_DIM_A)[None, :]).astype(BF16)
    slopes = jnp.exp2(-ALIBI_MAX * (jnp.arange(HEADS_A, dtype=F32) + 1.0) / HEADS_A)
    return expand, slopes


def _layer(x2d, lp, sc, expand, slopes, b, l):
    proj = _inproj(x2d, lp['norm_attn_g'], lp['w_r'])
    oas, lses = [], []
    for g, dil in enumerate(DILATIONS):
        o, lse = _mixer_a_group(proj, lp['a_qg'][g], lp['a_kg'][g], -slopes * (dil * LOG2E), b, l, g, dil)
        oas.append(o)
        lses.append(lse)
    qp, kp, vt = _mla_prep(proj, lp['gql'], lp['gkl'], lp['wq'], lp['wk'], lp['wv'], lp['gq'], lp['gk'],
                           sc['rc'], sc['rs'], b, l)
    yb = _mla_attn(qp, kp, vt, b, l)
    u, x0 = _hyena_proj(x2d, lp['norm_attn_g'], lp['w_c'], lp['c_conv_w'], lp['c_conv_b'], l)
    kf = _hyena_filter(sc['pos'], sc['fcol'], lp['w1t'], lp['w1c'], lp['w1s'], lp['b1'], lp['w2t'], lp['b2'],
                       lp['w3t'], lp['fr'], sc['deltas'], l)
    nb = l // TOEP
    u_t = jnp.transpose(u.reshape(b, nb, TOEP, WIDTH_C), (3, 1, 0, 2)).reshape(WIDTH_C, nb * b, TOEP)
    cv_t = _hyena_conv(kf, u_t, nb, b)
    cv = jnp.transpose(cv_t.reshape(WIDTH_C, nb, b, TOEP), (2, 1, 3, 0)).reshape(b * l, WIDTH_C)
    xm = _merge(oas, lses, yb, cv, u, x0, proj, x2d, expand, lp['skip'], lp['bg'],
                lp['wa'], lp['wb'], lp['wc'], lp['wo'], l)
    act = _ffn_up(xm, lp['norm_ffn_g'], lp['w_up_a'], lp['w_up_g'], lp['cwa'], lp['cwg'], lp['cba'], lp['cbg'], l)
    return _ffn_down(xm, act, lp['w_down'])


def _trunk(x, layer_params, expand, slopes):
    b, l, _ = x.shape
    sc = _seq_constants(l)
    y = x.reshape(b * l, D_MODEL)
    for lp in layer_params:
        y = _layer(y, lp, sc, expand, slopes, b, l)
    return y.reshape(b, l, D_MODEL)


def kernel(x_prompt, x_sample, norm_attn_g, w_in, b_gate, a_q_g, a_k_g, b_q_lat_g, b_kv_lat_g, b_w_uq, b_w_ukv, b_q_g, b_k_g, c_conv_w, c_conv_b, c_w1, c_b1, c_w2, c_b2, c_w3, c_freq, c_skip, w_br_a, w_br_b, w_br_c, w_out, norm_ffn_g, w_up, ffn_conv_w, ffn_conv_b, w_down):
    p = dict(norm_attn_g=norm_attn_g, w_in=w_in, b_gate=b_gate, a_q_g=a_q_g, a_k_g=a_k_g, b_q_lat_g=b_q_lat_g,
             b_kv_lat_g=b_kv_lat_g, b_w_uq=b_w_uq, b_w_ukv=b_w_ukv, b_q_g=b_q_g, b_k_g=b_k_g, c_conv_w=c_conv_w,
             c_conv_b=c_conv_b, c_w1=c_w1, c_b1=c_b1, c_w2=c_w2, c_b2=c_b2, c_w3=c_w3, c_freq=c_freq,
             c_skip=c_skip, w_br_a=w_br_a, w_br_b=w_br_b, w_br_c=w_br_c, w_out=w_out, norm_ffn_g=norm_ffn_g,
             w_up=w_up, ffn_conv_w=ffn_conv_w, ffn_conv_b=ffn_conv_b, w_down=w_down)
    layer_params = [_layer_params(p, l) for l in range(DEPTH)]
    expand, slopes = _static_tables()
    return _trunk(x_prompt, layer_params, expand, slopes), _trunk(x_sample, layer_params, expand, slopes)
```

```python
import functools
import math

import jax
import jax.numpy as jnp
from jax import lax
from jax.experimental import pallas as pl
from jax.experimental.pallas import tpu as pltpu

F32 = jnp.float32
BF16 = jnp.bfloat16

D_MODEL = 1024
DEPTH = 2
EPS = 1e-6
HEADS_A = 8
HEAD_DIM_A = 64
DILATIONS = (1, 4, 16)
HALF_A = 64
WIDTH_A = HEADS_A * HEAD_DIM_A
ALIBI_MAX = 8.0
HEADS_B = 8
NOPE_B = 64
ROPE_B = 32
QK_B = NOPE_B + ROPE_B
V_B = 64
Q_RANK = 256
KV_RANK = 128
ROPE_THETA = 10000.0
WIDTH_B = HEADS_B * V_B
WIDTH_C = 512
HYENA_EMB = 33
HYENA_HID = 64
HYENA_FAST_DECAY = 0.3
HYENA_SLOW_DECAY = 1.5
HYENA_TARGET = 1e-2
D_FF = 2816
COLS_A = 3 * 3 * WIDTH_A
COLS_B = Q_RANK + KV_RANK + ROPE_B
COLS_C = 3 * WIDTH_C
COLS_G = 3 * D_MODEL
OFF_G = 0
OFF_A = COLS_G
OFF_B = OFF_A + COLS_A
PROJ_W = OFF_B + 512
LANE = 128
SLOT_B = 128
VROWS = 80
TOEP = 256
LOG2E = math.log2(math.e)
LN2 = math.log(2.0)
GELU_K1 = -2.0 * math.sqrt(2.0 / math.pi) * LOG2E
GELU_K3 = 0.044715 * GELU_K1
MASKED = 1e32
VMEM_LIMIT = 48 * 1024 * 1024


def _cparams(sem):
    return pltpu.CompilerParams(dimension_semantics=sem, vmem_limit_bytes=VMEM_LIMIT)


def _nt_dot(a, b):
    return lax.dot_general(a, b, (((1,), (1,)), ((), ())), preferred_element_type=F32)


def _dot(a, b):
    return jnp.dot(a, b, preferred_element_type=F32)


def _dot_exact(a, b):
    return jnp.dot(a, b, preferred_element_type=F32, precision=lax.Precision.HIGHEST)


def _inproj_kernel(x_ref, g_ref, w_ref, o_ref, h_ref):
    @pl.when(pl.program_id(1) == 0)
    def _():
        x = x_ref[...]
        ms = jnp.mean(x * x, axis=-1, keepdims=True)
        h_ref[...] = (x * lax.rsqrt(ms + EPS) * g_ref[...]).astype(BF16)

    o_ref[...] = _dot(h_ref[...], w_ref[...]).astype(o_ref.dtype)


def _inproj(x2d, gain, w_r):
    t = x2d.shape[0]
    tm, tn = 1024, PROJ_W // 4
    return pl.pallas_call(
        _inproj_kernel,
        grid=(t // tm, PROJ_W // tn),
        in_specs=[
            pl.BlockSpec((tm, D_MODEL), lambda i, j: (i, 0)),
            pl.BlockSpec((1, D_MODEL), lambda i, j: (0, 0)),
            pl.BlockSpec((D_MODEL, tn), lambda i, j: (0, j)),
        ],
        out_specs=pl.BlockSpec((tm, tn), lambda i, j: (i, j)),
        out_shape=jax.ShapeDtypeStruct((t, PROJ_W), BF16),
        scratch_shapes=[pltpu.VMEM((tm, D_MODEL), BF16)],
        compiler_params=_cparams(("parallel", "arbitrary")),
        name="inproj",
    )(x2d, gain, w_r)


def _dil_attn_kernel(negc_ref, q_ref, k_ref, v_ref, qg_ref, kg_ref, o_ref, lse_ref,
                     qn_ref, kn_ref, vn_ref, stg_ref, stg2_ref, ost_ref, bias_ref, s_ref, p_ref, r_ref, *, l, dil):
    hb = pl.program_id(1)
    lu = l // dil
    cw = q_ref.shape[-1]
    win = min(2 * LANE, lu)
    nq = lu // LANE
    row = lax.broadcasted_iota(jnp.int32, (cw, cw), 0) // HEAD_DIM_A
    col = lax.broadcasted_iota(jnp.int32, (cw, cw), 1) // HEAD_DIM_A
    seg_mean = jnp.where(row == col, 1.0 / HEAD_DIM_A, 0.0).astype(BF16)

    chunk = 512
    per = chunk // dil
    step4 = 4

    def norm_body(c, carry):
        r0 = pl.multiple_of(c * chunk, chunk)
        for src, gain, dst in ((q_ref, qg_ref, qn_ref), (k_ref, kg_ref, kn_ref), (v_ref, None, vn_ref)):
            if gain is None and dil == 1:
                continue
            x = src[0, pl.ds(r0, chunk), :].astype(F32)
            if gain is not None:
                msq = _dot((x * x).astype(BF16), seg_mean)
                x = x * lax.rsqrt(msq + EPS) * gain[...]
            if dil == 1:
                dst[pl.ds(r0, chunk), :] = x.astype(BF16)
            else:
                for h in range(cw // LANE):
                    stg_ref[h] = x[:, h * LANE:(h + 1) * LANE]
                src = stg_ref
                if dil > step4:
                    sub = chunk // step4
                    for b in range(step4):
                        for h in range(cw // LANE):
                            stg2_ref[h, b * sub:(b + 1) * sub, :] = stg_ref[h, pl.ds(b, sub, stride=step4), :]
                    src = stg2_ref
                for r in range(dil):
                    d0 = pl.multiple_of(r * lu + c * per, per)
                    start = r if dil <= step4 else (r % step4) * (chunk // step4) + r // step4
                    for h in range(cw // LANE):
                        dst[pl.ds(d0, per), h * LANE:(h + 1) * LANE] = (
                            src[h, pl.ds(start, per, stride=min(dil, step4)), :].astype(BF16))
        return carry

    lax.fori_loop(0, l // chunk, norm_body, 0)

    lane_lo = lax.broadcasted_iota(jnp.int32, (1, LANE), 1) < HEAD_DIM_A
    npair = cw // LANE
    tiles_per_step = 2
    j_k = lax.broadcasted_iota(jnp.int32, (win, LANE), 0)
    i_q = lax.broadcasted_iota(jnp.int32, (win, LANE), 1)
    for variant in range(3):
        absrel = jnp.abs(j_k - i_q - variant * HALF_A).astype(F32)
        absrel = jnp.where(absrel <= float(HALF_A), absrel, MASKED)
        for pair in range(npair):
            bias_ref[variant * npair + pair] = jnp.concatenate(
                [absrel * negc_ref[4 * hb + 2 * pair + a] for a in range(2)], axis=1)
    row8 = lax.broadcasted_iota(jnp.int32, (8, LANE), 0)

    nsteps = l // (LANE * tiles_per_step)

    def coords(idx):
        r = idx // nq
        q0 = pl.multiple_of((idx % nq) * LANE, LANE)
        w0 = pl.multiple_of(jnp.clip(q0 - HALF_A, 0, lu - win), HALF_A)
        return r, q0, w0, pl.multiple_of(r * lu, LANE)

    def issue_scores(step, slot):
        for u in range(tiles_per_step):
            r, q0, w0, base = coords(step * tiles_per_step + u)
            q = qn_ref[pl.ds(base + q0, LANE), :]
            kw = kn_ref[pl.ds(base + w0, win), :]
            for pair in range(npair):
                qp = q[:, pair * LANE:(pair + 1) * LANE]
                zero = jnp.zeros_like(qp)
                qs = jnp.concatenate([jnp.where(lane_lo, qp, zero), jnp.where(lane_lo, zero, qp)], axis=0)
                s_ref[slot, u * npair + pair] = _nt_dot(kw[:, pair * LANE:(pair + 1) * LANE], qs)

    def softmax(step, slot):
        for u in range(tiles_per_step):
            idx = step * tiles_per_step + u
            r, q0, w0, base = coords(idx)
            variant = (q0 - w0) // HALF_A
            lse_tile = jnp.zeros((8, LANE), F32)
            for pair in range(npair):
                st = s_ref[slot, u * npair + pair] + bias_ref[variant * npair + pair]
                m = jnp.max(st, axis=0, keepdims=True)
                p = jnp.exp2(st - m)
                den = jnp.sum(p, axis=0, keepdims=True)
                p_ref[slot, u * npair + pair] = p.astype(BF16)
                r_ref[slot, u * npair + pair] = 1.0 / den
                lse = (m + jnp.log2(den)) * LN2
                for a in range(2):
                    lse_tile = jnp.where(row8 == 2 * pair + a, lse[:, a * LANE:(a + 1) * LANE], lse_tile)
            lse_ref[0, 0, idx] = lse_tile

    def issue_values(step, slot):
        for u in range(tiles_per_step):
            r, q0, w0, base = coords(step * tiles_per_step + u)
            vw = v_ref[0, pl.ds(w0, win), :] if dil == 1 else vn_ref[pl.ds(base + w0, win), :]
            outs = []
            for pair in range(npair):
                vt = vw[:, pair * LANE:(pair + 1) * LANE].T
                res = _dot(vt, p_ref[slot, u * npair + pair])
                rden = r_ref[slot, u * npair + pair]
                o_t = jnp.concatenate([res[:HEAD_DIM_A, :LANE] * rden[:, :LANE],
                                       res[HEAD_DIM_A:, LANE:] * rden[:, LANE:]], axis=0)
                outs.append(o_t.T)
            if dil == 1:
                o_ref[0, pl.ds(q0, LANE), :] = jnp.concatenate(outs, axis=1).astype(o_ref.dtype)
            else:
                for h, o_pair in enumerate(outs):
                    ost_ref[h, pl.ds(r + q0 * dil, LANE, stride=dil), :] = o_pair

    p_ref[...] = jnp.zeros_like(p_ref)
    r_ref[...] = jnp.zeros_like(r_ref)
    issue_scores(jnp.int32(0), 0)

    per_trip = 8

    def step_body(j, carry):
        for i in range(per_trip):
            k = per_trip * j + i
            issue_scores(jnp.minimum(k + 1, nsteps - 1), (i + 1) % 2)
            issue_values(jnp.maximum(k - 1, 0), (i + 1) % 2)
            softmax(k, i % 2)
        return carry

    lax.fori_loop(0, nsteps // per_trip, step_body, 0)
    issue_values(jnp.int32(nsteps - 1), 1)

    if dil != 1:
        def out_body(c, carry):
            r0 = pl.multiple_of(c * chunk, chunk)
            for h in range(cw // LANE):
                o_ref[0, pl.ds(r0, chunk), h * LANE:(h + 1) * LANE] = (
                    ost_ref[h, pl.ds(r0, chunk), :].astype(o_ref.dtype))
            return carry

        lax.fori_loop(0, l // chunk, out_body, 0)


def _mixer_a_group(proj, qg, kg, negc, b, l, g, dil):
    pv = proj.reshape(b, l, PROJ_W)
    cq, ck, cv = (OFF_A + (0 + g) * 512) // 256, (OFF_A + (3 + g) * 512) // 256, (OFF_A + (6 + g) * 512) // 256

    def in_spec(c0):
        return pl.BlockSpec((1, l, 256), lambda bi, hb: (bi, 0, c0 + hb))

    nt = l // LANE
    win = min(2 * LANE, l // dil)
    o, lse = pl.pallas_call(
        functools.partial(_dil_attn_kernel, l=l, dil=dil),
        grid=(b, 2),
        in_specs=[
            pl.BlockSpec(memory_space=pltpu.SMEM),
            in_spec(cq), in_spec(ck), in_spec(cv),
            pl.BlockSpec((1, 256), lambda bi, hb: (0, 0)),
            pl.BlockSpec((1, 256), lambda bi, hb: (0, 0)),
        ],
        out_specs=[
            pl.BlockSpec((1, l, 256), lambda bi, hb: (bi, 0, hb)),
            pl.BlockSpec((1, 1, nt, 8, LANE), lambda bi, hb: (bi, hb, 0, 0, 0)),
        ],
        out_shape=[
            jax.ShapeDtypeStruct((b, l, WIDTH_A), BF16),
            jax.ShapeDtypeStruct((b, 2, nt, 8, LANE), F32),
        ],
        scratch_shapes=[pltpu.VMEM((l, 256), BF16), pltpu.VMEM((l, 256), BF16),
                        pltpu.VMEM((l if dil != 1 else 16, 256), BF16),
                        pltpu.VMEM((2, 512 if dil != 1 else 8, LANE), F32),
                        pltpu.VMEM((2, 512 if dil > 4 else 8, LANE), F32),
                        pltpu.VMEM((2, l if dil != 1 else 8, LANE), F32),
                        pltpu.VMEM((6, win, 2 * LANE), F32),
                        pltpu.VMEM((2, 4, win, 2 * LANE), F32), pltpu.VMEM((2, 4, win, 2 * LANE), BF16),
                        pltpu.VMEM((2, 4, 1, 2 * LANE), F32)],
        compiler_params=_cparams(("parallel", "arbitrary")),
        name=f"dilated_attention_g{g}",
    )(negc, pv, pv, pv, qg, kg)
    lse = lse[:, :, :, :4, :].reshape(b, 2, dil, nt // dil, 4, LANE)
    lse = lse.transpose(0, 1, 4, 3, 5, 2).reshape(b, HEADS_A, l)
    return o.reshape(b * l, WIDTH_A), lse


def _mla_prep_kernel(pb_ref, gql_ref, gkl_ref, wq_ref, wk_ref, wv_ref, gq_ref, gk_ref, rc_ref, rs_ref,
                     sel_ref, exp_ref, qt_ref, k_ref, vt_ref):
    width = HEADS_B * SLOT_B
    c = pb_ref[...].astype(F32)
    cq = c[:, :Q_RANK]
    cqn = (cq * lax.rsqrt(jnp.mean(cq * cq, axis=-1, keepdims=True) + EPS) * gql_ref[...]).astype(BF16)
    ckv = c[:, Q_RANK:Q_RANK + KV_RANK]
    ckvn = (ckv * lax.rsqrt(jnp.mean(ckv * ckv, axis=-1, keepdims=True) + EPS) * gkl_ref[...]).astype(BF16)
    q2 = _dot(cqn, wq_ref[...])
    k2 = _dot(jnp.concatenate([ckvn, pb_ref[:, Q_RANK + KV_RANK:]], axis=1), wk_ref[...])
    v = _dot(ckvn, wv_ref[...])
    q, k = q2[:, :width], k2[:, :width]
    rc, rs = rc_ref[...], rs_ref[...]

    lane = lax.broadcasted_iota(jnp.int32, (1, LANE), 1)

    ssqs = [_dot((x * x).astype(BF16), sel_ref[...]) for x in (q, k)]
    packed = []
    for ssq in ssqs:
        r = lax.rsqrt(ssq * (1.0 / QK_B) + EPS)
        r_hi = r.astype(BF16)
        r_lo = (r - r_hi.astype(F32)).astype(BF16)
        packed.append(jnp.where(lane < HEADS_B, r_hi, r_lo))
    scales = [_dot(pk, exp_ref[...]) for pk in packed]

    qg = q2 * gq_ref[...]
    kg = k2 * gk_ref[...]
    for h in range(HEADS_B):
        sl = slice(h * SLOT_B, (h + 1) * SLOT_B)
        sw = slice(width + h * SLOT_B, width + (h + 1) * SLOT_B)
        qt_ref[0, sl, :] = ((qg[:, sl] * rc + qg[:, sw] * rs) * scales[0][:, sl]).T.astype(qt_ref.dtype)
        k_ref[:, sl] = ((kg[:, sl] * rc + kg[:, sw] * rs) * scales[1][:, sl]).astype(k_ref.dtype)
    vt_ref[0] = v.T.astype(vt_ref.dtype)


def _swap_rotary_halves(w):
    s = w.reshape(*w.shape[:-1], HEADS_B, SLOT_B)
    half = ROPE_B // 2
    s = jnp.concatenate([s[..., :NOPE_B], s[..., NOPE_B + half:QK_B], s[..., NOPE_B:NOPE_B + half], s[..., QK_B:]],
                        axis=-1)
    return s.reshape(w.shape)


def _mla_prep(proj, gql, gkl, wq, wk, wv, gq, gk, rc, rs, b, l):
    t = b * l
    tm = 512
    per_seq = l // tm
    const = lambda i: (0, 0)
    wq, wk, gq, gk = [jnp.concatenate([w, _swap_rotary_halves(w)], axis=1) for w in (wq, wk, gq, gk)]
    slot_of_lane = jnp.arange(HEADS_B * SLOT_B) // SLOT_B
    col = jnp.arange(LANE)
    sel = ((col[None, :] % HEADS_B == slot_of_lane[:, None]) & (col[None, :] < 2 * HEADS_B)).astype(BF16)
    spread = sel.T
    return pl.pallas_call(
        _mla_prep_kernel,
        grid=(t // tm,),
        in_specs=[
            pl.BlockSpec((tm, 512), lambda i: (i, OFF_B // 512)),
            pl.BlockSpec((1, Q_RANK), const), pl.BlockSpec((1, KV_RANK), const),
            pl.BlockSpec((Q_RANK, 2 * HEADS_B * SLOT_B), const),
            pl.BlockSpec((256, 2 * HEADS_B * SLOT_B), const),
            pl.BlockSpec((KV_RANK, WIDTH_B), const),
            pl.BlockSpec((1, 2 * HEADS_B * SLOT_B), const), pl.BlockSpec((1, 2 * HEADS_B * SLOT_B), const),
            pl.BlockSpec((tm, SLOT_B), lambda i: (i % per_seq, 0)),
            pl.BlockSpec((tm, SLOT_B), lambda i: (i % per_seq, 0)),
            pl.BlockSpec((HEADS_B * SLOT_B, LANE), const), pl.BlockSpec((LANE, HEADS_B * SLOT_B), const),
        ],
        out_specs=[
            pl.BlockSpec((1, HEADS_B * SLOT_B, tm), lambda i: (i // per_seq, 0, i % per_seq)),
            pl.BlockSpec((tm, HEADS_B * SLOT_B), lambda i: (i, 0)),
            pl.BlockSpec((1, WIDTH_B, tm), lambda i: (i // per_seq, 0, i % per_seq)),
        ],
        out_shape=[
            jax.ShapeDtypeStruct((b, HEADS_B * SLOT_B, l), BF16),
            jax.ShapeDtypeStruct((t, HEADS_B * SLOT_B), BF16),
            jax.ShapeDtypeStruct((b, WIDTH_B, l), BF16),
        ],
        compiler_params=_cparams(("parallel",)),
        name="latent_prep",
    )(proj, gql, gkl, wq, wk, wv, gq, gk, rc, rs, sel, spread)


def _mla_attn_kernel(qt_ref, k_ref, vt_ref, o_ref, vaug_ref, *, tk):
    nkv = vaug_ref.shape[1]
    tq = qt_ref.shape[2]

    @pl.when(pl.program_id(2) == 0)
    def _():
        ones_rows = jnp.where(lax.broadcasted_iota(jnp.int32, (VROWS - V_B, tk), 0) == 0, 1.0, 0.0).astype(BF16)
        for a in range(2):
            for j in range(nkv):
                vaug_ref[a, j, 0:V_B, :] = vt_ref[0, a * V_B:(a + 1) * V_B, j * tk:(j + 1) * tk]
                vaug_ref[a, j, V_B:VROWS, :] = ones_rows

    qw = 2 * LANE
    ntile = tq // qw

    def scores(c, j):
        return [_dot(k_ref[0, j * tk:(j + 1) * tk, a * SLOT_B:(a + 1) * SLOT_B],
                     qt_ref[0, a * SLOT_B:(a + 1) * SLOT_B, c * qw:(c + 1) * qw]) for a in range(2)]

    def values(j, ps):
        return [_dot(vaug_ref[a, j], ps[a]) for a in range(2)]

    st = scores(0, 0)
    for c in range(ntile):
        m = [jnp.full((1, qw), -jnp.inf, F32) for _ in range(2)]
        acc = [jnp.zeros((VROWS, qw), F32) for _ in range(2)]
        ps = None
        for j in range(nkv):
            if j + 1 < nkv:
                st_next = scores(c, j + 1)
            else:
                st_next = scores(c + 1, 0) if c + 1 < ntile else None
            pv = values(j - 1, ps) if j > 0 else None
            ps = []
            for a in range(2):
                mn = jnp.maximum(m[a], jnp.max(st[a], axis=0, keepdims=True))
                ps.append(jnp.exp2(st[a] - mn).astype(BF16))
                if pv is not None:
                    acc[a] = (acc[a] + pv[a]) * jnp.exp2(m[a] - mn)
                m[a] = mn
            st = st_next
        pv = values(nkv - 1, ps)
        outs = []
        for a in range(2):
            tot = acc[a] + pv[a]
            outs.append(tot[0:V_B] * (1.0 / tot[V_B:V_B + 1]))
        o_ref[0, c * qw:(c + 1) * qw, :] = jnp.concatenate(outs, axis=0).T.astype(o_ref.dtype)


def _mla_attn(qt, kp, vt, b, l):
    tq, tk = 2048, 256
    k3 = kp.reshape(b, l, HEADS_B * SLOT_B)
    out = pl.pallas_call(
        functools.partial(_mla_attn_kernel, tk=tk),
        grid=(b, HEADS_B // 2, l // tq),
        in_specs=[
            pl.BlockSpec((1, 2 * SLOT_B, tq), lambda bi, p, qi: (bi, p, qi)),
            pl.BlockSpec((1, l, 2 * SLOT_B), lambda bi, p, qi: (bi, 0, p)),
            pl.BlockSpec((1, 2 * V_B, l), lambda bi, p, qi: (bi, p, 0)),
        ],
        out_specs=pl.BlockSpec((1, tq, 2 * V_B), lambda bi, p, qi: (bi, qi, p)),
        out_shape=jax.ShapeDtypeStruct((b, l, WIDTH_B), BF16),
        scratch_shapes=[pltpu.VMEM((2, l // tk, VROWS, tk), BF16)],
        compiler_params=_cparams(("parallel", "parallel", "arbitrary")),
        name="latent_attention",
    )(qt, k3, vt)
    return out.reshape(b * l, WIDTH_B)


def _hyena_proj_kernel(x_ref, prev_ref, next_ref, g_ref, w_ref, cw_ref, cb_ref, u_ref, x0_ref, *, per_seq, halo):
    tm = x_ref.shape[0]
    i = pl.program_id(0) % per_seq

    def norm(x):
        return x * lax.rsqrt(jnp.mean(x * x, axis=-1, keepdims=True) + EPS) * g_ref[...]

    h = jnp.concatenate([jnp.where(i == 0, 0.0, norm(prev_ref[...])).astype(BF16),
                         norm(x_ref[...]).astype(BF16),
                         jnp.where(i == per_seq - 1, 0.0, norm(next_ref[...])).astype(BF16)], axis=0)
    n = h.shape[0]
    y = _dot(h, w_ref[...])
    ym = pltpu.roll(y, 1, 0)[halo:halo + tm]
    yp = pltpu.roll(y, n - 1, 0)[halo:halo + tm]
    c = ym * cw_ref[0:1, :] + y[halo:halo + tm] * cw_ref[1:2, :] + yp * cw_ref[2:3, :] + cb_ref[...]
    x0_ref[...] = c[:, :WIDTH_C].astype(x0_ref.dtype)
    u_ref[...] = (c[:, 2 * WIDTH_C:] * c[:, WIDTH_C:2 * WIDTH_C]).astype(u_ref.dtype)


def _hyena_proj(x2d, gain, w_c, conv_w, conv_b, l):
    t = x2d.shape[0]
    tm, halo = 512, 16
    per_seq = l // tm
    nh = tm // halo
    const = lambda i: (0, 0)
    return pl.pallas_call(
        functools.partial(_hyena_proj_kernel, per_seq=per_seq, halo=halo),
        grid=(t // tm,),
        in_specs=[
            pl.BlockSpec((tm, D_MODEL), lambda i: (i, 0)),
            pl.BlockSpec((halo, D_MODEL), lambda i: (jnp.maximum(i * nh - 1, 0), 0)),
            pl.BlockSpec((halo, D_MODEL), lambda i: (jnp.minimum((i + 1) * nh, t // halo - 1), 0)),
            pl.BlockSpec((1, D_MODEL), const),
            pl.BlockSpec((D_MODEL, COLS_C), const),
            pl.BlockSpec((3, COLS_C), const),
            pl.BlockSpec((1, COLS_C), const),
        ],
        out_specs=[pl.BlockSpec((tm, WIDTH_C), lambda i: (i, 0)), pl.BlockSpec((tm, WIDTH_C), lambda i: (i, 0))],
        out_shape=[jax.ShapeDtypeStruct((t, WIDTH_C), BF16), jax.ShapeDtypeStruct((t, WIDTH_C), BF16)],
        compiler_params=_cparams(("parallel",)),
        name="hyena_proj",
    )(x2d, x2d, x2d, gain, w_c, conv_w, conv_b)


def _hyena_filter_kernel(pos_ref, fcol_ref, w1t_ref, w1c_ref, w1s_ref, b1_ref, w2_ref, b2_ref, w3_ref, fr_ref,
                         dl_ref, o_ref):
    tt = pos_ref[0:1, :]
    ang = pos_ref[1:2, :]
    valid = pos_ref[2:3, :]
    arg = fcol_ref[...] * ang
    pre1 = w1t_ref[...] * tt + _dot_exact(w1c_ref[...], jnp.cos(arg)) + _dot_exact(w1s_ref[...], jnp.sin(arg))
    fr = fr_ref[...]
    hid = jnp.sin(fr * (pre1 + b1_ref[...]))
    hid = jnp.sin(fr * (_dot_exact(w2_ref[...], hid) + b2_ref[...]))
    filt = _dot_exact(w3_ref[0], hid)
    o_ref[...] = filt * jnp.exp(-dl_ref[...] * tt) * valid


def _hyena_filter(pos, fcol, w1t, w1c, w1s, b1, w2t, b2, w3t, fr, dl, l):
    n = 2048
    const = lambda j: (0, 0)
    return pl.pallas_call(
        _hyena_filter_kernel,
        grid=(2 * l // n,),
        in_specs=[
            pl.BlockSpec((3, n), lambda j: (0, j)),
            pl.BlockSpec((16, 1), const),
            pl.BlockSpec((HYENA_HID, 1), const), pl.BlockSpec((HYENA_HID, 16), const),
            pl.BlockSpec((HYENA_HID, 16), const), pl.BlockSpec((HYENA_HID, 1), const),
            pl.BlockSpec((HYENA_HID, HYENA_HID), const), pl.BlockSpec((HYENA_HID, 1), const),
            pl.BlockSpec((1, WIDTH_C, HYENA_HID), lambda j: (j // (l // n), 0, 0)),
            pl.BlockSpec((HYENA_HID, 1), const), pl.BlockSpec((WIDTH_C, 1), const),
        ],
        out_specs=pl.BlockSpec((WIDTH_C, n), lambda j: (0, j)),
        out_shape=jax.ShapeDtypeStruct((WIDTH_C, 2 * l), F32),
        compiler_params=_cparams(("parallel",)),
        name="hyena_filter",
    )(pos, fcol, w1t, w1c, w1s, b1, w2t, b2, w3t, fr, dl)


def _hyena_conv_kernel(kf_ref, u_ref, o_ref, uf_ref, acc_ref, *, nb, bsz):
    p = TOEP
    n2 = 2 * nb * p

    def channel(c, carry):
        krow = kf_ref[pl.ds(c, 1), :]
        packed_rows = bsz % 16 == 0
        if not packed_rows:
            uf_ref[...] = u_ref[c].astype(F32)
        acc_ref[...] = jnp.zeros_like(acc_ref)
        for d in range(-(nb - 1), nb):
            a0 = (d * p) % n2
            b0 = ((d - 1) * p) % n2
            seg = jnp.concatenate([krow[:, a0:a0 + p], krow[:, b0:b0 + p]], axis=1)
            rolled = pltpu.roll(jnp.broadcast_to(seg, (p, 2 * p)), 0, 1, stride=1, stride_axis=0)
            toep = rolled[:, :p].astype(BF16)
            rows = (nb - abs(d)) * bsz
            src = 0 if d >= 0 else -d * bsz
            dst = d * bsz if d >= 0 else 0
            lhs = u_ref[c, src:src + rows, :] if packed_rows else uf_ref[src:src + rows, :].astype(BF16)
            acc_ref[dst:dst + rows, :] += _dot(lhs, toep)
        o_ref[c] = acc_ref[...].astype(o_ref.dtype)
        return carry

    lax.fori_loop(0, kf_ref.shape[0], channel, 0)


def _hyena_conv(kf, u_t, nb, bsz):
    cblk = 8
    rows = nb * bsz
    return pl.pallas_call(
        functools.partial(_hyena_conv_kernel, nb=nb, bsz=bsz),
        grid=(WIDTH_C // cblk,),
        in_specs=[
            pl.BlockSpec((cblk, 2 * nb * TOEP), lambda i: (i, 0)),
            pl.BlockSpec((cblk, rows, TOEP), lambda i: (i, 0, 0)),
        ],
        out_specs=pl.BlockSpec((cblk, rows, TOEP), lambda i: (i, 0, 0)),
        out_shape=jax.ShapeDtypeStruct((WIDTH_C, rows, TOEP), BF16),
        scratch_shapes=[pltpu.VMEM((rows, TOEP), F32), pltpu.VMEM((rows, TOEP), F32)],
        compiler_params=_cparams(("parallel",)),
        name="hyena_conv",
    )(kf, u_t)


def _merge_kernel(oa0_ref, oa1_ref, oa2_ref, l0_ref, l1_ref, l2_ref, yb_ref, cv_ref, u_ref, x0_ref, pg_ref, x_ref,
                  ex_ref, skip_ref, bg_ref, wa_ref, wb_ref, wc_ref, wo_ref, o_ref):
    lses = [l0_ref[0], l1_ref[0], l2_ref[0]]
    mx = jnp.maximum(jnp.maximum(lses[0], lses[1]), lses[2])
    pad = jnp.zeros((LANE - HEADS_A, mx.shape[1]), F32)
    num = None
    den = None
    for lse, oa in zip(lses, (oa0_ref, oa1_ref, oa2_ref)):
        wt = jnp.concatenate([jnp.exp(lse - mx), pad], axis=0).T
        w = _dot(wt.astype(BF16), ex_ref[...])
        num = w * oa[...].astype(F32) if num is None else num + w * oa[...].astype(F32)
        den = w if den is None else den + w
    ya = (num / den).astype(BF16)
    u = u_ref[...].astype(F32)
    yc = (x0_ref[...].astype(F32) * (cv_ref[...].astype(F32) + skip_ref[...] * u)).astype(BF16)
    mixed = None
    for i, (y, w_ref) in enumerate(((ya, wa_ref), (yb_ref[...], wb_ref), (yc, wc_ref))):
        sl = slice(i * D_MODEL, (i + 1) * D_MODEL)
        gate = jax.nn.sigmoid(pg_ref[:, sl].astype(F32) + bg_ref[:, sl])
        term = gate * _dot(y, w_ref[...])
        mixed = term if mixed is None else mixed + term
    o_ref[...] = x_ref[...] + _dot(mixed.astype(BF16), wo_ref[...])


def _merge(oas, lses, yb, cv, u, x0, proj, x2d, expand, skip, bg, wa, wb, wc, wo, l):
    t = x2d.shape[0]
    tm = 512
    per_seq = l // tm
    row = lambda i: (i, 0)
    const = lambda i: (0, 0)
    half = pl.BlockSpec((tm, 512), row)
    lse_spec = pl.BlockSpec((1, HEADS_A, tm), lambda i: (i // per_seq, 0, i % per_seq))
    return pl.pallas_call(
        _merge_kernel,
        grid=(t // tm,),
        in_specs=[
            half, half, half,
            lse_spec, lse_spec, lse_spec,
            half, half, half, half,
            pl.BlockSpec((tm, COLS_G), lambda i: (i, OFF_G // COLS_G)),
            pl.BlockSpec((tm, D_MODEL), row),
            pl.BlockSpec((LANE, WIDTH_A), const),
            pl.BlockSpec((1, WIDTH_C), const), pl.BlockSpec((1, COLS_G), const),
            pl.BlockSpec((WIDTH_A, D_MODEL), const), pl.BlockSpec((WIDTH_B, D_MODEL), const),
            pl.BlockSpec((WIDTH_C, D_MODEL), const), pl.BlockSpec((D_MODEL, D_MODEL), const),
        ],
        out_specs=pl.BlockSpec((tm, D_MODEL), row),
        out_shape=jax.ShapeDtypeStruct((t, D_MODEL), F32),
        compiler_params=_cparams(("parallel",)),
        name="branch_merge",
    )(*oas, *lses, yb, cv, u, x0, proj, x2d, expand, skip, bg, wa, wb, wc, wo)


def _ffn_up_kernel(x_ref, prev_ref, next_ref, g_ref, wa_ref, wg_ref, cwa_ref, cwg_ref, cba_ref, cbg_ref, o_ref,
                   h_ref, *, per_seq, halo):
    tm = x_ref.shape[0]

    @pl.when(pl.program_id(1) == 0)
    def _():
        i = pl.program_id(0) % per_seq

        def norm(x):
            return x * lax.rsqrt(jnp.mean(x * x, axis=-1, keepdims=True) + EPS) * g_ref[...]

        h_ref[0:halo, :] = jnp.where(i == 0, 0.0, norm(prev_ref[...])).astype(BF16)
        h_ref[halo:halo + tm, :] = norm(x_ref[...]).astype(BF16)
        h_ref[halo + tm:, :] = jnp.where(i == per_seq - 1, 0.0, norm(next_ref[...])).astype(BF16)

    h = h_ref[...]
    n = h.shape[0]

    def conv(y, cw_ref, cb_ref):
        ym = pltpu.roll(y, 1, 0)[halo:halo + tm]
        yp = pltpu.roll(y, n - 1, 0)[halo:halo + tm]
        return ym * cw_ref[0:1, :] + y[halo:halo + tm] * cw_ref[1:2, :] + yp * cw_ref[2:3, :] + cb_ref[...]

    ya = _dot(h, wa_ref[...])
    yg = _dot(h, wg_ref[...])
    a = conv(ya, cwa_ref, cba_ref)
    g = conv(yg, cwg_ref, cbg_ref)
    e = jnp.exp2(a * (GELU_K1 + GELU_K3 * (a * a)))
    o_ref[...] = (a * g / (1.0 + e)).astype(o_ref.dtype)


def _ffn_up(x2d, gain, wa, wg, cwa, cwg, cba, cbg, l):
    t = x2d.shape[0]
    tm, tn, halo = 1024, D_FF // 2, 16
    per_seq = l // tm
    nh = tm // halo
    const = lambda i, j: (0, 0)
    col = lambda i, j: (0, j)
    return pl.pallas_call(
        functools.partial(_ffn_up_kernel, per_seq=per_seq, halo=halo),
        grid=(t // tm, D_FF // tn),
        in_specs=[
            pl.BlockSpec((tm, D_MODEL), lambda i, j: (i, 0)),
            pl.BlockSpec((halo, D_MODEL), lambda i, j: (jnp.maximum(i * nh - 1, 0), 0)),
            pl.BlockSpec((halo, D_MODEL), lambda i, j: (jnp.minimum((i + 1) * nh, t // halo - 1), 0)),
            pl.BlockSpec((1, D_MODEL), const),
            pl.BlockSpec((D_MODEL, tn), col), pl.BlockSpec((D_MODEL, tn), col),
            pl.BlockSpec((3, tn), col), pl.BlockSpec((3, tn), col),
            pl.BlockSpec((1, tn), col), pl.BlockSpec((1, tn), col),
        ],
        out_specs=pl.BlockSpec((tm, tn), lambda i, j: (i, j)),
        out_shape=jax.ShapeDtypeStruct((t, D_FF), BF16),
        scratch_shapes=[pltpu.VMEM((tm + 2 * halo, D_MODEL), BF16)],
        compiler_params=_cparams(("parallel", "arbitrary")),
        name="ffn_up",
    )(x2d, x2d, x2d, gain, wa, wg, cwa, cwg, cba, cbg)


def _ffn_down_kernel(x_ref, a_ref, w_ref, o_ref):
    o_ref[...] = x_ref[...] + _dot(a_ref[...], w_ref[...])


def _ffn_down(x2d, act, w):
    t = x2d.shape[0]
    tm = 512
    return pl.pallas_call(
        _ffn_down_kernel,
        grid=(t // tm,),
        in_specs=[
            pl.BlockSpec((tm, D_MODEL), lambda i: (i, 0)),
            pl.BlockSpec((tm, D_FF), lambda i: (i, 0)),
            pl.BlockSpec((D_FF, D_MODEL), lambda i: (0, 0)),
        ],
        out_specs=pl.BlockSpec((tm, D_MODEL), lambda i: (i, 0)),
        out_shape=jax.ShapeDtypeStruct((t, D_MODEL), F32),
        compiler_params=_cparams(("parallel",)),
        name="ffn_down",
    )(x2d, act, w)


def _layer_params(p, l):
    w_in = p['w_in'][l]
    a_end, b_end, c_end = COLS_A, COLS_A + COLS_B, COLS_A + COLS_B + COLS_C
    w_r = jnp.concatenate([w_in[:, c_end:], w_in[:, :a_end], w_in[:, a_end:b_end],
                           jnp.zeros((D_MODEL, PROJ_W - OFF_B - COLS_B), F32)], axis=1).astype(BF16)
    w_c = w_in[:, b_end:c_end].astype(BF16)
    slot_pad = SLOT_B - QK_B

    def slots(w):
        return jnp.pad(w, [(0, 0)] * (w.ndim - 1) + [(0, slot_pad)]).reshape(*w.shape[:-2], HEADS_B * SLOT_B)

    wq = slots(p['b_w_uq'][l].reshape(Q_RANK, HEADS_B, QK_B)).astype(BF16)
    wkv = p['b_w_ukv'][l].reshape(KV_RANK, HEADS_B, NOPE_B + V_B)
    wk_nope = jnp.pad(wkv[:, :, :NOPE_B], ((0, 0), (0, 0), (0, SLOT_B - NOPE_B))).reshape(KV_RANK, -1)
    place = jnp.pad(jnp.eye(ROPE_B, dtype=F32), ((0, 0), (NOPE_B, SLOT_B - QK_B)))
    wk_pe = jnp.tile(place, (1, HEADS_B))
    wk = jnp.concatenate([wk_nope, wk_pe, jnp.zeros((256 - KV_RANK - ROPE_B, HEADS_B * SLOT_B), F32)], axis=0)
    wv = wkv[:, :, NOPE_B:].reshape(KV_RANK, WIDTH_B)
    gq = jnp.tile(jnp.pad(p['b_q_g'][l], (0, slot_pad)), HEADS_B)[None] * (QK_B ** -0.5 * LOG2E)
    gk = jnp.tile(jnp.pad(p['b_k_g'][l], (0, slot_pad)), HEADS_B)[None]
    w_up = p['w_up'][l]
    cw = p['ffn_conv_w'][l]
    cb = p['ffn_conv_b'][l]
    return dict(
        norm_attn_g=p['norm_attn_g'][l][None], w_r=w_r, w_c=w_c,
        a_qg=[jnp.tile(p['a_q_g'][l, g], 4)[None] * (HEAD_DIM_A ** -0.5 * LOG2E) for g in range(3)],
        a_kg=[jnp.tile(p['a_k_g'][l, g], 4)[None] for g in range(3)],
        gql=p['b_q_lat_g'][l][None], gkl=p['b_kv_lat_g'][l][None],
        wq=wq, wk=wk.astype(BF16), wv=wv.astype(BF16), gq=gq, gk=gk,
        c_conv_w=p['c_conv_w'][l], c_conv_b=p['c_conv_b'][l][None],
        w1t=p['c_w1'][l][0:1].T, w1c=p['c_w1'][l][1:17].T, w1s=-p['c_w1'][l][17:33].T,
        b1=p['c_b1'][l][:, None], w2t=p['c_w2'][l].T, b2=p['c_b2'][l][:, None],
        w3t=p['c_w3'][l].T.reshape(2, WIDTH_C, HYENA_HID), fr=p['c_freq'][l][:, None],
        skip=p['c_skip'][l][None], bg=p['b_gate'][l][None],
        wa=p['w_br_a'][l].astype(BF16), wb=p['w_br_b'][l].astype(BF16), wc=p['w_br_c'][l].astype(BF16),
        wo=p['w_out'][l].astype(BF16), norm_ffn_g=p['norm_ffn_g'][l][None],
        w_up_a=w_up[:, :D_FF].astype(BF16), w_up_g=w_up[:, D_FF:].astype(BF16),
        cwa=cw[:, :D_FF], cwg=cw[:, D_FF:], cba=cb[None, :D_FF], cbg=cb[None, D_FF:],
        w_down=p['w_down'][l].astype(BF16),
    )


def _seq_constants(l):
    half = ROPE_B // 2
    pos = jnp.arange(l, dtype=F32)
    inv = ROPE_THETA ** (-jnp.arange(half, dtype=F32) / half)
    ang = pos[:, None] * inv[None, :]
    cos, sin = jnp.cos(ang), jnp.sin(ang)
    one = jnp.ones((l, NOPE_B), F32)
    zn = jnp.zeros((l, NOPE_B), F32)
    zh = jnp.zeros((l, half), F32)
    zp = jnp.zeros((l, SLOT_B - QK_B), F32)
    rc = jnp.concatenate([one, cos, cos, zp], axis=1)
    rs = jnp.concatenate([zn, -sin, sin, zp], axis=1)
    m = jnp.arange(2 * l)
    p_idx = jnp.where(m < l, m, 2 * l - m)
    tlin = jnp.linspace(0.0, 1.0, l, dtype=F32)
    tt = tlin[jnp.minimum(p_idx, l - 1)]
    angf = (2.0 * math.pi / l) * p_idx.astype(F32)
    valid = (m != l).astype(F32)
    bands = (HYENA_EMB - 1) // 2
    fcol = jnp.linspace(1e-4, bands - 1, bands, dtype=F32)[:, None]
    deltas = jnp.abs(jnp.linspace(math.log(HYENA_TARGET) / HYENA_FAST_DECAY,
                                  math.log(HYENA_TARGET) / HYENA_SLOW_DECAY, WIDTH_C, dtype=F32))[:, None]
    return dict(rc=rc, rs=rs, pos=jnp.stack([tt, angf, valid]), fcol=fcol, deltas=deltas)


def _static_tables():
    expand = (jnp.arange(LANE)[:, None] == (jnp.arange(WIDTH_A) // HEAD_DIM_A)[None, :]).astype(BF16)
    slopes = jnp.exp2(-ALIBI_MAX * (jnp.arange(HEADS_A, dtype=F32) + 1.0) / HEADS_A)
    return expand, slopes


def _layer(x2d, lp, sc, expand, slopes, b, l):
    proj = _inproj(x2d, lp['norm_attn_g'], lp['w_r'])
    oas, lses = [], []
    for g, dil in enumerate(DILATIONS):
        o, lse = _mixer_a_group(proj, lp['a_qg'][g], lp['a_kg'][g], -slopes * (dil * LOG2E), b, l, g, dil)
        oas.append(o)
        lses.append(lse)
    qp, kp, vt = _mla_prep(proj, lp['gql'], lp['gkl'], lp['wq'], lp['wk'], lp['wv'], lp['gq'], lp['gk'],
                           sc['rc'], sc['rs'], b, l)
    yb = _mla_attn(qp, kp, vt, b, l)
    u, x0 = _hyena_proj(x2d, lp['norm_attn_g'], lp['w_c'], lp['c_conv_w'], lp['c_conv_b'], l)
    kf = _hyena_filter(sc['pos'], sc['fcol'], lp['w1t'], lp['w1c'], lp['w1s'], lp['b1'], lp['w2t'], lp['b2'],
                       lp['w3t'], lp['fr'], sc['deltas'], l)
    nb = l // TOEP
    u_t = jnp.transpose(u.reshape(b, nb, TOEP, WIDTH_C), (3, 1, 0, 2)).reshape(WIDTH_C, nb * b, TOEP)
    cv_t = _hyena_conv(kf, u_t, nb, b)
    cv = jnp.transpose(cv_t.reshape(WIDTH_C, nb, b, TOEP), (2, 1, 3, 0)).reshape(b * l, WIDTH_C)
    xm = _merge(oas, lses, yb, cv, u, x0, proj, x2d, expand, lp['skip'], lp['bg'],
                lp['wa'], lp['wb'], lp['wc'], lp['wo'], l)
    act = _ffn_up(xm, lp['norm_ffn_g'], lp['w_up_a'], lp['w_up_g'], lp['cwa'], lp['cwg'], lp['cba'], lp['cbg'], l)
    return _ffn_down(xm, act, lp['w_down'])


def _trunk(x, layer_params, expand, slopes):
    b, l, _ = x.shape
    sc = _seq_constants(l)
    y = x.reshape(b * l, D_MODEL)
    for lp in layer_params:
        y = _layer(y, lp, sc, expand, slopes, b, l)
    return y.reshape(b, l, D_MODEL)


def kernel(x_prompt, x_sample, norm_attn_g, w_in, b_gate, a_q_g, a_k_g, b_q_lat_g, b_kv_lat_g, b_w_uq, b_w_ukv, b_q_g, b_k_g, c_conv_w, c_conv_b, c_w1, c_b1, c_w2, c_b2, c_w3, c_freq, c_skip, w_br_a, w_br_b, w_br_c, w_out, norm_ffn_g, w_up, ffn_conv_w, ffn_conv_b, w_down):
    p = dict(norm_attn_g=norm_attn_g, w_in=w_in, b_gate=b_gate, a_q_g=a_q_g, a_k_g=a_k_g, b_q_lat_g=b_q_lat_g,
             b_kv_lat_g=b_kv_lat_g, b_w_uq=b_w_uq, b_w_ukv=b_w_ukv, b_q_g=b_q_g, b_k_g=b_k_g, c_conv_w=c_conv_w,
             c_conv_b=c_conv_b, c_w1=c_w1, c_b1=c_b1, c_w2=c_w2, c_b2=c_b2, c_w3=c_w3, c_freq=c_freq,
             c_skip=c_skip, w_br_a=w_br_a, w_br_b=w_br_b, w_br_c=w_br_c, w_out=w_out, norm_ffn_g=norm_ffn_g,
             w_up=w_up, ffn_conv_w=ffn_conv_w, ffn_conv_b=ffn_conv_b, w_down=w_down)
    layer_params = [_layer_params(p, l) for l in range(DEPTH)]
    expand, slopes = _static_tables()
    return _trunk(x_prompt, layer_params, expand, slopes), _trunk(x_sample, layer_params, expand, slopes)
```

```python
import functools
import math

import jax
import jax.numpy as jnp
from jax import lax
from jax.experimental import pallas as pl
from jax.experimental.pallas import tpu as pltpu

F32 = jnp.float32
BF16 = jnp.bfloat16

D_MODEL = 1024
DEPTH = 2
EPS = 1e-6
HEADS_A = 8
HEAD_DIM_A = 64
DILATIONS = (1, 4, 16)
HALF_A = 64
WIDTH_A = HEADS_A * HEAD_DIM_A
ALIBI_MAX = 8.0
HEADS_B = 8
NOPE_B = 64
ROPE_B = 32
QK_B = NOPE_B + ROPE_B
V_B = 64
Q_RANK = 256
KV_RANK = 128
ROPE_THETA = 10000.0
WIDTH_B = HEADS_B * V_B
WIDTH_C = 512
HYENA_EMB = 33
HYENA_HID = 64
HYENA_FAST_DECAY = 0.3
HYENA_SLOW_DECAY = 1.5
HYENA_TARGET = 1e-2
D_FF = 2816
COLS_A = 3 * 3 * WIDTH_A
COLS_B = Q_RANK + KV_RANK + ROPE_B
COLS_C = 3 * WIDTH_C
COLS_G = 3 * D_MODEL
OFF_G = 0
OFF_A = COLS_G
OFF_B = OFF_A + COLS_A
PROJ_W = OFF_B + 512
LANE = 128
SLOT_B = 128
VROWS = 80
TOEP = 256
LOG2E = math.log2(math.e)
LN2 = math.log(2.0)
GELU_K1 = -2.0 * math.sqrt(2.0 / math.pi) * LOG2E
GELU_K3 = 0.044715 * GELU_K1
MASKED = 1e32
VMEM_LIMIT = 48 * 1024 * 1024


def _cparams(sem):
    return pltpu.CompilerParams(dimension_semantics=sem, vmem_limit_bytes=VMEM_LIMIT)


def _nt_dot(a, b):
    return lax.dot_general(a, b, (((1,), (1,)), ((), ())), preferred_element_type=F32)


def _dot(a, b):
    return jnp.dot(a, b, preferred_element_type=F32)


def _dot_exact(a, b):
    return jnp.dot(a, b, preferred_element_type=F32, precision=lax.Precision.HIGHEST)


def _inproj_kernel(x_ref, g_ref, w_ref, o_ref, h_ref):
    @pl.when(pl.program_id(1) == 0)
    def _():
        x = x_ref[...]
        ms = jnp.mean(x * x, axis=-1, keepdims=True)
        h_ref[...] = (x * lax.rsqrt(ms + EPS) * g_ref[...]).astype(BF16)

    o_ref[...] = _dot(h_ref[...], w_ref[...]).astype(o_ref.dtype)


def _inproj(x2d, gain, w_r):
    t = x2d.shape[0]
    tm, tn = 1024, PROJ_W // 4
    return pl.pallas_call(
        _inproj_kernel,
        grid=(t // tm, PROJ_W // tn),
        in_specs=[
            pl.BlockSpec((tm, D_MODEL), lambda i, j: (i, 0)),
            pl.BlockSpec((1, D_MODEL), lambda i, j: (0, 0)),
            pl.BlockSpec((D_MODEL, tn), lambda i, j: (0, j)),
        ],
        out_specs=pl.BlockSpec((tm, tn), lambda i, j: (i, j)),
        out_shape=jax.ShapeDtypeStruct((t, PROJ_W), BF16),
        scratch_shapes=[pltpu.VMEM((tm, D_MODEL), BF16)],
        compiler_params=_cparams(("parallel", "arbitrary")),
        name="inproj",
    )(x2d, gain, w_r)


def _dil_attn_kernel(negc_ref, q_ref, k_ref, v_ref, qg_ref, kg_ref, o_ref, lse_ref,
                     qn_ref, kn_ref, vn_ref, stg_ref, stg2_ref, ost_ref, bias_ref, s_ref, p_ref, r_ref, *, l, dil):
    hb = pl.program_id(1)
    lu = l // dil
    cw = q_ref.shape[-1]
    win = min(2 * LANE, lu)
    nq = lu // LANE
    row = lax.broadcasted_iota(jnp.int32, (cw, cw), 0) // HEAD_DIM_A
    col = lax.broadcasted_iota(jnp.int32, (cw, cw), 1) // HEAD_DIM_A
    seg_mean = jnp.where(row == col, 1.0 / HEAD_DIM_A, 0.0).astype(BF16)

    chunk = 512
    per = chunk // dil
    step4 = 4

    def norm_body(c, carry):
        r0 = pl.multiple_of(c * chunk, chunk)
        for src, gain, dst in ((q_ref, qg_ref, qn_ref), (k_ref, kg_ref, kn_ref), (v_ref, None, vn_ref)):
            if gain is None and dil == 1:
                continue
            x = src[0, pl.ds(r0, chunk), :].astype(F32)
            if gain is not None:
                msq = _dot((x * x).astype(BF16), seg_mean)
                x = x * lax.rsqrt(msq + EPS) * gain[...]
            if dil == 1:
                dst[pl.ds(r0, chunk), :] = x.astype(BF16)
            else:
                for h in range(cw // LANE):
                    stg_ref[h] = x[:, h * LANE:(h + 1) * LANE]
                src = stg_ref
                if dil > step4:
                    sub = chunk // step4
                    for b in range(step4):
                        for h in range(cw // LANE):
                            stg2_ref[h, b * sub:(b + 1) * sub, :] = stg_ref[h, pl.ds(b, sub, stride=step4), :]
                    src = stg2_ref
                for r in range(dil):
                    d0 = pl.multiple_of(r * lu + c * per, per)
                    start = r if dil <= step4 else (r % step4) * (chunk // step4) + r // step4
                    for h in range(cw // LANE):
                        dst[pl.ds(d0, per), h * LANE:(h + 1) * LANE] = (
                            src[h, pl.ds(start, per, stride=min(dil, step4)), :].astype(BF16))
        return carry

    lax.fori_loop(0, l // chunk, norm_body, 0)

    lane_lo = lax.broadcasted_iota(jnp.int32, (1, LANE), 1) < HEAD_DIM_A
    npair = cw // LANE
    tiles_per_step = 2
    j_k = lax.broadcasted_iota(jnp.int32, (win, LANE), 0)
    i_q = lax.broadcasted_iota(jnp.int32, (win, LANE), 1)
    for variant in range(3):
        absrel = jnp.abs(j_k - i_q - variant * HALF_A).astype(F32)
        absrel = jnp.where(absrel <= float(HALF_A), absrel, MASKED)
        for pair in range(npair):
            bias_ref[variant * npair + pair] = jnp.concatenate(
                [absrel * negc_ref[4 * hb + 2 * pair + a] for a in range(2)], axis=1)
    row8 = lax.broadcasted_iota(jnp.int32, (8, LANE), 0)

    nsteps = l // (LANE * tiles_per_step)

    def coords(idx):
        r = idx // nq
        q0 = pl.multiple_of((idx % nq) * LANE, LANE)
        w0 = pl.multiple_of(jnp.clip(q0 - HALF_A, 0, lu - win), HALF_A)
        return r, q0, w0, pl.multiple_of(r * lu, LANE)

    def issue_scores(step, slot):
        for u in range(tiles_per_step):
            r, q0, w0, base = coords(step * tiles_per_step + u)
            q = qn_ref[pl.ds(base + q0, LANE), :]
            kw = kn_ref[pl.ds(base + w0, win), :]
            for pair in range(npair):
                qp = q[:, pair * LANE:(pair + 1) * LANE]
                zero = jnp.zeros_like(qp)
                qs = jnp.concatenate([jnp.where(lane_lo, qp, zero), jnp.where(lane_lo, zero, qp)], axis=0)
                s_ref[slot, u * npair + pair] = _nt_dot(kw[:, pair * LANE:(pair + 1) * LANE], qs)

    def softmax(step, slot):
        for u in range(tiles_per_step):
            idx = step * tiles_per_step + u
            r, q0, w0, base = coords(idx)
            variant = (q0 - w0) // HALF_A
            lse_tile = jnp.zeros((8, LANE), F32)
            for pair in range(npair):
                st = s_ref[slot, u * npair + pair] + bias_ref[variant * npair + pair]
                m = jnp.max(st, axis=0, keepdims=True)
                p = jnp.exp2(st - m)
                den = jnp.sum(p, axis=0, keepdims=True)
                p_ref[slot, u * npair + pair] = p.astype(BF16)
                r_ref[slot, u * npair + pair] = 1.0 / den
                lse = (m + jnp.log2(den)) * LN2
                for a in range(2):
                    lse_tile = jnp.where(row8 == 2 * pair + a, lse[:, a * LANE:(a + 1) * LANE], lse_tile)
            lse_ref[0, 0, idx] = lse_tile

    def issue_values(step, slot):
        for u in range(tiles_per_step):
            r, q0, w0, base = coords(step * tiles_per_step + u)
            vw = v_ref[0, pl.ds(w0, win), :] if dil == 1 else vn_ref[pl.ds(base + w0, win), :]
            outs = []
            for pair in range(npair):
                vt = vw[:, pair * LANE:(pair + 1) * LANE].T
                res = _dot(vt, p_ref[slot, u * npair + pair])
                rden = r_ref[slot, u * npair + pair]
                o_t = jnp.concatenate([res[:HEAD_DIM_A, :LANE] * rden[:, :LANE],
                                       res[HEAD_DIM_A:, LANE:] * rden[:, LANE:]], axis=0)
                outs.append(o_t.T)
            if dil == 1:
                o_ref[0, pl.ds(q0, LANE), :] = jnp.concatenate(outs, axis=1).astype(o_ref.dtype)
            else:
                for h, o_pair in enumerate(outs):
                    ost_ref[h, pl.ds(r + q0 * dil, LANE, stride=dil), :] = o_pair

    p_ref[...] = jnp.zeros_like(p_ref)
    r_ref[...] = jnp.zeros_like(r_ref)
    issue_scores(jnp.int32(0), 0)

    per_trip = 8

    def step_body(j, carry):
        for i in range(per_trip):
            k = per_trip * j + i
            issue_scores(jnp.minimum(k + 1, nsteps - 1), (i + 1) % 2)
            issue_values(jnp.maximum(k - 1, 0), (i + 1) % 2)
            softmax(k, i % 2)
        return carry

    lax.fori_loop(0, nsteps // per_trip, step_body, 0)
    issue_values(jnp.int32(nsteps - 1), 1)

    if dil != 1:
        def out_body(c, carry):
            r0 = pl.multiple_of(c * chunk, chunk)
            for h in range(cw // LANE):
                o_ref[0, pl.ds(r0, chunk), h * LANE:(h + 1) * LANE] = (
                    ost_ref[h, pl.ds(r0, chunk), :].astype(o_ref.dtype))
            return carry

        lax.fori_loop(0, l // chunk, out_body, 0)


def _mixer_a_group(proj, qg, kg, negc, b, l, g, dil):
    pv = proj.reshape(b, l, PROJ_W)
    cq, ck, cv = (OFF_A + (0 + g) * 512) // 256, (OFF_A + (3 + g) * 512) // 256, (OFF_A + (6 + g) * 512) // 256

    def in_spec(c0):
        return pl.BlockSpec((1, l, 256), lambda bi, hb: (bi, 0, c0 + hb))

    nt = l // LANE
    win = min(2 * LANE, l // dil)
    o, lse = pl.pallas_call(
        functools.partial(_dil_attn_kernel, l=l, dil=dil),
        grid=(b, 2),
        in_specs=[
            pl.BlockSpec(memory_space=pltpu.SMEM),
            in_spec(cq), in_spec(ck), in_spec(cv),
            pl.BlockSpec((1, 256), lambda bi, hb: (0, 0)),
            pl.BlockSpec((1, 256), lambda bi, hb: (0, 0)),
        ],
        out_specs=[
            pl.BlockSpec((1, l, 256), lambda bi, hb: (bi, 0, hb)),
            pl.BlockSpec((1, 1, nt, 8, LANE), lambda bi, hb: (bi, hb, 0, 0, 0)),
        ],
        out_shape=[
            jax.ShapeDtypeStruct((b, l, WIDTH_A), BF16),
            jax.ShapeDtypeStruct((b, 2, nt, 8, LANE), F32),
        ],
        scratch_shapes=[pltpu.VMEM((l, 256), BF16), pltpu.VMEM((l, 256), BF16),
                        pltpu.VMEM((l if dil != 1 else 16, 256), BF16),
                        pltpu.VMEM((2, 512 if dil != 1 else 8, LANE), F32),
                        pltpu.VMEM((2, 512 if dil > 4 else 8, LANE), F32),
                        pltpu.VMEM((2, l if dil != 1 else 8, LANE), F32),
                        pltpu.VMEM((6, win, 2 * LANE), F32),
                        pltpu.VMEM((2, 4, win, 2 * LANE), F32), pltpu.VMEM((2, 4, win, 2 * LANE), BF16),
                        pltpu.VMEM((2, 4, 1, 2 * LANE), F32)],
        compiler_params=_cparams(("parallel", "arbitrary")),
        name=f"dilated_attention_g{g}",
    )(negc, pv, pv, pv, qg, kg)
    lse = lse[:, :, :, :4, :].reshape(b, 2, dil, nt // dil, 4, LANE)
    lse = lse.transpose(0, 1, 4, 3, 5, 2).reshape(b, HEADS_A, l)
    return o.reshape(b * l, WIDTH_A), lse


def _mla_prep_kernel(pb_ref, gql_ref, gkl_ref, wq_ref, wk_ref, wv_ref, gq_ref, gk_ref, rc_ref, rs_ref,
                     sel_ref, exp_ref, qt_ref, k_ref, vt_ref):
    width = HEADS_B * SLOT_B
    c = pb_ref[...].astype(F32)
    cq = c[:, :Q_RANK]
    cqn = (cq * lax.rsqrt(jnp.mean(cq * cq, axis=-1, keepdims=True) + EPS) * gql_ref[...]).astype(BF16)
    ckv = c[:, Q_RANK:Q_RANK + KV_RANK]
    ckvn = (ckv * lax.rsqrt(jnp.mean(ckv * ckv, axis=-1, keepdims=True) + EPS) * gkl_ref[...]).astype(BF16)
    q2 = _dot(cqn, wq_ref[...])
    k2 = _dot(jnp.concatenate([ckvn, pb_ref[:, Q_RANK + KV_RANK:]], axis=1), wk_ref[...])
    v = _dot(ckvn, wv_ref[...])
    q, k = q2[:, :width], k2[:, :width]
    rc, rs = rc_ref[...], rs_ref[...]

    lane = lax.broadcasted_iota(jnp.int32, (1, LANE), 1)

    ssqs = [_dot((x * x).astype(BF16), sel_ref[...]) for x in (q, k)]
    packed = []
    for ssq in ssqs:
        r = lax.rsqrt(ssq * (1.0 / QK_B) + EPS)
        r_hi = r.astype(BF16)
        r_lo = (r - r_hi.astype(F32)).astype(BF16)
        packed.append(jnp.where(lane < HEADS_B, r_hi, r_lo))
    scales = [_dot(pk, exp_ref[...]) for pk in packed]

    qg = q2 * gq_ref[...]
    kg = k2 * gk_ref[...]
    for h in range(HEADS_B):
        sl = slice(h * SLOT_B, (h + 1) * SLOT_B)
        sw = slice(width + h * SLOT_B, width + (h + 1) * SLOT_B)
        qt_ref[0, sl, :] = ((qg[:, sl] * rc + qg[:, sw] * rs) * scales[0][:, sl]).T.astype(qt_ref.dtype)
        k_ref[:, sl] = ((kg[:, sl] * rc + kg[:, sw] * rs) * scales[1][:, sl]).astype(k_ref.dtype)
    vt_ref[0] = v.T.astype(vt_ref.dtype)


def _swap_rotary_halves(w):
    s = w.reshape(*w.shape[:-1], HEADS_B, SLOT_B)
    half = ROPE_B // 2
    s = jnp.concatenate([s[..., :NOPE_B], s[..., NOPE_B + half:QK_B], s[..., NOPE_B:NOPE_B + half], s[..., QK_B:]],
                        axis=-1)
    return s.reshape(w.shape)


def _mla_prep(proj, gql, gkl, wq, wk, wv, gq, gk, rc, rs, b, l):
    t = b * l
    tm = 512
    per_seq = l // tm
    const = lambda i: (0, 0)
    wq, wk, gq, gk = [jnp.concatenate([w, _swap_rotary_halves(w)], axis=1) for w in (wq, wk, gq, gk)]
    slot_of_lane = jnp.arange(HEADS_B * SLOT_B) // SLOT_B
    col = jnp.arange(LANE)
    sel = ((col[None, :] % HEADS_B == slot_of_lane[:, None]) & (col[None, :] < 2 * HEADS_B)).astype(BF16)
    spread = sel.T
    return pl.pallas_call(
        _mla_prep_kernel,
        grid=(t // tm,),
        in_specs=[
            pl.BlockSpec((tm, 512), lambda i: (i, OFF_B // 512)),
            pl.BlockSpec((1, Q_RANK), const), pl.BlockSpec((1, KV_RANK), const),
            pl.BlockSpec((Q_RANK, 2 * HEADS_B * SLOT_B), const),
            pl.BlockSpec((256, 2 * HEADS_B * SLOT_B), const),
            pl.BlockSpec((KV_RANK, WIDTH_B), const),
            pl.BlockSpec((1, 2 * HEADS_B * SLOT_B), const), pl.BlockSpec((1, 2 * HEADS_B * SLOT_B), const),
            pl.BlockSpec((tm, SLOT_B), lambda i: (i % per_seq, 0)),
            pl.BlockSpec((tm, SLOT_B), lambda i: (i % per_seq, 0)),
            pl.BlockSpec((HEADS_B * SLOT_B, LANE), const), pl.BlockSpec((LANE, HEADS_B * SLOT_B), const),
        ],
        out_specs=[
            pl.BlockSpec((1, HEADS_B * SLOT_B, tm), lambda i: (i // per_seq, 0, i % per_seq)),
            pl.BlockSpec((tm, HEADS_B * SLOT_B), lambda i: (i, 0)),
            pl.BlockSpec((1, WIDTH_B, tm), lambda i: (i // per_seq, 0, i % per_seq)),
        ],
        out_shape=[
            jax.ShapeDtypeStruct((b, HEADS_B * SLOT_B, l), BF16),
            jax.ShapeDtypeStruct((t, HEADS_B * SLOT_B), BF16),
            jax.ShapeDtypeStruct((b, WIDTH_B, l), BF16),
        ],
        compiler_params=_cparams(("parallel",)),
        name="latent_prep",
    )(proj, gql, gkl, wq, wk, wv, gq, gk, rc, rs, sel, spread)


def _mla_attn_kernel(qt_ref, k_ref, vt_ref, o_ref, vaug_ref, *, tk):
    nkv = vaug_ref.shape[1]
    tq = qt_ref.shape[2]

    @pl.when(pl.program_id(2) == 0)
    def _():
        ones_rows = jnp.where(lax.broadcasted_iota(jnp.int32, (VROWS - V_B, tk), 0) == 0, 1.0, 0.0).astype(BF16)
        for a in range(2):
            for j in range(nkv):
                vaug_ref[a, j, 0:V_B, :] = vt_ref[0, a * V_B:(a + 1) * V_B, j * tk:(j + 1) * tk]
                vaug_ref[a, j, V_B:VROWS, :] = ones_rows

    qw = 2 * LANE
    ntile = tq // qw

    def scores(c, j):
        return [_dot(k_ref[0, j * tk:(j + 1) * tk, a * SLOT_B:(a + 1) * SLOT_B],
                     qt_ref[0, a * SLOT_B:(a + 1) * SLOT_B, c * qw:(c + 1) * qw]) for a in range(2)]

    def values(j, ps):
        return [_dot(vaug_ref[a, j], ps[a]) for a in range(2)]

    st = scores(0, 0)
    for c in range(ntile):
        m = [jnp.full((1, qw), -jnp.inf, F32) for _ in range(2)]
        acc = [jnp.zeros((VROWS, qw), F32) for _ in range(2)]
        ps = None
        for j in range(nkv):
            if j + 1 < nkv:
                st_next = scores(c, j + 1)
            else:
                st_next = scores(c + 1, 0) if c + 1 < ntile else None
            pv = values(j - 1, ps) if j > 0 else None
            ps = []
            for a in range(2):
                mn = jnp.maximum(m[a], jnp.max(st[a], axis=0, keepdims=True))
                ps.append(jnp.exp2(st[a] - mn).astype(BF16))
                if pv is not None:
                    acc[a] = (acc[a] + pv[a]) * jnp.exp2(m[a] - mn)
                m[a] = mn
            st = st_next
        pv = values(nkv - 1, ps)
        outs = []
        for a in range(2):
            tot = acc[a] + pv[a]
            outs.append(tot[0:V_B] * (1.0 / tot[V_B:V_B + 1]))
        o_ref[0, c * qw:(c + 1) * qw, :] = jnp.concatenate(outs, axis=0).T.astype(o_ref.dtype)


def _mla_attn(qt, kp, vt, b, l):
    tq, tk = 2048, 256
    k3 = kp.reshape(b, l, HEADS_B * SLOT_B)
    out = pl.pallas_call(
        functools.partial(_mla_attn_kernel, tk=tk),
        grid=(b, HEADS_B // 2, l // tq),
        in_specs=[
            pl.BlockSpec((1, 2 * SLOT_B, tq), lambda bi, p, qi: (bi, p, qi)),
            pl.BlockSpec((1, l, 2 * SLOT_B), lambda bi, p, qi: (bi, 0, p)),
            pl.BlockSpec((1, 2 * V_B, l), lambda bi, p, qi: (bi, p, 0)),
        ],
        out_specs=pl.BlockSpec((1, tq, 2 * V_B), lambda bi, p, qi: (bi, qi, p)),
        out_shape=jax.ShapeDtypeStruct((b, l, WIDTH_B), BF16),
        scratch_shapes=[pltpu.VMEM((2, l // tk, VROWS, tk), BF16)],
        compiler_params=_cparams(("parallel", "parallel", "arbitrary")),
        name="latent_attention",
    )(qt, k3, vt)
    return out.reshape(b * l, WIDTH_B)


def _hyena_proj_kernel(x_ref, prev_ref, next_ref, g_ref, w_ref, cw_ref, cb_ref, u_ref, x0_ref, *, per_seq, halo):
    tm = x_ref.shape[0]
    i = pl.program_id(0) % per_seq

    def norm(x):
        return x * lax.rsqrt(jnp.mean(x * x, axis=-1, keepdims=True) + EPS) * g_ref[...]

    h = jnp.concatenate([jnp.where(i == 0, 0.0, norm(prev_ref[...])).astype(BF16),
                         norm(x_ref[...]).astype(BF16),
                         jnp.where(i == per_seq - 1, 0.0, norm(next_ref[...])).astype(BF16)], axis=0)
    n = h.shape[0]
    y = _dot(h, w_ref[...])
    ym = pltpu.roll(y, 1, 0)[halo:halo + tm]
    yp = pltpu.roll(y, n - 1, 0)[halo:halo + tm]
    c = ym * cw_ref[0:1, :] + y[halo:halo + tm] * cw_ref[1:2, :] + yp * cw_ref[2:3, :] + cb_ref[...]
    x0_ref[...] = c[:, :WIDTH_C].astype(x0_ref.dtype)
    u_ref[...] = (c[:, 2 * WIDTH_C:] * c[:, WIDTH_C:2 * WIDTH_C]).astype(u_ref.dtype)


def _hyena_proj(x2d, gain, w_c, conv_w, conv_b, l):
    t = x2d.shape[0]
    tm, halo = 1024, 16
    per_seq = l // tm
    nh = tm // halo
    const = lambda i: (0, 0)
    return pl.pallas_call(
        functools.partial(_hyena_proj_kernel, per_seq=per_seq, halo=halo),
        grid=(t // tm,),
        in_specs=[
            pl.BlockSpec((tm, D_MODEL), lambda i: (i, 0)),
            pl.BlockSpec((halo, D_MODEL), lambda i: (jnp.maximum(i * nh - 1, 0), 0)),
            pl.BlockSpec((halo, D_MODEL), lambda i: (jnp.minimum((i + 1) * nh, t // halo - 1), 0)),
            pl.BlockSpec((1, D_MODEL), const),
            pl.BlockSpec((D_MODEL, COLS_C), const),
            pl.BlockSpec((3, COLS_C), const),
            pl.BlockSpec((1, COLS_C), const),
        ],
        out_specs=[pl.BlockSpec((tm, WIDTH_C), lambda i: (i, 0)), pl.BlockSpec((tm, WIDTH_C), lambda i: (i, 0))],
        out_shape=[jax.ShapeDtypeStruct((t, WIDTH_C), BF16), jax.ShapeDtypeStruct((t, WIDTH_C), BF16)],
        compiler_params=_cparams(("parallel",)),
        name="hyena_proj",
    )(x2d, x2d, x2d, gain, w_c, conv_w, conv_b)


def _hyena_filter_kernel(pos_ref, fcol_ref, w1t_ref, w1c_ref, w1s_ref, b1_ref, w2_ref, b2_ref, w3_ref, fr_ref,
                         dl_ref, o_ref):
    tt = pos_ref[0:1, :]
    ang = pos_ref[1:2, :]
    valid = pos_ref[2:3, :]
    arg = fcol_ref[...] * ang
    pre1 = w1t_ref[...] * tt + _dot_exact(w1c_ref[...], jnp.cos(arg)) + _dot_exact(w1s_ref[...], jnp.sin(arg))
    fr = fr_ref[...]
    hid = jnp.sin(fr * (pre1 + b1_ref[...]))
    hid = jnp.sin(fr * (_dot_exact(w2_ref[...], hid) + b2_ref[...]))
    filt = _dot_exact(w3_ref[0], hid)
    o_ref[...] = filt * jnp.exp(-dl_ref[...] * tt) * valid


def _hyena_filter(pos, fcol, w1t, w1c, w1s, b1, w2t, b2, w3t, fr, dl, l):
    n = 2048
    const = lambda j: (0, 0)
    return pl.pallas_call(
        _hyena_filter_kernel,
        grid=(2 * l // n,),
        in_specs=[
            pl.BlockSpec((3, n), lambda j: (0, j)),
            pl.BlockSpec((16, 1), const),
            pl.BlockSpec((HYENA_HID, 1), const), pl.BlockSpec((HYENA_HID, 16), const),
            pl.BlockSpec((HYENA_HID, 16), const), pl.BlockSpec((HYENA_HID, 1), const),
            pl.BlockSpec((HYENA_HID, HYENA_HID), const), pl.BlockSpec((HYENA_HID, 1), const),
            pl.BlockSpec((1, WIDTH_C, HYENA_HID), lambda j: (j // (l // n), 0, 0)),
            pl.BlockSpec((HYENA_HID, 1), const), pl.BlockSpec((WIDTH_C, 1), const),
        ],
        out_specs=pl.BlockSpec((WIDTH_C, n), lambda j: (0, j)),
        out_shape=jax.ShapeDtypeStruct((WIDTH_C, 2 * l), F32),
        compiler_params=_cparams(("parallel",)),
        name="hyena_filter",
    )(pos, fcol, w1t, w1c, w1s, b1, w2t, b2, w3t, fr, dl)


def _hyena_conv_kernel(kf_ref, u_ref, o_ref, uf_ref, acc_ref, *, nb, bsz):
    p = TOEP
    n2 = 2 * nb * p

    def channel(c, carry):
        krow = kf_ref[pl.ds(c, 1), :]
        packed_rows = bsz % 16 == 0
        if not packed_rows:
            uf_ref[...] = u_ref[c].astype(F32)
        acc_ref[...] = jnp.zeros_like(acc_ref)
        for d in range(-(nb - 1), nb):
            a0 = (d * p) % n2
            b0 = ((d - 1) * p) % n2
            seg = jnp.concatenate([krow[:, a0:a0 + p], krow[:, b0:b0 + p]], axis=1)
            rolled = pltpu.roll(jnp.broadcast_to(seg, (p, 2 * p)), 0, 1, stride=1, stride_axis=0)
            toep = rolled[:, :p].astype(BF16)
            rows = (nb - abs(d)) * bsz
            src = 0 if d >= 0 else -d * bsz
            dst = d * bsz if d >= 0 else 0
            lhs = u_ref[c, src:src + rows, :] if packed_rows else uf_ref[src:src + rows, :].astype(BF16)
            acc_ref[dst:dst + rows, :] += _dot(lhs, toep)
        o_ref[c] = acc_ref[...].astype(o_ref.dtype)
        return carry

    lax.fori_loop(0, kf_ref.shape[0], channel, 0)


def _hyena_conv(kf, u_t, nb, bsz):
    cblk = 8
    rows = nb * bsz
    return pl.pallas_call(
        functools.partial(_hyena_conv_kernel, nb=nb, bsz=bsz),
        grid=(WIDTH_C // cblk,),
        in_specs=[
            pl.BlockSpec((cblk, 2 * nb * TOEP), lambda i: (i, 0)),
            pl.BlockSpec((cblk, rows, TOEP), lambda i: (i, 0, 0)),
        ],
        out_specs=pl.BlockSpec((cblk, rows, TOEP), lambda i: (i, 0, 0)),
        out_shape=jax.ShapeDtypeStruct((WIDTH_C, rows, TOEP), BF16),
        scratch_shapes=[pltpu.VMEM((rows, TOEP), F32), pltpu.VMEM((rows, TOEP), F32)],
        compiler_params=_cparams(("parallel",)),
        name="hyena_conv",
    )(kf, u_t)


def _merge_kernel(oa0_ref, oa1_ref, oa2_ref, l0_ref, l1_ref, l2_ref, yb_ref, cv_ref, u_ref, x0_ref, pg_ref, x_ref,
                  ex_ref, skip_ref, bg_ref, wa_ref, wb_ref, wc_ref, wo_ref, o_ref):
    lses = [l0_ref[0], l1_ref[0], l2_ref[0]]
    mx = jnp.maximum(jnp.maximum(lses[0], lses[1]), lses[2])
    pad = jnp.zeros((LANE - HEADS_A, mx.shape[1]), F32)
    num = None
    den = None
    for lse, oa in zip(lses, (oa0_ref, oa1_ref, oa2_ref)):
        wt = jnp.concatenate([jnp.exp(lse - mx), pad], axis=0).T
        w = _dot(wt.astype(BF16), ex_ref[...])
        num = w * oa[...].astype(F32) if num is None else num + w * oa[...].astype(F32)
        den = w if den is None else den + w
    ya = (num / den).astype(BF16)
    u = u_ref[...].astype(F32)
    yc = (x0_ref[...].astype(F32) * (cv_ref[...].astype(F32) + skip_ref[...] * u)).astype(BF16)
    mixed = None
    for i, (y, w_ref) in enumerate(((ya, wa_ref), (yb_ref[...], wb_ref), (yc, wc_ref))):
        sl = slice(i * D_MODEL, (i + 1) * D_MODEL)
        gate = jax.nn.sigmoid(pg_ref[:, sl].astype(F32) + bg_ref[:, sl])
        term = gate * _dot(y, w_ref[...])
        mixed = term if mixed is None else mixed + term
    o_ref[...] = x_ref[...] + _dot(mixed.astype(BF16), wo_ref[...])


def _merge(oas, lses, yb, cv, u, x0, proj, x2d, expand, skip, bg, wa, wb, wc, wo, l):
    t = x2d.shape[0]
    tm = 512
    per_seq = l // tm
    row = lambda i: (i, 0)
    const = lambda i: (0, 0)
    half = pl.BlockSpec((tm, 512), row)
    lse_spec = pl.BlockSpec((1, HEADS_A, tm), lambda i: (i // per_seq, 0, i % per_seq))
    return pl.pallas_call(
        _merge_kernel,
        grid=(t // tm,),
        in_specs=[
            half, half, half,
            lse_spec, lse_spec, lse_spec,
            half, half, half, half,
            pl.BlockSpec((tm, COLS_G), lambda i: (i, OFF_G // COLS_G)),
            pl.BlockSpec((tm, D_MODEL), row),
            pl.BlockSpec((LANE, WIDTH_A), const),
            pl.BlockSpec((1, WIDTH_C), const), pl.BlockSpec((1, COLS_G), const),
            pl.BlockSpec((WIDTH_A, D_MODEL), const), pl.BlockSpec((WIDTH_B, D_MODEL), const),
            pl.BlockSpec((WIDTH_C, D_MODEL), const), pl.BlockSpec((D_MODEL, D_MODEL), const),
        ],
        out_specs=pl.BlockSpec((tm, D_MODEL), row),
        out_shape=jax.ShapeDtypeStruct((t, D_MODEL), F32),
        compiler_params=_cparams(("parallel",)),
        name="branch_merge",
    )(*oas, *lses, yb, cv, u, x0, proj, x2d, expand, skip, bg, wa, wb, wc, wo)


def _ffn_up_kernel(x_ref, prev_ref, next_ref, g_ref, wa_ref, wg_ref, cwa_ref, cwg_ref, cba_ref, cbg_ref, o_ref,
                   h_ref, *, per_seq, halo):
    tm = x_ref.shape[0]

    @pl.when(pl.program_id(1) == 0)
    def _():
        i = pl.program_id(0) % per_seq

        def norm(x):
            return x * lax.rsqrt(jnp.mean(x * x, axis=-1, keepdims=True) + EPS) * g_ref[...]

        h_ref[0:halo, :] = jnp.where(i == 0, 0.0, norm(prev_ref[...])).astype(BF16)
        h_ref[halo:halo + tm, :] = norm(x_ref[...]).astype(BF16)
        h_ref[halo + tm:, :] = jnp.where(i == per_seq - 1, 0.0, norm(next_ref[...])).astype(BF16)

    h = h_ref[...]
    n = h.shape[0]

    def conv(y, cw_ref, cb_ref):
        ym = pltpu.roll(y, 1, 0)[halo:halo + tm]
        yp = pltpu.roll(y, n - 1, 0)[halo:halo + tm]
        return ym * cw_ref[0:1, :] + y[halo:halo + tm] * cw_ref[1:2, :] + yp * cw_ref[2:3, :] + cb_ref[...]

    ya = _dot(h, wa_ref[...])
    yg = _dot(h, wg_ref[...])
    a = conv(ya, cwa_ref, cba_ref)
    g = conv(yg, cwg_ref, cbg_ref)
    e = jnp.exp2(a * (GELU_K1 + GELU_K3 * (a * a)))
    o_ref[...] = (a * g / (1.0 + e)).astype(o_ref.dtype)


def _ffn_up(x2d, gain, wa, wg, cwa, cwg, cba, cbg, l):
    t = x2d.shape[0]
    tm, tn, halo = 1024, D_FF // 2, 16
    per_seq = l // tm
    nh = tm // halo
    const = lambda i, j: (0, 0)
    col = lambda i, j: (0, j)
    return pl.pallas_call(
        functools.partial(_ffn_up_kernel, per_seq=per_seq, halo=halo),
        grid=(t // tm, D_FF // tn),
        in_specs=[
            pl.BlockSpec((tm, D_MODEL), lambda i, j: (i, 0)),
            pl.BlockSpec((halo, D_MODEL), lambda i, j: (jnp.maximum(i * nh - 1, 0), 0)),
            pl.BlockSpec((halo, D_MODEL), lambda i, j: (jnp.minimum((i + 1) * nh, t // halo - 1), 0)),
            pl.BlockSpec((1, D_MODEL), const),
            pl.BlockSpec((D_MODEL, tn), col), pl.BlockSpec((D_MODEL, tn), col),
            pl.BlockSpec((3, tn), col), pl.BlockSpec((3, tn), col),
            pl.BlockSpec((1, tn), col), pl.BlockSpec((1, tn), col),
        ],
        out_specs=pl.BlockSpec((tm, tn), lambda i, j: (i, j)),
        out_shape=jax.ShapeDtypeStruct((t, D_FF), BF16),
        scratch_shapes=[pltpu.VMEM((tm + 2 * halo, D_MODEL), BF16)],
        compiler_params=_cparams(("parallel", "arbitrary")),
        name="ffn_up",
    )(x2d, x2d, x2d, gain, wa, wg, cwa, cwg, cba, cbg)


def _ffn_down_kernel(x_ref, a_ref, w_ref, o_ref):
    o_ref[...] = x_ref[...] + _dot(a_ref[...], w_ref[...])


def _ffn_down(x2d, act, w):
    t = x2d.shape[0]
    tm = 512
    return pl.pallas_call(
        _ffn_down_kernel,
        grid=(t // tm,),
        in_specs=[
            pl.BlockSpec((tm, D_MODEL), lambda i: (i, 0)),
            pl.BlockSpec((tm, D_FF), lambda i: (i, 0)),
            pl.BlockSpec((D_FF, D_MODEL), lambda i: (0, 0)),
        ],
        out_specs=pl.BlockSpec((tm, D_MODEL), lambda i: (i, 0)),
        out_shape=jax.ShapeDtypeStruct((t, D_MODEL), F32),
        compiler_params=_cparams(("parallel",)),
        name="ffn_down",
    )(x2d, act, w)


def _layer_params(p, l):
    w_in = p['w_in'][l]
    a_end, b_end, c_end = COLS_A, COLS_A + COLS_B, COLS_A + COLS_B + COLS_C
    w_r = jnp.concatenate([w_in[:, c_end:], w_in[:, :a_end], w_in[:, a_end:b_end],
                           jnp.zeros((D_MODEL, PROJ_W - OFF_B - COLS_B), F32)], axis=1).astype(BF16)
    w_c = w_in[:, b_end:c_end].astype(BF16)
    slot_pad = SLOT_B - QK_B

    def slots(w):
        return jnp.pad(w, [(0, 0)] * (w.ndim - 1) + [(0, slot_pad)]).reshape(*w.shape[:-2], HEADS_B * SLOT_B)

    wq = slots(p['b_w_uq'][l].reshape(Q_RANK, HEADS_B, QK_B)).astype(BF16)
    wkv = p['b_w_ukv'][l].reshape(KV_RANK, HEADS_B, NOPE_B + V_B)
    wk_nope = jnp.pad(wkv[:, :, :NOPE_B], ((0, 0), (0, 0), (0, SLOT_B - NOPE_B))).reshape(KV_RANK, -1)
    place = jnp.pad(jnp.eye(ROPE_B, dtype=F32), ((0, 0), (NOPE_B, SLOT_B - QK_B)))
    wk_pe = jnp.tile(place, (1, HEADS_B))
    wk = jnp.concatenate([wk_nope, wk_pe, jnp.zeros((256 - KV_RANK - ROPE_B, HEADS_B * SLOT_B), F32)], axis=0)
    wv = wkv[:, :, NOPE_B:].reshape(KV_RANK, WIDTH_B)
    gq = jnp.tile(jnp.pad(p['b_q_g'][l], (0, slot_pad)), HEADS_B)[None] * (QK_B ** -0.5 * LOG2E)
    gk = jnp.tile(jnp.pad(p['b_k_g'][l], (0, slot_pad)), HEADS_B)[None]
    w_up = p['w_up'][l]
    cw = p['ffn_conv_w'][l]
    cb = p['ffn_conv_b'][l]
    return dict(
        norm_attn_g=p['norm_attn_g'][l][None], w_r=w_r, w_c=w_c,
        a_qg=[jnp.tile(p['a_q_g'][l, g], 4)[None] * (HEAD_DIM_A ** -0.5 * LOG2E) for g in range(3)],
        a_kg=[jnp.tile(p['a_k_g'][l, g], 4)[None] for g in range(3)],
        gql=p['b_q_lat_g'][l][None], gkl=p['b_kv_lat_g'][l][None],
        wq=wq, wk=wk.astype(BF16), wv=wv.astype(BF16), gq=gq, gk=gk,
        c_conv_w=p['c_conv_w'][l], c_conv_b=p['c_conv_b'][l][None],
        w1t=p['c_w1'][l][0:1].T, w1c=p['c_w1'][l][1:17].T, w1s=-p['c_w1'][l][17:33].T,
        b1=p['c_b1'][l][:, None], w2t=p['c_w2'][l].T, b2=p['c_b2'][l][:, None],
        w3t=p['c_w3'][l].T.reshape(2, WIDTH_C, HYENA_HID), fr=p['c_freq'][l][:, None],
        skip=p['c_skip'][l][None], bg=p['b_gate'][l][None],
        wa=p['w_br_a'][l].astype(BF16), wb=p['w_br_b'][l].astype(BF16), wc=p['w_br_c'][l].astype(BF16),
        wo=p['w_out'][l].astype(BF16), norm_ffn_g=p['norm_ffn_g'][l][None],
        w_up_a=w_up[:, :D_FF].astype(BF16), w_up_g=w_up[:, D_FF:].astype(BF16),
        cwa=cw[:, :D_FF], cwg=cw[:, D_FF:], cba=cb[None, :D_FF], cbg=cb[None, D_FF:],
        w_down=p['w_down'][l].astype(BF16),
    )


def _seq_constants(l):
    half = ROPE_B // 2
    pos = jnp.arange(l, dtype=F32)
    inv = ROPE_THETA ** (-jnp.arange(half, dtype=F32) / half)
    ang = pos[:, None] * inv[None, :]
    cos, sin = jnp.cos(ang), jnp.sin(ang)
    one = jnp.ones((l, NOPE_B), F32)
    zn = jnp.zeros((l, NOPE_B), F32)
    zp = jnp.zeros((l, SLOT_B - QK_B), F32)
    rc = jnp.concatenate([one, cos, cos, zp], axis=1)
    rs = jnp.concatenate([zn, -sin, sin, zp], axis=1)
    m = jnp.arange(2 * l)
    p_idx = jnp.where(m < l, m, 2 * l - m)
    tlin = jnp.linspace(0.0, 1.0, l, dtype=F32)
    tt = tlin[jnp.minimum(p_idx, l - 1)]
    angf = (2.0 * math.pi / l) * p_idx.astype(F32)
    valid = (m != l).astype(F32)
    bands = (HYENA_EMB - 1) // 2
    fcol = jnp.linspace(1e-4, bands - 1, bands, dtype=F32)[:, None]
    deltas = jnp.abs(jnp.linspace(math.log(HYENA_TARGET) / HYENA_FAST_DECAY,
                                  math.log(HYENA_TARGET) / HYENA_SLOW_DECAY, WIDTH_C, dtype=F32))[:, None]
    return dict(rc=rc, rs=rs, pos=jnp.stack([tt, angf, valid]), fcol=fcol, deltas=deltas)


def _static_tables():
    expand = (jnp.arange(LANE)[:, None] == (jnp.arange(WIDTH_A) // HEAD_DIM_A)[None, :]).astype(BF16)
    slopes = jnp.exp2(-ALIBI_MAX * (jnp.arange(HEADS_A, dtype=F32) + 1.0) / HEADS_A)
    return expand, slopes


def _layer(x2d, lp, sc, expand, slopes, b, l):
    proj = _inproj(x2d, lp['norm_attn_g'], lp['w_r'])
    oas, lses = [], []
    for g, dil in enumerate(DILATIONS):
        o, lse = _mixer_a_group(proj, lp['a_qg'][g], lp['a_kg'][g], -slopes * (dil * LOG2E), b, l, g, dil)
        oas.append(o)
        lses.append(lse)
    qp, kp, vt = _mla_prep(proj, lp['gql'], lp['gkl'], lp['wq'], lp['wk'], lp['wv'], lp['gq'], lp['gk'],
                           sc['rc'], sc['rs'], b, l)
    yb = _mla_attn(qp, kp, vt, b, l)
    u, x0 = _hyena_proj(x2d, lp['norm_attn_g'], lp['w_c'], lp['c_conv_w'], lp['c_conv_b'], l)
    kf = _hyena_filter(sc['pos'], sc['fcol'], lp['w1t'], lp['w1c'], lp['w1s'], lp['b1'], lp['w2t'], lp['b2'],
                       lp['w3t'], lp['fr'], sc['deltas'], l)
    nb = l // TOEP
    u_t = jnp.transpose(u.reshape(b, nb, TOEP, WIDTH_C), (3, 1, 0, 2)).reshape(WIDTH_C, nb * b, TOEP)
    cv_t = _hyena_conv(kf, u_t, nb, b)
    cv = jnp.transpose(cv_t.reshape(WIDTH_C, nb, b, TOEP), (2, 1, 3, 0)).reshape(b * l, WIDTH_C)
    xm = _merge(oas, lses, yb, cv, u, x0, proj, x2d, expand, lp['skip'], lp['bg'],
                lp['wa'], lp['wb'], lp['wc'], lp['wo'], l)
    act = _ffn_up(xm, lp['norm_ffn_g'], lp['w_up_a'], lp['w_up_g'], lp['cwa'], lp['cwg'], lp['cba'], lp['cbg'], l)
    return _ffn_down(xm, act, lp['w_down'])


def _trunk(x, layer_params, expand, slopes):
    b, l, _ = x.shape
    sc = _seq_constants(l)
    y = x.reshape(b * l, D_MODEL)
    for lp in layer_params:
        y = _layer(y, lp, sc, expand, slopes, b, l)
    return y.reshape(b, l, D_MODEL)


def kernel(x_prompt, x_sample, norm_attn_g, w_in, b_gate, a_q_g, a_k_g, b_q_lat_g, b_kv_lat_g, b_w_uq, b_w_ukv, b_q_g, b_k_g, c_conv_w, c_conv_b, c_w1, c_b1, c_w2, c_b2, c_w3, c_freq, c_skip, w_br_a, w_br_b, w_br_c, w_out, norm_ffn_g, w_up, ffn_conv_w, ffn_conv_b, w_down):
    p = dict(norm_attn_g=norm_attn_g, w_in=w_in, b_gate=b_gate, a_q_g=a_q_g, a_k_g=a_k_g, b_q_lat_g=b_q_lat_g,
             b_kv_lat_g=b_kv_lat_g, b_w_uq=b_w_uq, b_w_ukv=b_w_ukv, b_q_g=b_q_g, b_k_g=b_k_g, c_conv_w=c_conv_w,
             c_conv_b=c_conv_b, c_w1=c_w1, c_b1=c_b1, c_w2=c_w2, c_b2=c_b2, c_w3=c_w3, c_freq=c_freq,
             c_skip=c_skip, w_br_a=w_br_a, w_br_b=w_br_b, w_br_c=w_br_c, w_out=w_out, norm_ffn_g=norm_ffn_g,
             w_up=w_up, ffn_conv_w=ffn_conv_w, ffn_conv_b=ffn_conv_b, w_down=w_down)
    layer_params = [_layer_params(p, l) for l in range(DEPTH)]
    expand, slopes = _static_tables()
    return _trunk(x_prompt, layer_params, expand, slopes), _trunk(x_sample, layer_params, expand, slopes)
```

```python
import functools
import math

import jax
import jax.numpy as jnp
from jax import lax
from jax.experimental import pallas as pl
from jax.experimental.pallas import tpu as pltpu

F32 = jnp.float32
BF16 = jnp.bfloat16

D_MODEL = 1024
DEPTH = 2
EPS = 1e-6
HEADS_A = 8
HEAD_DIM_A = 64
DILATIONS = (1, 4, 16)
HALF_A = 64
WIDTH_A = HEADS_A * HEAD_DIM_A
ALIBI_MAX = 8.0
HEADS_B = 8
NOPE_B = 64
ROPE_B = 32
QK_B = NOPE_B + ROPE_B
V_B = 64
Q_RANK = 256
KV_RANK = 128
ROPE_THETA = 10000.0
WIDTH_B = HEADS_B * V_B
WIDTH_C = 512
HYENA_EMB = 33
HYENA_HID = 64
HYENA_FAST_DECAY = 0.3
HYENA_SLOW_DECAY = 1.5
HYENA_TARGET = 1e-2
D_FF = 2816
COLS_A = 3 * 3 * WIDTH_A
COLS_B = Q_RANK + KV_RANK + ROPE_B
COLS_C = 3 * WIDTH_C
COLS_G = 3 * D_MODEL
OFF_G = 0
OFF_A = COLS_G
OFF_B = OFF_A + COLS_A
PROJ_W = OFF_B + 512
LANE = 128
SLOT_B = 128
VROWS = 80
TOEP = 256
LOG2E = math.log2(math.e)
LN2 = math.log(2.0)
GELU_K1 = -2.0 * math.sqrt(2.0 / math.pi) * LOG2E
GELU_K3 = 0.044715 * GELU_K1
MASKED = 1e32
VMEM_LIMIT = 48 * 1024 * 1024


def _cparams(sem):
    return pltpu.CompilerParams(dimension_semantics=sem, vmem_limit_bytes=VMEM_LIMIT)


def _nt_dot(a, b):
    return lax.dot_general(a, b, (((1,), (1,)), ((), ())), preferred_element_type=F32)


def _dot(a, b):
    return jnp.dot(a, b, preferred_element_type=F32)


def _dot_exact(a, b):
    return jnp.dot(a, b, preferred_element_type=F32, precision=lax.Precision.HIGHEST)


def _inproj_kernel(x_ref, g_ref, w_ref, o_ref, h_ref):
    @pl.when(pl.program_id(1) == 0)
    def _():
        x = x_ref[...]
        ms = jnp.mean(x * x, axis=-1, keepdims=True)
        h_ref[...] = (x * lax.rsqrt(ms + EPS) * g_ref[...]).astype(BF16)

    o_ref[...] = _dot(h_ref[...], w_ref[...]).astype(o_ref.dtype)


def _inproj(x2d, gain, w_r):
    t = x2d.shape[0]
    tm, tn = 1024, PROJ_W // 4
    return pl.pallas_call(
        _inproj_kernel,
        grid=(t // tm, PROJ_W // tn),
        in_specs=[
            pl.BlockSpec((tm, D_MODEL), lambda i, j: (i, 0)),
            pl.BlockSpec((1, D_MODEL), lambda i, j: (0, 0)),
            pl.BlockSpec((D_MODEL, tn), lambda i, j: (0, j)),
        ],
        out_specs=pl.BlockSpec((tm, tn), lambda i, j: (i, j)),
        out_shape=jax.ShapeDtypeStruct((t, PROJ_W), BF16),
        scratch_shapes=[pltpu.VMEM((tm, D_MODEL), BF16)],
        compiler_params=_cparams(("parallel", "arbitrary")),
        name="inproj",
    )(x2d, gain, w_r)


def _dil_attn_kernel(negc_ref, q_ref, k_ref, v_ref, qg_ref, kg_ref, o_ref, lse_ref,
                     qn_ref, kn_ref, vn_ref, stg_ref, stg2_ref, ost_ref, bias_ref, s_ref, p_ref, r_ref, *, l, dil):
    hb = pl.program_id(1)
    lu = l // dil
    cw = q_ref.shape[-1]
    win = min(2 * LANE, lu)
    nq = lu // LANE
    row = lax.broadcasted_iota(jnp.int32, (cw, cw), 0) // HEAD_DIM_A
    col = lax.broadcasted_iota(jnp.int32, (cw, cw), 1) // HEAD_DIM_A
    seg_mean = jnp.where(row == col, 1.0 / HEAD_DIM_A, 0.0).astype(BF16)

    chunk = 512
    per = chunk // dil
    step4 = 4

    def norm_body(c, carry):
        r0 = pl.multiple_of(c * chunk, chunk)
        for src, gain, dst in ((q_ref, qg_ref, qn_ref), (k_ref, kg_ref, kn_ref), (v_ref, None, vn_ref)):
            if gain is None and dil == 1:
                continue
            x = src[0, pl.ds(r0, chunk), :].astype(F32)
            if gain is not None:
                msq = _dot((x * x).astype(BF16), seg_mean)
                x = x * lax.rsqrt(msq + EPS) * gain[...]
            if dil == 1:
                dst[pl.ds(r0, chunk), :] = x.astype(BF16)
            else:
                for h in range(cw // LANE):
                    stg_ref[h] = x[:, h * LANE:(h + 1) * LANE]
                src = stg_ref
                if dil > step4:
                    sub = chunk // step4
                    for b in range(step4):
                        for h in range(cw // LANE):
                            stg2_ref[h, b * sub:(b + 1) * sub, :] = stg_ref[h, pl.ds(b, sub, stride=step4), :]
                    src = stg2_ref
                for r in range(dil):
                    d0 = pl.multiple_of(r * lu + c * per, per)
                    start = r if dil <= step4 else (r % step4) * (chunk // step4) + r // step4
                    for h in range(cw // LANE):
                        dst[pl.ds(d0, per), h * LANE:(h + 1) * LANE] = (
                            src[h, pl.ds(start, per, stride=min(dil, step4)), :].astype(BF16))
        return carry

    lax.fori_loop(0, l // chunk, norm_body, 0)

    lane_lo = lax.broadcasted_iota(jnp.int32, (1, LANE), 1) < HEAD_DIM_A
    npair = cw // LANE
    tiles_per_step = 2
    j_k = lax.broadcasted_iota(jnp.int32, (win, LANE), 0)
    i_q = lax.broadcasted_iota(jnp.int32, (win, LANE), 1)
    for variant in range(3):
        absrel = jnp.abs(j_k - i_q - variant * HALF_A).astype(F32)
        absrel = jnp.where(absrel <= float(HALF_A), absrel, MASKED)
        for pair in range(npair):
            bias_ref[variant * npair + pair] = jnp.concatenate(
                [absrel * negc_ref[4 * hb + 2 * pair + a] for a in range(2)], axis=1)
    row8 = lax.broadcasted_iota(jnp.int32, (8, LANE), 0)

    nsteps = l // (LANE * tiles_per_step)

    def coords(idx):
        r = idx // nq
        q0 = pl.multiple_of((idx % nq) * LANE, LANE)
        w0 = pl.multiple_of(jnp.clip(q0 - HALF_A, 0, lu - win), HALF_A)
        return r, q0, w0, pl.multiple_of(r * lu, LANE)

    def issue_scores(step, slot):
        for u in range(tiles_per_step):
            r, q0, w0, base = coords(step * tiles_per_step + u)
            q = qn_ref[pl.ds(base + q0, LANE), :]
            kw = kn_ref[pl.ds(base + w0, win), :]
            for pair in range(npair):
                qp = q[:, pair * LANE:(pair + 1) * LANE]
                zero = jnp.zeros_like(qp)
                qs = jnp.concatenate([jnp.where(lane_lo, qp, zero), jnp.where(lane_lo, zero, qp)], axis=0)
                s_ref[slot, u * npair + pair] = _nt_dot(kw[:, pair * LANE:(pair + 1) * LANE], qs)

    def softmax(step, slot):
        for u in range(tiles_per_step):
            idx = step * tiles_per_step + u
            r, q0, w0, base = coords(idx)
            variant = (q0 - w0) // HALF_A
            lse_tile = jnp.zeros((8, LANE), F32)
            for pair in range(npair):
                st = s_ref[slot, u * npair + pair] + bias_ref[variant * npair + pair]
                m = jnp.max(st, axis=0, keepdims=True)
                p = jnp.exp2(st - m)
                den = jnp.sum(p, axis=0, keepdims=True)
                p_ref[slot, u * npair + pair] = p.astype(BF16)
                r_ref[slot, u * npair + pair] = 1.0 / den
                lse = (m + jnp.log2(den)) * LN2
                for a in range(2):
                    lse_tile = jnp.where(row8 == 2 * pair + a, lse[:, a * LANE:(a + 1) * LANE], lse_tile)
            lse_ref[0, 0, idx] = lse_tile

    def issue_values(step, slot):
        for u in range(tiles_per_step):
            r, q0, w0, base = coords(step * tiles_per_step + u)
            vw = v_ref[0, pl.ds(w0, win), :] if dil == 1 else vn_ref[pl.ds(base + w0, win), :]
            outs = []
            for pair in range(npair):
                vt = vw[:, pair * LANE:(pair + 1) * LANE].T
                res = _dot(vt, p_ref[slot, u * npair + pair])
                rden = r_ref[slot, u * npair + pair]
                o_t = jnp.concatenate([res[:HEAD_DIM_A, :LANE] * rden[:, :LANE],
                                       res[HEAD_DIM_A:, LANE:] * rden[:, LANE:]], axis=0)
                outs.append(o_t.T)
            if dil == 1:
                o_ref[0, pl.ds(q0, LANE), :] = jnp.concatenate(outs, axis=1).astype(o_ref.dtype)
            else:
                for h, o_pair in enumerate(outs):
                    ost_ref[h, pl.ds(r + q0 * dil, LANE, stride=dil), :] = o_pair

    p_ref[...] = jnp.zeros_like(p_ref)
    r_ref[...] = jnp.zeros_like(r_ref)
    issue_scores(jnp.int32(0), 0)

    per_trip = min(16, nsteps)

    def step_body(j, carry):
        for i in range(per_trip):
            k = per_trip * j + i
            issue_scores(jnp.minimum(k + 1, nsteps - 1), (i + 1) % 2)
            issue_values(jnp.maximum(k - 1, 0), (i + 1) % 2)
            softmax(k, i % 2)
        return carry

    lax.fori_loop(0, nsteps // per_trip, step_body, 0)
    issue_values(jnp.int32(nsteps - 1), 1)

    if dil != 1:
        def out_body(c, carry):
            r0 = pl.multiple_of(c * chunk, chunk)
            for h in range(cw // LANE):
                o_ref[0, pl.ds(r0, chunk), h * LANE:(h + 1) * LANE] = (
                    ost_ref[h, pl.ds(r0, chunk), :].astype(o_ref.dtype))
            return carry

        lax.fori_loop(0, l // chunk, out_body, 0)


def _mixer_a_group(proj, qg, kg, negc, b, l, g, dil):
    pv = proj.reshape(b, l, PROJ_W)
    cq, ck, cv = (OFF_A + (0 + g) * 512) // 256, (OFF_A + (3 + g) * 512) // 256, (OFF_A + (6 + g) * 512) // 256

    def in_spec(c0):
        return pl.BlockSpec((1, l, 256), lambda bi, hb: (bi, 0, c0 + hb))

    nt = l // LANE
    win = min(2 * LANE, l // dil)
    o, lse = pl.pallas_call(
        functools.partial(_dil_attn_kernel, l=l, dil=dil),
        grid=(b, 2),
        in_specs=[
            pl.BlockSpec(memory_space=pltpu.SMEM),
            in_spec(cq), in_spec(ck), in_spec(cv),
            pl.BlockSpec((1, 256), lambda bi, hb: (0, 0)),
            pl.BlockSpec((1, 256), lambda bi, hb: (0, 0)),
        ],
        out_specs=[
            pl.BlockSpec((1, l, 256), lambda bi, hb: (bi, 0, hb)),
            pl.BlockSpec((1, 1, nt, 8, LANE), lambda bi, hb: (bi, hb, 0, 0, 0)),
        ],
        out_shape=[
            jax.ShapeDtypeStruct((b, l, WIDTH_A), BF16),
            jax.ShapeDtypeStruct((b, 2, nt, 8, LANE), F32),
        ],
        scratch_shapes=[pltpu.VMEM((l, 256), BF16), pltpu.VMEM((l, 256), BF16),
                        pltpu.VMEM((l if dil != 1 else 16, 256), BF16),
                        pltpu.VMEM((2, 512 if dil != 1 else 8, LANE), F32),
                        pltpu.VMEM((2, 512 if dil > 4 else 8, LANE), F32),
                        pltpu.VMEM((2, l if dil != 1 else 8, LANE), F32),
                        pltpu.VMEM((6, win, 2 * LANE), F32),
                        pltpu.VMEM((2, 4, win, 2 * LANE), F32), pltpu.VMEM((2, 4, win, 2 * LANE), BF16),
                        pltpu.VMEM((2, 4, 1, 2 * LANE), F32)],
        compiler_params=_cparams(("parallel", "arbitrary")),
        name=f"dilated_attention_g{g}",
    )(negc, pv, pv, pv, qg, kg)
    lse = lse[:, :, :, :4, :].reshape(b, 2, dil, nt // dil, 4, LANE)
    lse = lse.transpose(0, 1, 4, 3, 5, 2).reshape(b, HEADS_A, l)
    return o.reshape(b * l, WIDTH_A), lse


def _mla_prep_kernel(pb_ref, gql_ref, gkl_ref, wq_ref, wk_ref, wv_ref, gq_ref, gk_ref, rc_ref, rs_ref,
                     sel_ref, exp_ref, qt_ref, k_ref, vt_ref):
    width = HEADS_B * SLOT_B
    c = pb_ref[...].astype(F32)
    cq = c[:, :Q_RANK]
    cqn = (cq * lax.rsqrt(jnp.mean(cq * cq, axis=-1, keepdims=True) + EPS) * gql_ref[...]).astype(BF16)
    ckv = c[:, Q_RANK:Q_RANK + KV_RANK]
    ckvn = (ckv * lax.rsqrt(jnp.mean(ckv * ckv, axis=-1, keepdims=True) + EPS) * gkl_ref[...]).astype(BF16)
    q2 = _dot(cqn, wq_ref[...])
    k2 = _dot(jnp.concatenate([ckvn, pb_ref[:, Q_RANK + KV_RANK:]], axis=1), wk_ref[...])
    v = _dot(ckvn, wv_ref[...])
    q, k = q2[:, :width], k2[:, :width]
    rc, rs = rc_ref[...], rs_ref[...]

    lane = lax.broadcasted_iota(jnp.int32, (1, LANE), 1)

    ssqs = [_dot((x * x).astype(BF16), sel_ref[...]) for x in (q, k)]
    packed = []
    for ssq in ssqs:
        r = lax.rsqrt(ssq * (1.0 / QK_B) + EPS)
        r_hi = r.astype(BF16)
        r_lo = (r - r_hi.astype(F32)).astype(BF16)
        packed.append(jnp.where(lane < HEADS_B, r_hi, r_lo))
    scales = [_dot(pk, exp_ref[...]) for pk in packed]

    qg = q2 * gq_ref[...]
    kg = k2 * gk_ref[...]
    for h in range(HEADS_B):
        sl = slice(h * SLOT_B, (h + 1) * SLOT_B)
        sw = slice(width + h * SLOT_B, width + (h + 1) * SLOT_B)
        qt_ref[0, sl, :] = ((qg[:, sl] * rc + qg[:, sw] * rs) * scales[0][:, sl]).T.astype(qt_ref.dtype)
        k_ref[:, sl] = ((kg[:, sl] * rc + kg[:, sw] * rs) * scales[1][:, sl]).astype(k_ref.dtype)
    vt_ref[0] = v.T.astype(vt_ref.dtype)


def _swap_rotary_halves(w):
    s = w.reshape(*w.shape[:-1], HEADS_B, SLOT_B)
    half = ROPE_B // 2
    s = jnp.concatenate([s[..., :NOPE_B], s[..., NOPE_B + half:QK_B], s[..., NOPE_B:NOPE_B + half], s[..., QK_B:]],
                        axis=-1)
    return s.reshape(w.shape)


def _mla_prep(proj, gql, gkl, wq, wk, wv, gq, gk, rc, rs, b, l):
    t = b * l
    tm = 512
    per_seq = l // tm
    const = lambda i: (0, 0)
    wq, wk, gq, gk = [jnp.concatenate([w, _swap_rotary_halves(w)], axis=1) for w in (wq, wk, gq, gk)]
    slot_of_lane = jnp.arange(HEADS_B * SLOT_B) // SLOT_B
    col = jnp.arange(LANE)
    sel = ((col[None, :] % HEADS_B == slot_of_lane[:, None]) & (col[None, :] < 2 * HEADS_B)).astype(BF16)
    spread = sel.T
    return pl.pallas_call(
        _mla_prep_kernel,
        grid=(t // tm,),
        in_specs=[
            pl.BlockSpec((tm, 512), lambda i: (i, OFF_B // 512)),
            pl.BlockSpec((1, Q_RANK), const), pl.BlockSpec((1, KV_RANK), const),
            pl.BlockSpec((Q_RANK, 2 * HEADS_B * SLOT_B), const),
            pl.BlockSpec((256, 2 * HEADS_B * SLOT_B), const),
            pl.BlockSpec((KV_RANK, WIDTH_B), const),
            pl.BlockSpec((1, 2 * HEADS_B * SLOT_B), const), pl.BlockSpec((1, 2 * HEADS_B * SLOT_B), const),
            pl.BlockSpec((tm, SLOT_B), lambda i: (i % per_seq, 0)),
            pl.BlockSpec((tm, SLOT_B), lambda i: (i % per_seq, 0)),
            pl.BlockSpec((HEADS_B * SLOT_B, LANE), const), pl.BlockSpec((LANE, HEADS_B * SLOT_B), const),
        ],
        out_specs=[
            pl.BlockSpec((1, HEADS_B * SLOT_B, tm), lambda i: (i // per_seq, 0, i % per_seq)),
            pl.BlockSpec((tm, HEADS_B * SLOT_B), lambda i: (i, 0)),
            pl.BlockSpec((1, WIDTH_B, tm), lambda i: (i // per_seq, 0, i % per_seq)),
        ],
        out_shape=[
            jax.ShapeDtypeStruct((b, HEADS_B * SLOT_B, l), BF16),
            jax.ShapeDtypeStruct((t, HEADS_B * SLOT_B), BF16),
            jax.ShapeDtypeStruct((b, WIDTH_B, l), BF16),
        ],
        compiler_params=_cparams(("parallel",)),
        name="latent_prep",
    )(proj, gql, gkl, wq, wk, wv, gq, gk, rc, rs, sel, spread)


def _mla_attn_kernel(qt_ref, k_ref, vt_ref, o_ref, vaug_ref, *, tk):
    nkv = vaug_ref.shape[1]
    tq = qt_ref.shape[2]

    @pl.when(pl.program_id(2) == 0)
    def _():
        ones_rows = jnp.where(lax.broadcasted_iota(jnp.int32, (VROWS - V_B, tk), 0) == 0, 1.0, 0.0).astype(BF16)
        for a in range(2):
            for j in range(nkv):
                vaug_ref[a, j, 0:V_B, :] = vt_ref[0, a * V_B:(a + 1) * V_B, j * tk:(j + 1) * tk]
                vaug_ref[a, j, V_B:VROWS, :] = ones_rows

    qw = 2 * LANE
    ntile = tq // qw

    def scores(c, j):
        return [_dot(k_ref[0, j * tk:(j + 1) * tk, a * SLOT_B:(a + 1) * SLOT_B],
                     qt_ref[0, a * SLOT_B:(a + 1) * SLOT_B, c * qw:(c + 1) * qw]) for a in range(2)]

    def values(j, ps):
        return [_dot(vaug_ref[a, j], ps[a]) for a in range(2)]

    st = scores(0, 0)
    for c in range(ntile):
        m = [jnp.full((1, qw), -jnp.inf, F32) for _ in range(2)]
        acc = [jnp.zeros((VROWS, qw), F32) for _ in range(2)]
        ps = None
        for j in range(nkv):
            if j + 1 < nkv:
                st_next = scores(c, j + 1)
            else:
                st_next = scores(c + 1, 0) if c + 1 < ntile else None
            pv = values(j - 1, ps) if j > 0 else None
            ps = []
            for a in range(2):
                mn = jnp.maximum(m[a], jnp.max(st[a], axis=0, keepdims=True))
                ps.append(jnp.exp2(st[a] - mn).astype(BF16))
                if pv is not None:
                    acc[a] = (acc[a] + pv[a]) * jnp.exp2(m[a] - mn)
                m[a] = mn
            st = st_next
        pv = values(nkv - 1, ps)
        outs = []
        for a in range(2):
            tot = acc[a] + pv[a]
            outs.append(tot[0:V_B] * (1.0 / tot[V_B:V_B + 1]))
        o_ref[0, c * qw:(c + 1) * qw, :] = jnp.concatenate(outs, axis=0).T.astype(o_ref.dtype)


def _mla_attn(qt, kp, vt, b, l):
    tq, tk = 2048, 256
    k3 = kp.reshape(b, l, HEADS_B * SLOT_B)
    out = pl.pallas_call(
        functools.partial(_mla_attn_kernel, tk=tk),
        grid=(b, HEADS_B // 2, l // tq),
        in_specs=[
            pl.BlockSpec((1, 2 * SLOT_B, tq), lambda bi, p, qi: (bi, p, qi)),
            pl.BlockSpec((1, l, 2 * SLOT_B), lambda bi, p, qi: (bi, 0, p)),
            pl.BlockSpec((1, 2 * V_B, l), lambda bi, p, qi: (bi, p, 0)),
        ],
        out_specs=pl.BlockSpec((1, tq, 2 * V_B), lambda bi, p, qi: (bi, qi, p)),
        out_shape=jax.ShapeDtypeStruct((b, l, WIDTH_B), BF16),
        scratch_shapes=[pltpu.VMEM((2, l // tk, VROWS, tk), BF16)],
        compiler_params=_cparams(("parallel", "parallel", "arbitrary")),
        name="latent_attention",
    )(qt, k3, vt)
    return out.reshape(b * l, WIDTH_B)


def _hyena_proj_kernel(x_ref, prev_ref, next_ref, g_ref, w_ref, cw_ref, cb_ref, u_ref, x0_ref, *, per_seq, halo):
    tm = x_ref.shape[0]
    i = pl.program_id(0) % per_seq

    def norm(x):
        return x * lax.rsqrt(jnp.mean(x * x, axis=-1, keepdims=True) + EPS) * g_ref[...]

    h = jnp.concatenate([jnp.where(i == 0, 0.0, norm(prev_ref[...])).astype(BF16),
                         norm(x_ref[...]).astype(BF16),
                         jnp.where(i == per_seq - 1, 0.0, norm(next_ref[...])).astype(BF16)], axis=0)
    n = h.shape[0]
    y = _dot(h, w_ref[...])
    ym = pltpu.roll(y, 1, 0)[halo:halo + tm]
    yp = pltpu.roll(y, n - 1, 0)[halo:halo + tm]
    c = ym * cw_ref[0:1, :] + y[halo:halo + tm] * cw_ref[1:2, :] + yp * cw_ref[2:3, :] + cb_ref[...]
    x0_ref[...] = c[:, :WIDTH_C].astype(x0_ref.dtype)
    u_ref[...] = (c[:, 2 * WIDTH_C:] * c[:, WIDTH_C:2 * WIDTH_C]).astype(u_ref.dtype)


def _hyena_proj(x2d, gain, w_c, conv_w, conv_b, l):
    t = x2d.shape[0]
    tm, halo = 1024, 16
    per_seq = l // tm
    nh = tm // halo
    const = lambda i: (0, 0)
    return pl.pallas_call(
        functools.partial(_hyena_proj_kernel, per_seq=per_seq, halo=halo),
        grid=(t // tm,),
        in_specs=[
            pl.BlockSpec((tm, D_MODEL), lambda i: (i, 0)),
            pl.BlockSpec((halo, D_MODEL), lambda i: (jnp.maximum(i * nh - 1, 0), 0)),
            pl.BlockSpec((halo, D_MODEL), lambda i: (jnp.minimum((i + 1) * nh, t // halo - 1), 0)),
            pl.BlockSpec((1, D_MODEL), const),
            pl.BlockSpec((D_MODEL, COLS_C), const),
            pl.BlockSpec((3, COLS_C), const),
            pl.BlockSpec((1, COLS_C), const),
        ],
        out_specs=[pl.BlockSpec((tm, WIDTH_C), lambda i: (i, 0)), pl.BlockSpec((tm, WIDTH_C), lambda i: (i, 0))],
        out_shape=[jax.ShapeDtypeStruct((t, WIDTH_C), BF16), jax.ShapeDtypeStruct((t, WIDTH_C), BF16)],
        compiler_params=_cparams(("parallel",)),
        name="hyena_proj",
    )(x2d, x2d, x2d, gain, w_c, conv_w, conv_b)


def _hyena_filter_kernel(pos_ref, fcol_ref, w1t_ref, w1c_ref, w1s_ref, b1_ref, w2_ref, b2_ref, w3_ref, fr_ref,
                         dl_ref, o_ref):
    tt = pos_ref[0:1, :]
    ang = pos_ref[1:2, :]
    valid = pos_ref[2:3, :]
    arg = fcol_ref[...] * ang
    pre1 = w1t_ref[...] * tt + _dot_exact(w1c_ref[...], jnp.cos(arg)) + _dot_exact(w1s_ref[...], jnp.sin(arg))
    fr = fr_ref[...]
    hid = jnp.sin(fr * (pre1 + b1_ref[...]))
    hid = jnp.sin(fr * (_dot_exact(w2_ref[...], hid) + b2_ref[...]))
    filt = _dot_exact(w3_ref[0], hid)
    o_ref[...] = filt * jnp.exp(-dl_ref[...] * tt) * valid


def _hyena_filter(pos, fcol, w1t, w1c, w1s, b1, w2t, b2, w3t, fr, dl, l):
    n = 2048
    const = lambda j: (0, 0)
    return pl.pallas_call(
        _hyena_filter_kernel,
        grid=(2 * l // n,),
        in_specs=[
            pl.BlockSpec((3, n), lambda j: (0, j)),
            pl.BlockSpec((16, 1), const),
            pl.BlockSpec((HYENA_HID, 1), const), pl.BlockSpec((HYENA_HID, 16), const),
            pl.BlockSpec((HYENA_HID, 16), const), pl.BlockSpec((HYENA_HID, 1), const),
            pl.BlockSpec((HYENA_HID, HYENA_HID), const), pl.BlockSpec((HYENA_HID, 1), const),
            pl.BlockSpec((1, WIDTH_C, HYENA_HID), lambda j: (j // (l // n), 0, 0)),
            pl.BlockSpec((HYENA_HID, 1), const), pl.BlockSpec((WIDTH_C, 1), const),
        ],
        out_specs=pl.BlockSpec((WIDTH_C, n), lambda j: (0, j)),
        out_shape=jax.ShapeDtypeStruct((WIDTH_C, 2 * l), F32),
        compiler_params=_cparams(("parallel",)),
        name="hyena_filter",
    )(pos, fcol, w1t, w1c, w1s, b1, w2t, b2, w3t, fr, dl)


def _hyena_conv_kernel(kf_ref, u_ref, o_ref, uf_ref, acc_ref, *, nb, bsz):
    p = TOEP
    n2 = 2 * nb * p

    def channel(c, carry):
        krow = kf_ref[pl.ds(c, 1), :]
        packed_rows = bsz % 16 == 0
        if not packed_rows:
            uf_ref[...] = u_ref[c].astype(F32)
        acc_ref[...] = jnp.zeros_like(acc_ref)
        for d in range(-(nb - 1), nb):
            a0 = (d * p) % n2
            b0 = ((d - 1) * p) % n2
            seg = jnp.concatenate([krow[:, a0:a0 + p], krow[:, b0:b0 + p]], axis=1)
            rolled = pltpu.roll(jnp.broadcast_to(seg, (p, 2 * p)), 0, 1, stride=1, stride_axis=0)
            toep = rolled[:, :p].astype(BF16)
            rows = (nb - abs(d)) * bsz
            src = 0 if d >= 0 else -d * bsz
            dst = d * bsz if d >= 0 else 0
            lhs = u_ref[c, src:src + rows, :] if packed_rows else uf_ref[src:src + rows, :].astype(BF16)
            acc_ref[dst:dst + rows, :] += _dot(lhs, toep)
        o_ref[c] = acc_ref[...].astype(o_ref.dtype)
        return carry

    lax.fori_loop(0, kf_ref.shape[0], channel, 0)


def _hyena_conv(kf, u_t, nb, bsz):
    cblk = 8
    rows = nb * bsz
    return pl.pallas_call(
        functools.partial(_hyena_conv_kernel, nb=nb, bsz=bsz),
        grid=(WIDTH_C // cblk,),
        in_specs=[
            pl.BlockSpec((cblk, 2 * nb * TOEP), lambda i: (i, 0)),
            pl.BlockSpec((cblk, rows, TOEP), lambda i: (i, 0, 0)),
        ],
        out_specs=pl.BlockSpec((cblk, rows, TOEP), lambda i: (i, 0, 0)),
        out_shape=jax.ShapeDtypeStruct((WIDTH_C, rows, TOEP), BF16),
        scratch_shapes=[pltpu.VMEM((rows, TOEP), F32), pltpu.VMEM((rows, TOEP), F32)],
        compiler_params=_cparams(("parallel",)),
        name="hyena_conv",
    )(kf, u_t)


def _merge_kernel(oa0_ref, oa1_ref, oa2_ref, l0_ref, l1_ref, l2_ref, yb_ref, cv_ref, u_ref, x0_ref, pg_ref, x_ref,
                  ex_ref, skip_ref, bg_ref, wa_ref, wb_ref, wc_ref, wo_ref, o_ref):
    lses = [l0_ref[0], l1_ref[0], l2_ref[0]]
    mx = jnp.maximum(jnp.maximum(lses[0], lses[1]), lses[2])
    pad = jnp.zeros((LANE - HEADS_A, mx.shape[1]), F32)
    num = None
    den = None
    for lse, oa in zip(lses, (oa0_ref, oa1_ref, oa2_ref)):
        wt = jnp.concatenate([jnp.exp(lse - mx), pad], axis=0).T
        w = _dot(wt.astype(BF16), ex_ref[...])
        num = w * oa[...].astype(F32) if num is None else num + w * oa[...].astype(F32)
        den = w if den is None else den + w
    ya = (num / den).astype(BF16)
    u = u_ref[...].astype(F32)
    yc = (x0_ref[...].astype(F32) * (cv_ref[...].astype(F32) + skip_ref[...] * u)).astype(BF16)
    mixed = None
    for i, (y, w_ref) in enumerate(((ya, wa_ref), (yb_ref[...], wb_ref), (yc, wc_ref))):
        sl = slice(i * D_MODEL, (i + 1) * D_MODEL)
        gate = jax.nn.sigmoid(pg_ref[:, sl].astype(F32) + bg_ref[:, sl])
        term = gate * _dot(y, w_ref[...])
        mixed = term if mixed is None else mixed + term
    o_ref[...] = x_ref[...] + _dot(mixed.astype(BF16), wo_ref[...])


def _merge(oas, lses, yb, cv, u, x0, proj, x2d, expand, skip, bg, wa, wb, wc, wo, l):
    t = x2d.shape[0]
    tm = 512
    per_seq = l // tm
    row = lambda i: (i, 0)
    const = lambda i: (0, 0)
    half = pl.BlockSpec((tm, 512), row)
    lse_spec = pl.BlockSpec((1, HEADS_A, tm), lambda i: (i // per_seq, 0, i % per_seq))
    return pl.pallas_call(
        _merge_kernel,
        grid=(t // tm,),
        in_specs=[
            half, half, half,
            lse_spec, lse_spec, lse_spec,
            half, half, half, half,
            pl.BlockSpec((tm, COLS_G), lambda i: (i, OFF_G // COLS_G)),
            pl.BlockSpec((tm, D_MODEL), row),
            pl.BlockSpec((LANE, WIDTH_A), const),
            pl.BlockSpec((1, WIDTH_C), const), pl.BlockSpec((1, COLS_G), const),
            pl.BlockSpec((WIDTH_A, D_MODEL), const), pl.BlockSpec((WIDTH_B, D_MODEL), const),
            pl.BlockSpec((WIDTH_C, D_MODEL), const), pl.BlockSpec((D_MODEL, D_MODEL), const),
        ],
        out_specs=pl.BlockSpec((tm, D_MODEL), row),
        out_shape=jax.ShapeDtypeStruct((t, D_MODEL), F32),
        compiler_params=_cparams(("parallel",)),
        name="branch_merge",
    )(*oas, *lses, yb, cv, u, x0, proj, x2d, expand, skip, bg, wa, wb, wc, wo)


def _ffn_up_kernel(x_ref, prev_ref, next_ref, g_ref, wa_ref, wg_ref, cwa_ref, cwg_ref, cba_ref, cbg_ref, o_ref,
                   h_ref, *, per_seq, halo):
    tm = x_ref.shape[0]

    @pl.when(pl.program_id(1) == 0)
    def _():
        i = pl.program_id(0) % per_seq

        def norm(x):
            return x * lax.rsqrt(jnp.mean(x * x, axis=-1, keepdims=True) + EPS) * g_ref[...]

        h_ref[0:halo, :] = jnp.where(i == 0, 0.0, norm(prev_ref[...])).astype(BF16)
        h_ref[halo:halo + tm, :] = norm(x_ref[...]).astype(BF16)
        h_ref[halo + tm:, :] = jnp.where(i == per_seq - 1, 0.0, norm(next_ref[...])).astype(BF16)

    h = h_ref[...]
    n = h.shape[0]

    def conv(y, cw_ref, cb_ref):
        ym = pltpu.roll(y, 1, 0)[halo:halo + tm]
        yp = pltpu.roll(y, n - 1, 0)[halo:halo + tm]
        return ym * cw_ref[0:1, :] + y[halo:halo + tm] * cw_ref[1:2, :] + yp * cw_ref[2:3, :] + cb_ref[...]

    ya = _dot(h, wa_ref[...])
    yg = _dot(h, wg_ref[...])
    a = conv(ya, cwa_ref, cba_ref)
    g = conv(yg, cwg_ref, cbg_ref)
    e = jnp.exp2(a * (GELU_K1 + GELU_K3 * (a * a)))
    o_ref[...] = (a * g / (1.0 + e)).astype(o_ref.dtype)


def _ffn_up(x2d, gain, wa, wg, cwa, cwg, cba, cbg, l):
    t = x2d.shape[0]
    tm, tn, halo = 1024, D_FF // 2, 16
    per_seq = l // tm
    nh = tm // halo
    const = lambda i, j: (0, 0)
    col = lambda i, j: (0, j)
    return pl.pallas_call(
        functools.partial(_ffn_up_kernel, per_seq=per_seq, halo=halo),
        grid=(t // tm, D_FF // tn),
        in_specs=[
            pl.BlockSpec((tm, D_MODEL), lambda i, j: (i, 0)),
            pl.BlockSpec((halo, D_MODEL), lambda i, j: (jnp.maximum(i * nh - 1, 0), 0)),
            pl.BlockSpec((halo, D_MODEL), lambda i, j: (jnp.minimum((i + 1) * nh, t // halo - 1), 0)),
            pl.BlockSpec((1, D_MODEL), const),
            pl.BlockSpec((D_MODEL, tn), col), pl.BlockSpec((D_MODEL, tn), col),
            pl.BlockSpec((3, tn), col), pl.BlockSpec((3, tn), col),
            pl.BlockSpec((1, tn), col), pl.BlockSpec((1, tn), col),
        ],
        out_specs=pl.BlockSpec((tm, tn), lambda i, j: (i, j)),
        out_shape=jax.ShapeDtypeStruct((t, D_FF), BF16),
        scratch_shapes=[pltpu.VMEM((tm + 2 * halo, D_MODEL), BF16)],
        compiler_params=_cparams(("parallel", "arbitrary")),
        name="ffn_up",
    )(x2d, x2d, x2d, gain, wa, wg, cwa, cwg, cba, cbg)


def _ffn_down_kernel(x_ref, a_ref, w_ref, o_ref):
    o_ref[...] = x_ref[...] + _dot(a_ref[...], w_ref[...])


def _ffn_down(x2d, act, w):
    t = x2d.shape[0]
    tm = 512
    return pl.pallas_call(
        _ffn_down_kernel,
        grid=(t // tm,),
        in_specs=[
            pl.BlockSpec((tm, D_MODEL), lambda i: (i, 0)),
            pl.BlockSpec((tm, D_FF), lambda i: (i, 0)),
            pl.BlockSpec((D_FF, D_MODEL), lambda i: (0, 0)),
        ],
        out_specs=pl.BlockSpec((tm, D_MODEL), lambda i: (i, 0)),
        out_shape=jax.ShapeDtypeStruct((t, D_MODEL), F32),
        compiler_params=_cparams(("parallel",)),
        name="ffn_down",
    )(x2d, act, w)


def _layer_params(p, l):
    w_in = p['w_in'][l]
    a_end, b_end, c_end = COLS_A, COLS_A + COLS_B, COLS_A + COLS_B + COLS_C
    w_r = jnp.concatenate([w_in[:, c_end:], w_in[:, :a_end], w_in[:, a_end:b_end],
                           jnp.zeros((D_MODEL, PROJ_W - OFF_B - COLS_B), F32)], axis=1).astype(BF16)
    w_c = w_in[:, b_end:c_end].astype(BF16)
    slot_pad = SLOT_B - QK_B

    def slots(w):
        return jnp.pad(w, [(0, 0)] * (w.ndim - 1) + [(0, slot_pad)]).reshape(*w.shape[:-2], HEADS_B * SLOT_B)

    wq = slots(p['b_w_uq'][l].reshape(Q_RANK, HEADS_B, QK_B)).astype(BF16)
    wkv = p['b_w_ukv'][l].reshape(KV_RANK, HEADS_B, NOPE_B + V_B)
    wk_nope = jnp.pad(wkv[:, :, :NOPE_B], ((0, 0), (0, 0), (0, SLOT_B - NOPE_B))).reshape(KV_RANK, -1)
    place = jnp.pad(jnp.eye(ROPE_B, dtype=F32), ((0, 0), (NOPE_B, SLOT_B - QK_B)))
    wk_pe = jnp.tile(place, (1, HEADS_B))
    wk = jnp.concatenate([wk_nope, wk_pe, jnp.zeros((256 - KV_RANK - ROPE_B, HEADS_B * SLOT_B), F32)], axis=0)
    wv = wkv[:, :, NOPE_B:].reshape(KV_RANK, WIDTH_B)
    gq = jnp.tile(jnp.pad(p['b_q_g'][l], (0, slot_pad)), HEADS_B)[None] * (QK_B ** -0.5 * LOG2E)
    gk = jnp.tile(jnp.pad(p['b_k_g'][l], (0, slot_pad)), HEADS_B)[None]
    w_up = p['w_up'][l]
    cw = p['ffn_conv_w'][l]
    cb = p['ffn_conv_b'][l]
    return dict(
        norm_attn_g=p['norm_attn_g'][l][None], w_r=w_r, w_c=w_c,
        a_qg=[jnp.tile(p['a_q_g'][l, g], 4)[None] * (HEAD_DIM_A ** -0.5 * LOG2E) for g in range(3)],
        a_kg=[jnp.tile(p['a_k_g'][l, g], 4)[None] for g in range(3)],
        gql=p['b_q_lat_g'][l][None], gkl=p['b_kv_lat_g'][l][None],
        wq=wq, wk=wk.astype(BF16), wv=wv.astype(BF16), gq=gq, gk=gk,
        c_conv_w=p['c_conv_w'][l], c_conv_b=p['c_conv_b'][l][None],
        w1t=p['c_w1'][l][0:1].T, w1c=p['c_w1'][l][1:17].T, w1s=-p['c_w1'][l][17:33].T,
        b1=p['c_b1'][l][:, None], w2t=p['c_w2'][l].T, b2=p['c_b2'][l][:, None],
        w3t=p['c_w3'][l].T.reshape(2, WIDTH_C, HYENA_HID), fr=p['c_freq'][l][:, None],
        skip=p['c_skip'][l][None], bg=p['b_gate'][l][None],
        wa=p['w_br_a'][l].astype(BF16), wb=p['w_br_b'][l].astype(BF16), wc=p['w_br_c'][l].astype(BF16),
        wo=p['w_out'][l].astype(BF16), norm_ffn_g=p['norm_ffn_g'][l][None],
        w_up_a=w_up[:, :D_FF].astype(BF16), w_up_g=w_up[:, D_FF:].astype(BF16),
        cwa=cw[:, :D_FF], cwg=cw[:, D_FF:], cba=cb[None, :D_FF], cbg=cb[None, D_FF:],
        w_down=p['w_down'][l].astype(BF16),
    )


def _seq_constants(l):
    half = ROPE_B // 2
    pos = jnp.arange(l, dtype=F32)
    inv = ROPE_THETA ** (-jnp.arange(half, dtype=F32) / half)
    ang = pos[:, None] * inv[None, :]
    cos, sin = jnp.cos(ang), jnp.sin(ang)
    one = jnp.ones((l, NOPE_B), F32)
    zn = jnp.zeros((l, NOPE_B), F32)
    zp = jnp.zeros((l, SLOT_B - QK_B), F32)
    rc = jnp.concatenate([one, cos, cos, zp], axis=1)
    rs = jnp.concatenate([zn, -sin, sin, zp], axis=1)
    m = jnp.arange(2 * l)
    p_idx = jnp.where(m < l, m, 2 * l - m)
    tlin = jnp.linspace(0.0, 1.0, l, dtype=F32)
    tt = tlin[jnp.minimum(p_idx, l - 1)]
    angf = (2.0 * math.pi / l) * p_idx.astype(F32)
    valid = (m != l).astype(F32)
    bands = (HYENA_EMB - 1) // 2
    fcol = jnp.linspace(1e-4, bands - 1, bands, dtype=F32)[:, None]
    deltas = jnp.abs(jnp.linspace(math.log(HYENA_TARGET) / HYENA_FAST_DECAY,
                                  math.log(HYENA_TARGET) / HYENA_SLOW_DECAY, WIDTH_C, dtype=F32))[:, None]
    return dict(rc=rc, rs=rs, pos=jnp.stack([tt, angf, valid]), fcol=fcol, deltas=deltas)


def _static_tables():
    expand = (jnp.arange(LANE)[:, None] == (jnp.arange(WIDTH_A) // HEAD_DIM_A)[None, :]).astype(BF16)
    slopes = jnp.exp2(-ALIBI_MAX * (jnp.arange(HEADS_A, dtype=F32) + 1.0) / HEADS_A)
    return expand, slopes


def _layer(x2d, lp, sc, expand, slopes, b, l):
    proj = _inproj(x2d, lp['norm_attn_g'], lp['w_r'])
    oas, lses = [], []
    for g, dil in enumerate(DILATIONS):
        o, lse = _mixer_a_group(proj, lp['a_qg'][g], lp['a_kg'][g], -slopes * (dil * LOG2E), b, l, g, dil)
        oas.append(o)
        lses.append(lse)
    qp, kp, vt = _mla_prep(proj, lp['gql'], lp['gkl'], lp['wq'], lp['wk'], lp['wv'], lp['gq'], lp['gk'],
                           sc['rc'], sc['rs'], b, l)
    yb = _mla_attn(qp, kp, vt, b, l)
    u, x0 = _hyena_proj(x2d, lp['norm_attn_g'], lp['w_c'], lp['c_conv_w'], lp['c_conv_b'], l)
    kf = _hyena_filter(sc['pos'], sc['fcol'], lp['w1t'], lp['w1c'], lp['w1s'], lp['b1'], lp['w2t'], lp['b2'],
                       lp['w3t'], lp['fr'], sc['deltas'], l)
    nb = l // TOEP
    u_t = jnp.transpose(u.reshape(b, nb, TOEP, WIDTH_C), (3, 1, 0, 2)).reshape(WIDTH_C, nb * b, TOEP)
    cv_t = _hyena_conv(kf, u_t, nb, b)
    cv = jnp.transpose(cv_t.reshape(WIDTH_C, nb, b, TOEP), (2, 1, 3, 0)).reshape(b * l, WIDTH_C)
    xm = _merge(oas, lses, yb, cv, u, x0, proj, x2d, expand, lp['skip'], lp['bg'],
                lp['wa'], lp['wb'], lp['wc'], lp['wo'], l)
    act = _ffn_up(xm, lp['norm_ffn_g'], lp['w_up_a'], lp['w_up_g'], lp['cwa'], lp['cwg'], lp['cba'], lp['cbg'], l)
    return _ffn_down(xm, act, lp['w_down'])


def _trunk(x, layer_params, expand, slopes):
    b, l, _ = x.shape
    sc = _seq_constants(l)
    y = x.reshape(b * l, D_MODEL)
    for lp in layer_params:
        y = _layer(y, lp, sc, expand, slopes, b, l)
    return y.reshape(b, l, D_MODEL)


def kernel(x_prompt, x_sample, norm_attn_g, w_in, b_gate, a_q_g, a_k_g, b_q_lat_g, b_kv_lat_g, b_w_uq, b_w_ukv, b_q_g, b_k_g, c_conv_w, c_conv_b, c_w1, c_b1, c_w2, c_b2, c_w3, c_freq, c_skip, w_br_a, w_br_b, w_br_c, w_out, norm_ffn_g, w_up, ffn_conv_w, ffn_conv_b, w_down):
    p = dict(norm_attn_g=norm_attn_g, w_in=w_in, b_gate=b_gate, a_q_g=a_q_g, a_k_g=a_k_g, b_q_lat_g=b_q_lat_g,
             b_kv_lat_g=b_kv_lat_g, b_w_uq=b_w_uq, b_w_ukv=b_w_ukv, b_q_g=b_q_g, b_k_g=b_k_g, c_conv_w=c_conv_w,
             c_conv_b=c_conv_b, c_w1=c_w1, c_b1=c_b1, c_w2=c_w2, c_b2=c_b2, c_w3=c_w3, c_freq=c_freq,
             c_skip=c_skip, w_br_a=w_br_a, w_br_b=w_br_b, w_br_c=w_br_c, w_out=w_out, norm_ffn_g=norm_ffn_g,
             w_up=w_up, ffn_conv_w=ffn_conv_w, ffn_conv_b=ffn_conv_b, w_down=w_down)
    layer_params = [_layer_params(p, l) for l in range(DEPTH)]
    expand, slopes = _static_tables()
    return _trunk(x_prompt, layer_params, expand, slopes), _trunk(x_sample, layer_params, expand, slopes)
```
